```python
import math
import jax, jax.numpy as jnp
from jax import lax
import numpy as np

D_MODEL = 1024
BATCH = 4
SEQ = 4096
DEPTH = 4

D_MIX = D_MODEL
EPS = 1e-5
ATTN_HEADS = 4
ATTN_QK_DIM = 64
ATTN_V_DIM = 2 * ATTN_QK_DIM
ATTN_WIDTH = ATTN_HEADS * ATTN_V_DIM
Q_BLOCK = 128
ROPE_THETA = 500000.0
ROT_DIM = ATTN_QK_DIM // 4
SC_WIDTH = 256
SC_CONV = 3
SSD_HEADS = 4
SSD_HEAD_DIM = 64
SSD_WIDTH = SSD_HEADS * SSD_HEAD_DIM
SSD_GROUPS = 2
SSD_STATE = 128
SSD_HEADS_PER_GROUP = SSD_HEADS // SSD_GROUPS
SSD_CONV = 3
SSD_CHUNK = 128
SSD_XBC = SSD_WIDTH + 2 * SSD_GROUPS * SSD_STATE
D_FF = 2816
FFN_CONV = 3

Q_COLS = ATTN_HEADS * 2 * ATTN_QK_DIM
K_COLS = Q_COLS
V_COLS = ATTN_WIDTH
_SEG = [Q_COLS, K_COLS, V_COLS, SC_WIDTH, SC_WIDTH, SC_WIDTH, SSD_WIDTH, SSD_XBC, SSD_HEADS, SSD_HEADS]
SPLITS = [int(v) for v in np.cumsum(_SEG)[:-1]]
IN_COLS = int(sum(_SEG))

kernel_name = "hymba_style_bidir_diffattn_shortconv_ssd"

F32 = jnp.float32


def rms_norm(x, g):
    xf = x.astype(F32)
    y = xf * lax.rsqrt(jnp.mean(xf * xf, axis=-1, keepdims=True) + EPS)
    return (y * g.astype(F32)).astype(x.dtype)


def conv_centered(t, w, bias=None):
    K = w.shape[0]
    pad = K // 2
    S = t.shape[1]
    tp = jnp.pad(t, ((0, 0), (pad, pad), (0, 0)))
    out = tp[:, 0:S] * w[0]
    for k in range(1, K):
        out = out + tp[:, k:k + S] * w[k]
    if bias is not None:
        out = out + bias
    return out


def partial_rotary(t, cos, sin):
    half = ROT_DIM // 2
    c = cos[:, :, None, None, :].astype(t.dtype)
    s = sin[:, :, None, None, :].astype(t.dtype)
    t1 = t[..., :half]
    t2 = t[..., half:ROT_DIM]
    return jnp.concatenate([t1 * c - t2 * s, t2 * c + t1 * s, t[..., ROT_DIM:]], axis=-1)


def diff_attention(q, k, v, lam, lam_init, subln_g):
    b, S, h, _, d = q.shape
    nq = S // Q_BLOCK
    scale = d ** -0.5
    qb = q.reshape(b, nq, Q_BLOCK, h, 2, d).transpose(1, 0, 2, 3, 4, 5)

    def block(qi):
        s = jnp.einsum('bqhcd,bkhcd->bhcqk', qi, k).astype(F32) * scale
        p = jax.nn.softmax(s, axis=-1)
        w = p[:, :, 0] - lam * p[:, :, 1]
        return jnp.einsum('bhqk,bkhe->bqhe', w.astype(v.dtype), v)

    o = lax.map(block, qb)
    o = o.transpose(1, 0, 2, 3, 4).reshape(b, S, h, ATTN_V_DIM)
    o = rms_norm(o, subln_g) * (1.0 - lam_init)
    return o.reshape(b, S, h * ATTN_V_DIM)


def ssd_scan(x, dt, A, Bm, Cm):
    b, S, h, p = x.shape
    n = Bm.shape[-1]
    c = S // SSD_CHUNK
    L = SSD_CHUNK
    xd = (x * dt[..., None]).reshape(b, c, L, h, p)
    a = (dt * A).reshape(b, c, L, h).transpose(0, 3, 1, 2)
    Bc = Bm.reshape(b, c, L, h, n)
    Cc = Cm.reshape(b, c, L, h, n)
    a_cs = jnp.cumsum(a, axis=-1)
    mask = jnp.tril(jnp.ones((L, L), dtype=bool))
    seg = a_cs[..., :, None] - a_cs[..., None, :]
    Lmat = jnp.where(mask, jnp.exp(jnp.where(mask, seg, 0.0)), 0.0)
    scores = jnp.einsum('bclhn,bcshn->bhcls', Cc, Bc) * Lmat
    y_diag = jnp.einsum('bhcls,bcshp->bclhp', scores, xd)
    decay_states = jnp.exp(a_cs[..., -1:] - a_cs)
    states = jnp.einsum('bclhn,bhcl,bclhp->bchpn', Bc, decay_states, xd)
    chunk_decay = jnp.exp(a_cs[..., -1])

    def step(hstate, inp):
        st, dec = inp
        return hstate * dec[..., None, None] + st, hstate

    h0 = jnp.zeros((b, h, p, n), dtype=states.dtype)
    _, prev = lax.scan(step, h0, (states.transpose(1, 0, 2, 3, 4), chunk_decay.transpose(2, 0, 1)))
    prev = prev.transpose(1, 0, 2, 3, 4)
    y_off = jnp.einsum('bclhn,bchpn,bhcl->bclhp', Cc, prev, jnp.exp(a_cs))
    return (y_diag + y_off).reshape(b, S, h, p)


def ssd_mixer(xbc, z, dt_f, dt_b, conv_w, conv_b, dt_bias, a_log, d_skip, norm_g):
    b, S, _ = xbc.shape
    xbc = jax.nn.silu(conv_centered(xbc, conv_w, conv_b))
    xs, Bm, Cm = jnp.split(xbc, [SSD_WIDTH, SSD_WIDTH + SSD_GROUPS * SSD_STATE], axis=-1)
    xs = xs.reshape(b, S, SSD_HEADS, SSD_HEAD_DIM).astype(F32)
    Bm = jnp.repeat(Bm.reshape(b, S, SSD_GROUPS, SSD_STATE), SSD_HEADS_PER_GROUP, axis=2).astype(F32)
    Cm = jnp.repeat(Cm.reshape(b, S, SSD_GROUPS, SSD_STATE), SSD_HEADS_PER_GROUP, axis=2).astype(F32)
    A = -jnp.exp(a_log.astype(F32))
    dtb = dt_bias.astype(F32)
    dtf = jax.nn.softplus(dt_f.astype(F32) + dtb[0])
    dtr = jax.nn.softplus(dt_b.astype(F32) + dtb[1])
    flip = lambda t: jnp.flip(t, axis=1)
    y_fwd = ssd_scan(xs, dtf, A[0], Bm, Cm)
    y_bwd = flip(ssd_scan(flip(xs), flip(dtr), A[1], flip(Bm), flip(Cm)))
    y = y_fwd + y_bwd + d_skip.astype(F32)[:, None] * xs
    y = y.reshape(b, S, SSD_WIDTH) * jax.nn.silu(z.astype(F32))
    gs = SSD_WIDTH // SSD_GROUPS
    y = rms_norm(y.reshape(b, S, SSD_GROUPS, gs), norm_g.reshape(SSD_GROUPS, gs))
    return y.reshape(b, S, SSD_WIDTH).astype(z.dtype)


def setup_inputs(seed: int = 0) -> dict:
    key = jax.random.key(seed)
    ks = jax.random.split(key, 24)
    nrm = lambda k, shape, s: jax.random.normal(k, shape, dtype=F32) * s
    x = jax.random.normal(ks[0], (BATCH, SEQ, D_MODEL), dtype=F32)
    offs = jax.random.randint(ks[1], (BATCH, 1), 0, 1024, dtype=jnp.int32)
    positions = (jnp.arange(SEQ, dtype=jnp.int32)[None, :] + offs).astype(jnp.int32)
    dt0 = jnp.exp(jax.random.uniform(ks[10], (DEPTH, 2, SSD_HEADS), minval=math.log(1e-3), maxval=math.log(1e-1)))
    return {
        "x": x,
        "positions": positions,
        "norm_mix_g": 1.0 + nrm(ks[2], (DEPTH, D_MODEL), 0.02),
        "w_in": nrm(ks[3], (DEPTH, D_MODEL, IN_COLS), D_MODEL ** -0.5),
        "lam_q1": nrm(ks[4], (DEPTH, ATTN_QK_DIM), 0.1),
        "lam_k1": nrm(ks[5], (DEPTH, ATTN_QK_DIM), 0.1),
        "lam_q2": nrm(ks[6], (DEPTH, ATTN_QK_DIM), 0.1),
        "lam_k2": nrm(ks[7], (DEPTH, ATTN_QK_DIM), 0.1),
        "subln_g": 1.0 + nrm(ks[8], (DEPTH, ATTN_V_DIM), 0.02),
        "sc_conv_w": nrm(ks[9], (DEPTH, SC_CONV, SC_WIDTH), SC_CONV ** -0.5),
        "ssd_conv_w": nrm(ks[11], (DEPTH, SSD_CONV, SSD_XBC), SSD_CONV ** -0.5),
        "ssd_conv_b": nrm(ks[12], (DEPTH, SSD_XBC), 0.02),
        "ssd_dt_bias": dt0 + jnp.log(-jnp.expm1(-dt0)),
        "ssd_a_log": jnp.log(jax.random.uniform(ks[13], (DEPTH, 2, SSD_HEADS), minval=1.0, maxval=16.0)),
        "ssd_d": 1.0 + nrm(ks[14], (DEPTH, SSD_HEADS), 0.02),
        "ssd_norm_g": 1.0 + nrm(ks[15], (DEPTH, SSD_WIDTH), 0.02),
        "w_out": nrm(ks[16], (DEPTH, D_MIX, D_MODEL), D_MIX ** -0.5),
        "norm_ffn_g": 1.0 + nrm(ks[17], (DEPTH, D_MODEL), 0.02),
        "w_up": nrm(ks[18], (DEPTH, D_MODEL, 2 * D_FF), D_MODEL ** -0.5),
        "ffn_conv_w": nrm(ks[19], (DEPTH, FFN_CONV, 2 * D_FF), FFN_CONV ** -0.5),
        "ffn_conv_b": nrm(ks[20], (DEPTH, 2 * D_FF), 0.02),
        "w_down": nrm(ks[21], (DEPTH, D_FF, D_MODEL), D_FF ** -0.5),
        "final_norm_g": 1.0 + nrm(ks[22], (D_MODEL,), 0.02),
    }


def reference(x, positions, norm_mix_g, w_in, lam_q1, lam_k1, lam_q2, lam_k2, subln_g, sc_conv_w,
              ssd_conv_w, ssd_conv_b, ssd_dt_bias, ssd_a_log, ssd_d, ssd_norm_g, w_out,
              norm_ffn_g, w_up, ffn_conv_w, ffn_conv_b, w_down, final_norm_g):
    b, S, _ = x.shape
    inv_freq = ROPE_THETA ** (-jnp.arange(0, ROT_DIM, 2, dtype=F32) / ROT_DIM)
    ang = positions.astype(F32)[..., None] * inv_freq
    cos, sin = jnp.cos(ang), jnp.sin(ang)

    for l in range(DEPTH):
        lam_init = 0.8 - 0.6 * math.exp(-0.3 * l)
        h = rms_norm(x, norm_mix_g[l])
        proj = h @ w_in[l]
        q, k, v, sc_b, sc_c, sc_h, z, xbc, dt_f, dt_b = jnp.split(proj, SPLITS, axis=-1)

        q = partial_rotary(q.reshape(b, S, ATTN_HEADS, 2, ATTN_QK_DIM), cos, sin)
        k = partial_rotary(k.reshape(b, S, ATTN_HEADS, 2, ATTN_QK_DIM), cos, sin)
        v = v.reshape(b, S, ATTN_HEADS, ATTN_V_DIM)
        lam = (jnp.exp(jnp.sum(lam_q1[l].astype(F32) * lam_k1[l].astype(F32)))
               - jnp.exp(jnp.sum(lam_q2[l].astype(F32) * lam_k2[l].astype(F32))) + lam_init)
        y_attn = diff_attention(q, k, v, lam, lam_init, subln_g[l])

        y_conv = sc_b * conv_centered(sc_c * sc_h, sc_conv_w[l])

        y_ssd = ssd_mixer(xbc, z, dt_f, dt_b, ssd_conv_w[l], ssd_conv_b[l], ssd_dt_bias[l],
                          ssd_a_log[l], ssd_d[l], ssd_norm_g[l])

        y = jnp.concatenate([y_attn, y_conv.astype(y_attn.dtype), y_ssd.astype(y_attn.dtype)], axis=-1)
        x = x + (y @ w_out[l]).astype(x.dtype)

        h = rms_norm(x, norm_ffn_g[l])
        u = conv_centered(h @ w_up[l], ffn_conv_w[l], ffn_conv_b[l])
        g, u = jnp.split(u, 2, axis=-1)
        x = x + ((jax.nn.silu(g) * u) @ w_down[l]).astype(x.dtype)

    return rms_norm(x, final_norm_g)
```

```python
import functools
import math

import jax
import jax.numpy as jnp
from jax import lax
from jax.experimental import pallas as pl
from jax.experimental.pallas import tpu as pltpu

F32 = jnp.float32
BF16 = jnp.bfloat16

D_MODEL = 1024
EPS = 1e-5
N_ATTN_HEADS = 4
QK_DIM = 64
V_DIM = 128
ATTN_W = N_ATTN_HEADS * V_DIM
ROPE_THETA = 500000.0
ROT_DIM = QK_DIM // 4
SC_W = 256
SSD_W = 256
SSD_STATE = 128
SSD_GROUPS = 2
SSD_HEADS = 4
SSD_CHUNK = 128
SSD_XBC = SSD_W + 2 * SSD_GROUPS * SSD_STATE
D_FF = 2816
IN_COLS = 3336

QKV_COLS = 3 * ATTN_W
REST_REAL = IN_COLS - QKV_COLS
REST_W = 1920
IN_PAD = QKV_COLS + REST_W
R_SCB, R_SCC, R_SCH, R_Z, R_XBC, R_DT = 0, 256, 512, 768, 1024, 1792

LANES = 128
SUBLANES = 8
BF16_SUBLANES = 16
VMEM_LIMIT = 56 * 1024 * 1024

FF_CHUNK = 256
N_FF_CHUNKS = D_FF // FF_CHUNK
HALO = BF16_SUBLANES


def _sigmoid(x):
    return 1.0 / (1.0 + jnp.exp(-x))


def _rms(x, g):
    ms = jnp.mean(x * x, axis=-1, keepdims=True)
    return x * lax.rsqrt(ms + EPS) * g


def _conv3_rows(ext, w, lo, n):
    tot = ext.shape[0]
    up = pltpu.roll(ext, 1, 0)
    dn = pltpu.roll(ext, tot - 1, 0)
    out = up[lo:lo + n] * w[0:1]
    out = out + ext[lo:lo + n] * w[1:2]
    out = out + dn[lo:lo + n] * w[2:3]
    return out


def _rope_kernel(pos_ref, invf_ref, o_ref):
    pos = pos_ref[...].astype(F32)
    ang = pos * invf_ref[...]
    lane = lax.broadcasted_iota(jnp.int32, ang.shape, 1) & (QK_DIM - 1)
    c = jnp.cos(ang)
    s = jnp.sin(ang)
    half = ROT_DIM // 2
    o_ref[:, 0:LANES] = c
    o_ref[:, LANES:2 * LANES] = jnp.where(lane < half, -s, 0.0)
    o_ref[:, 2 * LANES:3 * LANES] = jnp.where((lane >= half) & (lane < ROT_DIM), s, 0.0)


def _rope_tables(positions):
    t = positions.size
    tm = min(t, 2048)
    half = ROT_DIM // 2
    inv_freq = ROPE_THETA ** (-jnp.arange(0, ROT_DIM, 2, dtype=F32) / ROT_DIM)
    lane = jnp.arange(LANES) % QK_DIM
    invf = jnp.where(lane < ROT_DIM, inv_freq[lane % half], 0.0).astype(F32)[None, :]
    return pl.pallas_call(
        _rope_kernel,
        grid=(t // tm,),
        in_specs=[pl.BlockSpec((tm, 1), lambda i: (i, 0)),
                  pl.BlockSpec((1, LANES), lambda i: (0, 0))],
        out_specs=pl.BlockSpec((tm, 3 * LANES), lambda i: (i, 0)),
        out_shape=jax.ShapeDtypeStruct((t, 3 * LANES), F32),
        name="rope_tables",
    )(positions.reshape(t, 1), invf)


def _inproj_kernel(x_ref, g_ref, w_ref, rope_ref, q_ref, k_ref, v_ref, r_ref, h_scr):
    h_scr[...] = _rms(x_ref[...], g_ref[...]).astype(BF16)
    c = rope_ref[:, 0:LANES]
    s1 = rope_ref[:, LANES:2 * LANES]
    s2 = rope_ref[:, 2 * LANES:3 * LANES]
    half = ROT_DIM // 2

    def rot(t):
        return t * c + pltpu.roll(t, LANES - half, 1) * s1 + pltpu.roll(t, half, 1) * s2

    cw = 2 * LANES
    for ci in range(2 * ATTN_W // cw):
        r = jnp.dot(h_scr[...], w_ref[:, ci * cw:(ci + 1) * cw], preferred_element_type=F32)
        for hf in range(2):
            t = rot(r[:, hf * LANES:(hf + 1) * LANES])
            col = ci * cw + hf * LANES
            if col < ATTN_W:
                q_ref[:, col:col + LANES] = (t * (QK_DIM ** -0.5)).astype(BF16)
            else:
                k_ref[:, col - ATTN_W:col - ATTN_W + LANES] = t.astype(BF16)
    for ci in range(ATTN_W // cw):
        c0 = 2 * ATTN_W + ci * cw
        r = jnp.dot(h_scr[...], w_ref[:, c0:c0 + cw], preferred_element_type=F32)
        v_ref[:, ci * cw:(ci + 1) * cw] = r.astype(BF16)
    c0 = 0
    while c0 < REST_W:
        w = min(cw, REST_W - c0)
        r_ref[:, c0:c0 + w] = jnp.dot(h_scr[...], w_ref[:, QKV_COLS + c0:QKV_COLS + c0 + w],
                                      preferred_element_type=F32)
        c0 += w


def _in_proj(x, g, w_in, rope, layer, tm):
    t = x.shape[0]
    return pl.pallas_call(
        _inproj_kernel,
        grid=(t // tm,),
        in_specs=[pl.BlockSpec((tm, D_MODEL), lambda i: (i, 0)),
                  pl.BlockSpec((None, 1, D_MODEL), lambda i: (layer, 0, 0)),
                  pl.BlockSpec((None, D_MODEL, IN_PAD), lambda i: (layer, 0, 0)),
                  pl.BlockSpec((tm, 3 * LANES), lambda i: (i, 0))],
        out_specs=[pl.BlockSpec((tm, ATTN_W), lambda i: (i, 0)),
                   pl.BlockSpec((tm, ATTN_W), lambda i: (i, 0)),
                   pl.BlockSpec((tm, ATTN_W), lambda i: (i, 0)),
                   pl.BlockSpec((tm, REST_W), lambda i: (i, 0))],
        out_shape=[jax.ShapeDtypeStruct((t, ATTN_W), BF16),
                   jax.ShapeDtypeStruct((t, ATTN_W), BF16),
                   jax.ShapeDtypeStruct((t, ATTN_W), BF16),
                   jax.ShapeDtypeStruct((t, REST_W), F32)],
        scratch_shapes=[pltpu.VMEM((tm, D_MODEL), BF16)],
        compiler_params=pltpu.CompilerParams(dimension_semantics=("arbitrary",),
                                             vmem_limit_bytes=VMEM_LIMIT),
        name="in_proj",
    )(x, g, w_in, rope)


def _attn_kernel(lq1_ref, lk1_ref, lq2_ref, lk2_ref, sg_ref, q_ref, k_ref, v_ref, o_ref,
                 qp_scr, m_scr, l_scr, acc_scr, *, lam_init, tq, tk):
    seq = k_ref.shape[0]
    lam = (jnp.exp(jnp.sum(lq1_ref[...] * lk1_ref[...], axis=-1, keepdims=True))
           - jnp.exp(jnp.sum(lq2_ref[...] * lk2_ref[...], axis=-1, keepdims=True)) + lam_init)
    lane = lax.broadcasted_iota(jnp.int32, (tq, V_DIM), 1)
    nt = (((1,), (1,)), ((), ()))

    def q_tile(qi, carry):
        q0 = pl.multiple_of(qi * tq, tq)
        q = q_ref[pl.ds(q0, tq), :]
        zero = jnp.zeros_like(q)
        qp_scr[0:tq, :] = jnp.where(lane < QK_DIM, q, zero)
        qp_scr[tq:2 * tq, :] = jnp.where(lane >= QK_DIM, q, zero)
        m_scr[...] = jnp.full(m_scr.shape, -jnp.inf, F32)
        l_scr[...] = jnp.zeros(l_scr.shape, F32)
        acc_scr[...] = jnp.zeros(acc_scr.shape, F32)

        def kv_step(j, c2):
            k0 = pl.multiple_of(j * tk, tk)
            kb = k_ref[pl.ds(k0, tk), :]
            vb = v_ref[pl.ds(k0, tk), :]
            s = lax.dot_general(qp_scr[...], kb, nt, preferred_element_type=F32)
            m_old = m_scr[...]
            m_new = jnp.maximum(m_old, jnp.max(s, axis=-1, keepdims=True))
            alpha = jnp.exp(m_old - m_new)
            p = jnp.exp(s - m_new)
            l_scr[...] = alpha * l_scr[...] + jnp.sum(p, axis=-1, keepdims=True)
            acc_scr[...] = alpha * acc_scr[...] + jnp.dot(p.astype(BF16), vb, preferred_element_type=F32)
            m_scr[...] = m_new
            return c2

        lax.fori_loop(0, seq // tk, kv_step, 0)
        o = acc_scr[...] / l_scr[...]
        o = o[0:tq] - lam * o[tq:2 * tq]
        y = _rms(o, sg_ref[...]) * (1.0 - lam_init)
        o_ref[pl.ds(q0, tq), :] = y.astype(BF16)
        return carry

    lax.fori_loop(0, seq // tq, q_tile, 0)


def _attention(q, k, v, lq1, lk1, lq2, lk2, subg, layer, lam_init, tq, tk):
    b, s, _ = q.shape
    small = lambda n: pl.BlockSpec((None, 1, n), lambda bi, hi: (layer, 0, 0))
    head = pl.BlockSpec((None, s, V_DIM), lambda bi, hi: (bi, 0, hi))
    return pl.pallas_call(
        functools.partial(_attn_kernel, lam_init=lam_init, tq=tq, tk=tk),
        grid=(b, N_ATTN_HEADS),
        in_specs=[small(QK_DIM), small(QK_DIM), small(QK_DIM), small(QK_DIM), small(V_DIM), head, head, head],
        out_specs=head,
        out_shape=jax.ShapeDtypeStruct((b, s, ATTN_W), BF16),
        scratch_shapes=[pltpu.VMEM((2 * tq, V_DIM), BF16),
                        pltpu.VMEM((2 * tq, 1), F32),
                        pltpu.VMEM((2 * tq, 1), F32),
                        pltpu.VMEM((2 * tq, V_DIM), F32)],
        compiler_params=pltpu.CompilerParams(dimension_semantics=("arbitrary", "arbitrary"),
                                             vmem_limit_bytes=VMEM_LIMIT),
        name="diff_attn",
    )(lq1, lk1, lq2, lk2, subg, q, k, v)


def _ssd_kernel(dtb_ref, alog_ref, dexp_ref, ng_ref, scw_ref, cw_ref, cb_ref, main_ref, prev_ref, next_ref,
                o_ref, ybwd_scr, h_scr, *, rows, nblk):
    sweep = pl.program_id(1)
    i = pl.program_id(2)
    blk = jnp.where(sweep == 0, nblk - 1 - i, i)
    has_prev = blk > 0
    has_next = blk < nblk - 1
    L = SSD_CHUNK
    nsub = rows // L

    @pl.when(i == 0)
    def _():
        h_scr[...] = jnp.zeros(h_scr.shape, F32)

    def ext_cols(c0, c1):
        pv = jnp.where(has_prev, prev_ref[:, c0:c1], 0.0)
        nx = jnp.where(has_next, next_ref[:, c0:c1], 0.0)
        return jnp.concatenate([pv, main_ref[:, c0:c1], nx], axis=0)

    xbc = _conv3_rows(ext_cols(R_XBC, R_XBC + SSD_XBC), cw_ref[...], SUBLANES, rows) + cb_ref[...]
    xbc = xbc * _sigmoid(xbc)
    lane8 = lax.broadcasted_iota(jnp.int32, (1, LANES), 1) < 2 * SSD_HEADS
    xdt = main_ref[:, R_DT:R_DT + LANES] + dtb_ref[...]
    dt = jnp.maximum(xdt, 0.0) + jnp.log1p(jnp.exp(-jnp.abs(xdt)))
    a_all = dt * jnp.where(lane8, -jnp.exp(alog_ref[...]), 0.0)

    rowi = lax.broadcasted_iota(jnp.int32, (L, L), 0)
    coli = lax.broadcasted_iota(jnp.int32, (L, L), 1)
    lo = lax.broadcasted_iota(jnp.int32, (L, LANES), 1) < LANES // 2
    nt = (((1,), (1,)), ((), ()))

    def pick(arr, j0):
        return jnp.where(lo[0:arr.shape[0]], arr[:, j0:j0 + 1], arr[:, j0 + 1:j0 + 2])

    def chunk(r0, fwd):
        mask = (coli <= rowi) if fwd else (coli >= rowi)
        tmat = mask.astype(BF16)
        a = a_all[r0:r0 + L]
        a_hi = a.astype(BF16)
        r1 = a - a_hi.astype(F32)
        a_mid = r1.astype(BF16)
        a_lo = (r1 - a_mid.astype(F32)).astype(BF16)
        cs = (jnp.dot(tmat, a_hi, preferred_element_type=F32)
              + jnp.dot(tmat, a_mid, preferred_element_type=F32)
              + jnp.dot(tmat, a_lo, preferred_element_type=F32))
        cst = cs.T
        tot = cs[L - 1:L] if fwd else cs[0:1]
        ecs = jnp.exp(cs)
        dsd = jnp.exp(tot - cs)
        etot = jnp.exp(tot)
        outs = []
        for g in range(SSD_GROUPS):
            j0 = 2 * g + (0 if fwd else SSD_HEADS)
            xs_g = xbc[r0:r0 + L, g * LANES:(g + 1) * LANES]
            b_g = xbc[r0:r0 + L, SSD_W + g * SSD_STATE:SSD_W + (g + 1) * SSD_STATE]
            c_g = xbc[r0:r0 + L, SSD_W + (SSD_GROUPS + g) * SSD_STATE:SSD_W + (SSD_GROUPS + g + 1) * SSD_STATE]
            c_bf = c_g.astype(BF16)
            xd = xs_g * pick(dt[r0:r0 + L], j0)
            gm = lax.dot_general(c_bf, b_g.astype(BF16), nt, preferred_element_type=F32)

            def decay(j):
                seg = cs[:, j:j + 1] - cst[j:j + 1, :]
                return jnp.where(mask, jnp.exp(seg), 0.0)

            m0 = (gm * decay(j0)).astype(BF16)
            m1 = (gm * decay(j0 + 1)).astype(BF16)
            y = (jnp.dot(m0, jnp.where(lo, xd, 0.0).astype(BF16), preferred_element_type=F32)
                 + jnp.dot(m1, jnp.where(lo, 0.0, xd).astype(BF16), preferred_element_type=F32))
            hg = h_scr[g]
            y = y + jnp.dot(c_bf, hg.astype(BF16), preferred_element_type=F32) * pick(ecs, j0)
            xdw = (xd * pick(dsd, j0)).astype(BF16)
            st = jnp.dot(b_g.T.astype(BF16), xdw, preferred_element_type=F32)
            h_scr[g] = hg * pick(etot, j0) + st
            outs.append((y, xs_g))
        return outs

    @pl.when(sweep == 0)
    def _():
        for si in reversed(range(nsub)):
            r0 = si * L
            outs = chunk(r0, False)
            g0 = pl.multiple_of(blk * rows + r0, L)
            for g in range(SSD_GROUPS):
                ybwd_scr[pl.ds(g0, L), g * LANES:(g + 1) * LANES] = outs[g][0]

    @pl.when(sweep == 1)
    def _():
        u = ext_cols(R_SCC, R_SCC + SC_W) * ext_cols(R_SCH, R_SCH + SC_W)
        yc = main_ref[:, R_SCB:R_SCB + SC_W] * _conv3_rows(u, scw_ref[...], SUBLANES, rows)
        o_ref[:, 0:SC_W] = yc.astype(BF16)
        for si in range(nsub):
            r0 = si * L
            outs = chunk(r0, True)
            g0 = pl.multiple_of(blk * rows + r0, L)
            for g in range(SSD_GROUPS):
                y, xs_g = outs[g]
                gl = slice(g * LANES, (g + 1) * LANES)
                y = y + ybwd_scr[pl.ds(g0, L), gl] + dexp_ref[:, gl] * xs_g
                z = main_ref[r0:r0 + L, R_Z + g * LANES:R_Z + (g + 1) * LANES]
                y = y * (z * _sigmoid(z))
                o_ref[r0:r0 + L, SC_W + g * LANES:SC_W + (g + 1) * LANES] = _rms(y, ng_ref[:, gl]).astype(BF16)


def _ssd_mixers(rest, dtb, alog, dexp, ng, scw, cw, cb, layer, batch, seq, rows):
    t = rest.shape[0]
    nblk = seq // rows
    r8 = rows // SUBLANES

    def blk_of(s, i):
        return jnp.where(s == 0, nblk - 1 - i, i)

    small = lambda r, n: pl.BlockSpec((None, r, n), lambda bi, s, i: (layer, 0, 0))
    return pl.pallas_call(
        functools.partial(_ssd_kernel, rows=rows, nblk=nblk),
        grid=(batch, 2, nblk),
        in_specs=[small(1, LANES), small(1, LANES), small(1, SSD_W), small(1, SSD_W),
                  small(3, SC_W), small(3, SSD_XBC), small(1, SSD_XBC),
                  pl.BlockSpec((rows, REST_W), lambda bi, s, i: (bi * nblk + blk_of(s, i), 0)),
                  pl.BlockSpec((SUBLANES, REST_W),
                               lambda bi, s, i: (jnp.maximum((bi * nblk + blk_of(s, i)) * r8 - 1, 0), 0)),
                  pl.BlockSpec((SUBLANES, REST_W),
                               lambda bi, s, i: (jnp.minimum((bi * nblk + blk_of(s, i) + 1) * r8,
                                                             t // SUBLANES - 1), 0))],
        out_specs=pl.BlockSpec((rows, SC_W + SSD_W), lambda bi, s, i: (bi * nblk + s * i, 0)),
        out_shape=jax.ShapeDtypeStruct((t, SC_W + SSD_W), BF16),
        scratch_shapes=[pltpu.VMEM((seq, SSD_W), F32),
                        pltpu.VMEM((SSD_GROUPS, SSD_STATE, LANES), F32)],
        compiler_params=pltpu.CompilerParams(dimension_semantics=("arbitrary", "arbitrary", "arbitrary"),
                                             vmem_limit_bytes=VMEM_LIMIT),
        name="ssd_mixers",
    )(dtb, alog, dexp, ng, scw, cw, cb, rest, rest, rest)


def _ffn_kernel(x_ref, xp_ref, xn_ref, ya_ref, yap_ref, yan_ref, ym_ref, ymp_ref, ymn_ref,
                wout_ref, g_ref, wg_ref, wu_ref, cwg_ref, cwu_ref, cbg_ref, cbu_ref, wd_ref, fg_ref,
                o_ref, h_scr, xnew_scr, acc_scr, *, tm, tiles_per_seq, final_norm):
    ti = pl.program_id(0) % tiles_per_seq
    has_prev = ti > 0
    has_next = ti < tiles_per_seq - 1
    ext_rows = tm + 2 * HALO

    def ext(prev, main, nxt):
        return jnp.concatenate([prev[...], main[...], nxt[...]], axis=0)

    mix = (jnp.dot(ext(yap_ref, ya_ref, yan_ref), wout_ref[0:ATTN_W, :], preferred_element_type=F32)
           + jnp.dot(ext(ymp_ref, ym_ref, ymn_ref), wout_ref[ATTN_W:2 * ATTN_W, :], preferred_element_type=F32))
    xnew = ext(xp_ref, x_ref, xn_ref) + mix
    row = lax.broadcasted_iota(jnp.int32, (ext_rows, 1), 0)
    valid = (row >= jnp.where(has_prev, 0, HALO)) & (row < jnp.where(has_next, ext_rows, HALO + tm))
    h_scr[...] = jnp.where(valid, _rms(xnew, g_ref[...]), 0.0).astype(BF16)
    xnew_scr[...] = xnew[HALO:HALO + tm]
    acc_scr[...] = jnp.zeros(acc_scr.shape, F32)

    def ff_chunk(j, carry):
        hb = h_scr[...]
        ug = jnp.dot(hb, wg_ref[j], preferred_element_type=F32)
        uu = jnp.dot(hb, wu_ref[j], preferred_element_type=F32)
        cg = _conv3_rows(ug, cwg_ref[j], HALO, tm) + cbg_ref[j]
        cu = _conv3_rows(uu, cwu_ref[j], HALO, tm) + cbu_ref[j]
        act = (cg * _sigmoid(cg)) * cu
        acc_scr[...] += jnp.dot(act.astype(BF16), wd_ref[j], preferred_element_type=F32)
        return carry

    lax.fori_loop(0, N_FF_CHUNKS, ff_chunk, 0)
    out = xnew_scr[...] + acc_scr[...]
    if final_norm:
        out = _rms(out, fg_ref[...])
    o_ref[...] = out


def _ffn(x, ya, ym, w_out, g, wg, wu, cwg, cwu, cbg, cbu, wd, fg, layer, seq, tm, final_norm):
    t = x.shape[0]
    tiles_per_seq = seq // tm
    hb = tm // HALO
    nh = t // HALO

    def main(w):
        return pl.BlockSpec((tm, w), lambda i: (i, 0))

    def prev(w):
        return pl.BlockSpec((HALO, w), lambda i: (jnp.maximum(i * hb - 1, 0), 0))

    def nxt(w):
        return pl.BlockSpec((HALO, w), lambda i: (jnp.minimum((i + 1) * hb, nh - 1), 0))

    def resident(shape):
        nd = len(shape)
        return pl.BlockSpec((None,) + shape, lambda i: (layer,) + (0,) * nd,
                            pipeline_mode=pl.Buffered(1))

    ext_rows = tm + 2 * HALO
    return pl.pallas_call(
        functools.partial(_ffn_kernel, tm=tm, tiles_per_seq=tiles_per_seq, final_norm=final_norm),
        grid=(t // tm,),
        in_specs=[main(D_MODEL), prev(D_MODEL), nxt(D_MODEL),
                  main(ATTN_W), prev(ATTN_W), nxt(ATTN_W),
                  main(SC_W + SSD_W), prev(SC_W + SSD_W), nxt(SC_W + SSD_W),
                  resident((D_MODEL, D_MODEL)),
                  resident((1, D_MODEL)),
                  resident((N_FF_CHUNKS, D_MODEL, FF_CHUNK)),
                  resident((N_FF_CHUNKS, D_MODEL, FF_CHUNK)),
                  resident((N_FF_CHUNKS, 3, FF_CHUNK)),
                  resident((N_FF_CHUNKS, 3, FF_CHUNK)),
                  resident((N_FF_CHUNKS, 1, FF_CHUNK)),
                  resident((N_FF_CHUNKS, 1, FF_CHUNK)),
                  resident((N_FF_CHUNKS, FF_CHUNK, D_MODEL)),
                  pl.BlockSpec((1, D_MODEL), lambda i: (0, 0))],
        out_specs=main(D_MODEL),
        out_shape=jax.ShapeDtypeStruct((t, D_MODEL), F32),
        scratch_shapes=[pltpu.VMEM((ext_rows, D_MODEL), BF16),
                        pltpu.VMEM((tm, D_MODEL), F32),
                        pltpu.VMEM((tm, D_MODEL), F32)],
        compiler_params=pltpu.CompilerParams(dimension_semantics=("arbitrary",),
                                             vmem_limit_bytes=VMEM_LIMIT),
        name="ffn",
    )(x, x, x, ya, ya, ya, ym, ym, ym, w_out, g, wg, wu, cwg, cwu, cbg, cbu, wd, fg)


def kernel(x, positions, norm_mix_g, w_in, lam_q1, lam_k1, lam_q2, lam_k2, subln_g, sc_conv_w, ssd_conv_w,
           ssd_conv_b, ssd_dt_bias, ssd_a_log, ssd_d, ssd_norm_g, w_out, norm_ffn_g, w_up, ffn_conv_w,
           ffn_conv_b, w_down, final_norm_g):
    batch, seq, _ = x.shape
    depth = w_in.shape[0]
    t = batch * seq
    tm = min(512, seq)
    tq = min(256, seq)
    tk = min(512, seq)
    ssd_rows = min(256, seq)

    w_in_b = jnp.pad(w_in, ((0, 0), (0, 0), (0, IN_PAD - IN_COLS))).astype(BF16)
    w_out_b = w_out.astype(BF16)
    w_up_b = w_up.astype(BF16).reshape(depth, D_MODEL, 2, N_FF_CHUNKS, FF_CHUNK)
    wg = w_up_b[:, :, 0].transpose(0, 2, 1, 3)
    wu = w_up_b[:, :, 1].transpose(0, 2, 1, 3)
    wd = w_down.astype(BF16).reshape(depth, N_FF_CHUNKS, FF_CHUNK, D_MODEL)
    fcw = ffn_conv_w.reshape(depth, 3, 2, N_FF_CHUNKS, FF_CHUNK)
    cwg = fcw[:, :, 0].transpose(0, 2, 1, 3)
    cwu = fcw[:, :, 1].transpose(0, 2, 1, 3)
    fcb = ffn_conv_b.reshape(depth, 2, N_FF_CHUNKS, 1, FF_CHUNK)
    cbg, cbu = fcb[:, 0], fcb[:, 1]
    row3 = lambda a: a.reshape(depth, 1, -1)
    lane_pad = lambda a: jnp.pad(a.reshape(depth, 1, 2 * SSD_HEADS), ((0, 0), (0, 0), (0, LANES - 2 * SSD_HEADS)))
    dtb = lane_pad(ssd_dt_bias)
    alog = lane_pad(ssd_a_log)
    dexp = jnp.repeat(ssd_d, SSD_W // SSD_HEADS, axis=-1).reshape(depth, 1, SSD_W)
    fg = final_norm_g.reshape(1, D_MODEL)

    rope = _rope_tables(positions)
    xf = x.reshape(t, D_MODEL)
    for l in range(depth):
        lam_init = 0.8 - 0.6 * math.exp(-0.3 * l)
        q, k, v, rest = _in_proj(xf, row3(norm_mix_g), w_in_b, rope, l, tm)
        ya = _attention(q.reshape(batch, seq, ATTN_W), k.reshape(batch, seq, ATTN_W),
                        v.reshape(batch, seq, ATTN_W), row3(lam_q1), row3(lam_k1), row3(lam_q2), row3(lam_k2),
                        row3(subln_g), l, lam_init, tq, tk).reshape(t, ATTN_W)
        ym = _ssd_mixers(rest, dtb, alog, dexp, row3(ssd_norm_g), sc_conv_w, ssd_conv_w, row3(ssd_conv_b),
                         l, batch, seq, ssd_rows)
        xf = _ffn(xf, ya, ym, w_out_b, row3(norm_ffn_g), wg, wu, cwg, cwu, cbg, cbu, wd, fg,
                  l, seq, tm, l == depth - 1)
    return xf.reshape(batch, seq, D_MODEL)
```

```python
import functools
import math

import jax
import jax.numpy as jnp
from jax import lax
from jax.experimental import pallas as pl
from jax.experimental.pallas import tpu as pltpu

F32 = jnp.float32
BF16 = jnp.bfloat16

D_MODEL = 1024
EPS = 1e-5
N_ATTN_HEADS = 4
QK_DIM = 64
V_DIM = 128
ATTN_W = N_ATTN_HEADS * V_DIM
ROPE_THETA = 500000.0
ROT_DIM = QK_DIM // 4
Q_SCALE = QK_DIM ** -0.5 * math.log2(math.e)
SC_W = 256
SSD_W = 256
SSD_STATE = 128
SSD_GROUPS = 2
SSD_HEADS = 4
SSD_CHUNK = 128
SSD_XBC = SSD_W + 2 * SSD_GROUPS * SSD_STATE
D_FF = 2816
IN_COLS = 3336

QKV_COLS = 3 * ATTN_W
REST_REAL = IN_COLS - QKV_COLS
REST_W = 1920
IN_PAD = QKV_COLS + REST_W
R_SCB, R_SCC, R_SCH, R_Z, R_XBC, R_DT = 0, 256, 512, 768, 1024, 1792

LANES = 128
SUBLANES = 8
BF16_SUBLANES = 16
VMEM_LIMIT = 56 * 1024 * 1024

FF_CHUNK = 256
N_FF_CHUNKS = D_FF // FF_CHUNK
HALO = BF16_SUBLANES


def _sigmoid(x):
    return 1.0 / (1.0 + jnp.exp(-x))


def _rms(x, g):
    ms = jnp.mean(x * x, axis=-1, keepdims=True)
    return x * lax.rsqrt(ms + EPS) * g


def _conv3_rows(ext, w, lo, n):
    tot = ext.shape[0]
    up = pltpu.roll(ext, 1, 0)
    dn = pltpu.roll(ext, tot - 1, 0)
    out = up[lo:lo + n] * w[0:1]
    out = out + ext[lo:lo + n] * w[1:2]
    out = out + dn[lo:lo + n] * w[2:3]
    return out


def _rope_kernel(pos_ref, invf_ref, o_ref):
    pos = pos_ref[...].astype(F32)
    ang = pos * invf_ref[...]
    lane = lax.broadcasted_iota(jnp.int32, ang.shape, 1) & (QK_DIM - 1)
    c = jnp.cos(ang)
    s = jnp.sin(ang)
    half = ROT_DIM // 2
    o_ref[:, 0:LANES] = c
    o_ref[:, LANES:2 * LANES] = jnp.where(lane < half, -s, 0.0)
    o_ref[:, 2 * LANES:3 * LANES] = jnp.where((lane >= half) & (lane < ROT_DIM), s, 0.0)


def _rope_tables(positions):
    t = positions.size
    tm = min(t, 2048)
    half = ROT_DIM // 2
    inv_freq = ROPE_THETA ** (-jnp.arange(0, ROT_DIM, 2, dtype=F32) / ROT_DIM)
    lane = jnp.arange(LANES) % QK_DIM
    invf = jnp.where(lane < ROT_DIM, inv_freq[lane % half], 0.0).astype(F32)[None, :]
    return pl.pallas_call(
        _rope_kernel,
        grid=(t // tm,),
        in_specs=[pl.BlockSpec((tm, 1), lambda i: (i, 0)),
                  pl.BlockSpec((1, LANES), lambda i: (0, 0))],
        out_specs=pl.BlockSpec((tm, 3 * LANES), lambda i: (i, 0)),
        out_shape=jax.ShapeDtypeStruct((t, 3 * LANES), F32),
        name="rope_tables",
    )(positions.reshape(t, 1), invf)


def _inproj_kernel(x_ref, g_ref, w_ref, rope_ref, q_ref, k_ref, v_ref, r_ref, h_scr):
    h_scr[...] = _rms(x_ref[...], g_ref[...]).astype(BF16)
    c = rope_ref[:, 0:LANES]
    s1 = rope_ref[:, LANES:2 * LANES]
    s2 = rope_ref[:, 2 * LANES:3 * LANES]
    half = ROT_DIM // 2

    def rot(t):
        return t * c + pltpu.roll(t, LANES - half, 1) * s1 + pltpu.roll(t, half, 1) * s2

    cw = 2 * LANES
    for ci in range(2 * ATTN_W // cw):
        r = jnp.dot(h_scr[...], w_ref[:, ci * cw:(ci + 1) * cw], preferred_element_type=F32)
        for hf in range(2):
            t = rot(r[:, hf * LANES:(hf + 1) * LANES])
            col = ci * cw + hf * LANES
            if col < ATTN_W:
                q_ref[:, col:col + LANES] = (t * Q_SCALE).astype(BF16)
            else:
                k_ref[:, col - ATTN_W:col - ATTN_W + LANES] = t.astype(BF16)
    for ci in range(ATTN_W // cw):
        c0 = 2 * ATTN_W + ci * cw
        r = jnp.dot(h_scr[...], w_ref[:, c0:c0 + cw], preferred_element_type=F32)
        v_ref[ci * cw:(ci + 1) * cw, :] = r.T.astype(BF16)
    c0 = 0
    while c0 < REST_W:
        w = min(cw, REST_W - c0)
        r_ref[:, c0:c0 + w] = jnp.dot(h_scr[...], w_ref[:, QKV_COLS + c0:QKV_COLS + c0 + w],
                                      preferred_element_type=F32)
        c0 += w


def _in_proj(x, g, w_in, rope, layer, tm):
    t = x.shape[0]
    return pl.pallas_call(
        _inproj_kernel,
        grid=(t // tm,),
        in_specs=[pl.BlockSpec((tm, D_MODEL), lambda i: (i, 0)),
                  pl.BlockSpec((None, 1, D_MODEL), lambda i: (layer, 0, 0)),
                  pl.BlockSpec((None, D_MODEL, IN_PAD), lambda i: (layer, 0, 0)),
                  pl.BlockSpec((tm, 3 * LANES), lambda i: (i, 0))],
        out_specs=[pl.BlockSpec((tm, ATTN_W), lambda i: (i, 0)),
                   pl.BlockSpec((tm, ATTN_W), lambda i: (i, 0)),
                   pl.BlockSpec((None, ATTN_W, tm), lambda i: (i, 0, 0)),
                   pl.BlockSpec((tm, REST_W), lambda i: (i, 0))],
        out_shape=[jax.ShapeDtypeStruct((t, ATTN_W), BF16),
                   jax.ShapeDtypeStruct((t, ATTN_W), BF16),
                   jax.ShapeDtypeStruct((t // tm, ATTN_W, tm), BF16),
                   jax.ShapeDtypeStruct((t, REST_W), F32)],
        scratch_shapes=[pltpu.VMEM((tm, D_MODEL), BF16)],
        compiler_params=pltpu.CompilerParams(dimension_semantics=("arbitrary",),
                                             vmem_limit_bytes=VMEM_LIMIT),
        name="in_proj",
    )(x, g, w_in, rope)


def _attn_kernel(lq1_ref, lk1_ref, lq2_ref, lk2_ref, sg_ref, q_ref, k_ref, vt_ref, o_ref,
                 qp_scr, acc_scr, st_scr, *, lam_init, tq):
    seq = k_ref.shape[0]
    nk, _, tk = vt_ref.shape
    lam = (jnp.exp(jnp.sum(lq1_ref[...] * lk1_ref[...], axis=-1, keepdims=True))
           - jnp.exp(jnp.sum(lq2_ref[...] * lk2_ref[...], axis=-1, keepdims=True)) + lam_init)
    lane = lax.broadcasted_iota(jnp.int32, (tq, V_DIM), 1)
    nt = (((1,), (1,)), ((), ()))

    def q_tile(qi, carry):
        q0 = pl.multiple_of(qi * tq, tq)
        q = q_ref[pl.ds(q0, tq), :]
        zero = jnp.zeros_like(q)
        qp_scr[0:tq, :] = jnp.where(lane < QK_DIM, q, zero)
        qp_scr[tq:2 * tq, :] = jnp.where(lane >= QK_DIM, q, zero)
        acc_scr[...] = jnp.zeros(acc_scr.shape, F32)

        def scores(j):
            st = lax.dot_general(k_ref[j * tk:(j + 1) * tk, :], qp_scr[...], nt, preferred_element_type=F32)
            st_scr[j % 2] = st
            return jnp.max(st, axis=0, keepdims=True)

        m_run = jnp.full((1, 2 * tq), -jnp.inf, F32)
        l_run = jnp.zeros((1, 2 * tq), F32)
        cmax = scores(0)
        for j in range(nk):
            cmax_next = scores(j + 1) if j + 1 < nk else None
            m_new = jnp.maximum(m_run, cmax)
            alpha = jnp.exp2(m_run - m_new)
            p = jnp.exp2(st_scr[j % 2] - m_new)
            l_run = alpha * l_run + jnp.sum(p, axis=0, keepdims=True)
            acc_scr[...] = alpha * acc_scr[...] + jnp.dot(vt_ref[j], p.astype(BF16), preferred_element_type=F32)
            m_run, cmax = m_new, cmax_next
        o = acc_scr[...] / l_run
        o = o[:, 0:tq] - lam * o[:, tq:2 * tq]
        ms = jnp.mean(o * o, axis=0, keepdims=True)
        y = o * lax.rsqrt(ms + EPS) * sg_ref[...] * (1.0 - lam_init)
        o_ref[pl.ds(q0, tq), :] = y.T.astype(BF16)
        return carry

    lax.fori_loop(0, seq // tq, q_tile, 0)


def _attention(q, k, vt, lq1, lk1, lq2, lk2, subg_col, layer, lam_init, batch, seq, tq):
    t = q.shape[0]
    tk = vt.shape[2]
    small = lambda n: pl.BlockSpec((None, 1, n), lambda bi, hi: (layer, 0, 0))
    head = pl.BlockSpec((seq, V_DIM), lambda bi, hi: (bi, hi))
    return pl.pallas_call(
        functools.partial(_attn_kernel, lam_init=lam_init, tq=tq),
        grid=(batch, N_ATTN_HEADS),
        in_specs=[small(QK_DIM), small(QK_DIM), small(QK_DIM), small(QK_DIM),
                  pl.BlockSpec((None, V_DIM, 1), lambda bi, hi: (layer, 0, 0)),
                  head, head,
                  pl.BlockSpec((seq // tk, V_DIM, tk), lambda bi, hi: (bi, hi, 0))],
        out_specs=head,
        out_shape=jax.ShapeDtypeStruct((t, ATTN_W), BF16),
        scratch_shapes=[pltpu.VMEM((2 * tq, V_DIM), BF16),
                        pltpu.VMEM((V_DIM, 2 * tq), F32),
                        pltpu.VMEM((2, tk, 2 * tq), F32)],
        compiler_params=pltpu.CompilerParams(dimension_semantics=("arbitrary", "arbitrary"),
                                             vmem_limit_bytes=VMEM_LIMIT),
        name="diff_attn",
    )(lq1, lk1, lq2, lk2, subg_col, q, k, vt)


def _ssd_kernel(dtb_ref, alog_ref, dexp_ref, ng_ref, scw_ref, cw_ref, cb_ref, main_ref, prev_ref, next_ref,
                o_ref, ybwd_scr, h_scr, *, rows, nblk):
    sweep = pl.program_id(1)
    i = pl.program_id(2)
    blk = jnp.where(sweep == 0, nblk - 1 - i, i)
    has_prev = blk > 0
    has_next = blk < nblk - 1
    L = SSD_CHUNK
    nsub = rows // L

    @pl.when(i == 0)
    def _():
        h_scr[...] = jnp.zeros(h_scr.shape, F32)

    def ext_cols(c0, c1):
        pv = jnp.where(has_prev, prev_ref[:, c0:c1], 0.0)
        nx = jnp.where(has_next, next_ref[:, c0:c1], 0.0)
        return jnp.concatenate([pv, main_ref[:, c0:c1], nx], axis=0)

    xbc = _conv3_rows(ext_cols(R_XBC, R_XBC + SSD_XBC), cw_ref[...], SUBLANES, rows) + cb_ref[...]
    xbc = xbc * _sigmoid(xbc)
    lane8 = lax.broadcasted_iota(jnp.int32, (1, LANES), 1) < 2 * SSD_HEADS
    xdt = main_ref[:, R_DT:R_DT + LANES] + dtb_ref[...]
    dt = jnp.maximum(xdt, 0.0) + jnp.log1p(jnp.exp(-jnp.abs(xdt)))
    a_all = dt * jnp.where(lane8, -jnp.exp(alog_ref[...]), 0.0)

    rowi = lax.broadcasted_iota(jnp.int32, (L, L), 0)
    coli = lax.broadcasted_iota(jnp.int32, (L, L), 1)
    lo = lax.broadcasted_iota(jnp.int32, (L, LANES), 1) < LANES // 2
    nt = (((1,), (1,)), ((), ()))

    def pick(arr, j0):
        return jnp.where(lo[0:arr.shape[0]], arr[:, j0:j0 + 1], arr[:, j0 + 1:j0 + 2])

    def chunk(r0, fwd):
        mask = (coli <= rowi) if fwd else (coli >= rowi)
        tmat = mask.astype(BF16)
        a = a_all[r0:r0 + L]
        a_hi = a.astype(BF16)
        r1 = a - a_hi.astype(F32)
        a_mid = r1.astype(BF16)
        a_lo = (r1 - a_mid.astype(F32)).astype(BF16)
        cs = (jnp.dot(tmat, a_hi, preferred_element_type=F32)
              + jnp.dot(tmat, a_mid, preferred_element_type=F32)
              + jnp.dot(tmat, a_lo, preferred_element_type=F32))
        cst = cs.T
        tot = cs[L - 1:L] if fwd else cs[0:1]
        ecs = jnp.exp(cs)
        dsd = jnp.exp(tot - cs)
        etot = jnp.exp(tot)
        outs = []
        for g in range(SSD_GROUPS):
            j0 = 2 * g + (0 if fwd else SSD_HEADS)
            xs_g = xbc[r0:r0 + L, g * LANES:(g + 1) * LANES]
            b_g = xbc[r0:r0 + L, SSD_W + g * SSD_STATE:SSD_W + (g + 1) * SSD_STATE]
            c_g = xbc[r0:r0 + L, SSD_W + (SSD_GROUPS + g) * SSD_STATE:SSD_W + (SSD_GROUPS + g + 1) * SSD_STATE]
            c_bf = c_g.astype(BF16)
            xd = xs_g * pick(dt[r0:r0 + L], j0)
            gm = lax.dot_general(c_bf, b_g.astype(BF16), nt, preferred_element_type=F32)

            def decay(j):
                seg = cs[:, j:j + 1] - cst[j:j + 1, :]
                return jnp.where(mask, jnp.exp(seg), 0.0)

            m0 = (gm * decay(j0)).astype(BF16)
            m1 = (gm * decay(j0 + 1)).astype(BF16)
            y = (jnp.dot(m0, jnp.where(lo, xd, 0.0).astype(BF16), preferred_element_type=F32)
                 + jnp.dot(m1, jnp.where(lo, 0.0, xd).astype(BF16), preferred_element_type=F32))
            hg = h_scr[g]
            y = y + jnp.dot(c_bf, hg.astype(BF16), preferred_element_type=F32) * pick(ecs, j0)
            xdw = (xd * pick(dsd, j0)).astype(BF16)
            st = jnp.dot(b_g.T.astype(BF16), xdw, preferred_element_type=F32)
            h_scr[g] = hg * pick(etot, j0) + st
            outs.append((y, xs_g))
        return outs

    @pl.when(sweep == 0)
    def _():
        for si in reversed(range(nsub)):
            r0 = si * L
            outs = chunk(r0, False)
            g0 = pl.multiple_of(blk * rows + r0, L)
            for g in range(SSD_GROUPS):
                ybwd_scr[pl.ds(g0, L), g * LANES:(g + 1) * LANES] = outs[g][0]

    @pl.when(sweep == 1)
    def _():
        u = ext_cols(R_SCC, R_SCC + SC_W) * ext_cols(R_SCH, R_SCH + SC_W)
        yc = main_ref[:, R_SCB:R_SCB + SC_W] * _conv3_rows(u, scw_ref[...], SUBLANES, rows)
        o_ref[:, 0:SC_W] = yc.astype(BF16)
        for si in range(nsub):
            r0 = si * L
            outs = chunk(r0, True)
            g0 = pl.multiple_of(blk * rows + r0, L)
            for g in range(SSD_GROUPS):
                y, xs_g = outs[g]
                gl = slice(g * LANES, (g + 1) * LANES)
                y = y + ybwd_scr[pl.ds(g0, L), gl] + dexp_ref[:, gl] * xs_g
                z = main_ref[r0:r0 + L, R_Z + g * LANES:R_Z + (g + 1) * LANES]
                y = y * (z * _sigmoid(z))
                o_ref[r0:r0 + L, SC_W + g * LANES:SC_W + (g + 1) * LANES] = _rms(y, ng_ref[:, gl]).astype(BF16)


def _ssd_mixers(rest, dtb, alog, dexp, ng, scw, cw, cb, layer, batch, seq, rows):
    t = rest.shape[0]
    nblk = seq // rows
    r8 = rows // SUBLANES

    def blk_of(s, i):
        return jnp.where(s == 0, nblk - 1 - i, i)

    small = lambda r, n: pl.BlockSpec((None, r, n), lambda bi, s, i: (layer, 0, 0))
    return pl.pallas_call(
        functools.partial(_ssd_kernel, rows=rows, nblk=nblk),
        grid=(batch, 2, nblk),
        in_specs=[small(1, LANES), small(1, LANES), small(1, SSD_W), small(1, SSD_W),
                  small(3, SC_W), small(3, SSD_XBC), small(1, SSD_XBC),
                  pl.BlockSpec((rows, REST_W), lambda bi, s, i: (bi * nblk + blk_of(s, i), 0)),
                  pl.BlockSpec((SUBLANES, REST_W),
                               lambda bi, s, i: (jnp.maximum((bi * nblk + blk_of(s, i)) * r8 - 1, 0), 0)),
                  pl.BlockSpec((SUBLANES, REST_W),
                               lambda bi, s, i: (jnp.minimum((bi * nblk + blk_of(s, i) + 1) * r8,
                                                             t // SUBLANES - 1), 0))],
        out_specs=pl.BlockSpec((rows, SC_W + SSD_W), lambda bi, s, i: (bi * nblk + s * i, 0)),
        out_shape=jax.ShapeDtypeStruct((t, SC_W + SSD_W), BF16),
        scratch_shapes=[pltpu.VMEM((seq, SSD_W), F32),
                        pltpu.VMEM((SSD_GROUPS, SSD_STATE, LANES), F32)],
        compiler_params=pltpu.CompilerParams(dimension_semantics=("arbitrary", "arbitrary", "arbitrary"),
                                             vmem_limit_bytes=VMEM_LIMIT),
        name="ssd_mixers",
    )(dtb, alog, dexp, ng, scw, cw, cb, rest, rest, rest)


def _ffn_kernel(x_ref, xp_ref, xn_ref, ya_ref, yap_ref, yan_ref, ym_ref, ymp_ref, ymn_ref,
                wout_ref, g_ref, wg_ref, wu_ref, cwg_ref, cwu_ref, cbg_ref, cbu_ref, wd_ref, fg_ref,
                o_ref, h_scr, xnew_scr, acc_scr, ua_scr, ub_scr, *, tm, tiles_per_seq, final_norm):
    ti = pl.program_id(0) % tiles_per_seq
    has_prev = ti > 0
    has_next = ti < tiles_per_seq - 1
    ext_rows = tm + 2 * HALO

    def ext(prev, main, nxt):
        return jnp.concatenate([prev[...], main[...], nxt[...]], axis=0)

    mix = (jnp.dot(ext(yap_ref, ya_ref, yan_ref), wout_ref[0:ATTN_W, :], preferred_element_type=F32)
           + jnp.dot(ext(ymp_ref, ym_ref, ymn_ref), wout_ref[ATTN_W:2 * ATTN_W, :], preferred_element_type=F32))
    xnew = ext(xp_ref, x_ref, xn_ref) + mix
    row = lax.broadcasted_iota(jnp.int32, (ext_rows, 1), 0)
    valid = (row >= jnp.where(has_prev, 0, HALO)) & (row < jnp.where(has_next, ext_rows, HALO + tm))
    h_scr[...] = jnp.where(valid, _rms(xnew, g_ref[...]), 0.0).astype(BF16)
    xnew_scr[...] = xnew[HALO:HALO + tm]
    acc_scr[...] = jnp.zeros(acc_scr.shape, F32)

    def up_proj(j, u_ref):
        hb = h_scr[...]
        u_ref[0] = jnp.dot(hb, wg_ref[j], preferred_element_type=F32)
        u_ref[1] = jnp.dot(hb, wu_ref[j], preferred_element_type=F32)

    def gate_down(j, u_ref):
        cg = _conv3_rows(u_ref[0], cwg_ref[j], HALO, tm) + cbg_ref[j]
        cu = _conv3_rows(u_ref[1], cwu_ref[j], HALO, tm) + cbu_ref[j]
        act = (cg * _sigmoid(cg)) * cu
        acc_scr[...] += jnp.dot(act.astype(BF16), wd_ref[j], preferred_element_type=F32)

    def ff_pair(i, carry):
        j = 2 * i
        up_proj(j + 1, ub_scr)
        gate_down(j, ua_scr)
        up_proj(j + 2, ua_scr)
        gate_down(j + 1, ub_scr)
        return carry

    assert N_FF_CHUNKS % 2 == 1
    up_proj(0, ua_scr)
    lax.fori_loop(0, N_FF_CHUNKS // 2, ff_pair, 0)
    gate_down(N_FF_CHUNKS - 1, ua_scr)
    out = xnew_scr[...] + acc_scr[...]
    if final_norm:
        out = _rms(out, fg_ref[...])
    o_ref[...] = out


def _ffn(x, ya, ym, w_out, g, wg, wu, cwg, cwu, cbg, cbu, wd, fg, layer, seq, tm, final_norm):
    t = x.shape[0]
    tiles_per_seq = seq // tm
    hb = tm // HALO
    nh = t // HALO

    def main(w):
        return pl.BlockSpec((tm, w), lambda i: (i, 0))

    def prev(w):
        return pl.BlockSpec((HALO, w), lambda i: (jnp.maximum(i * hb - 1, 0), 0))

    def nxt(w):
        return pl.BlockSpec((HALO, w), lambda i: (jnp.minimum((i + 1) * hb, nh - 1), 0))

    def resident(shape):
        nd = len(shape)
        return pl.BlockSpec((None,) + shape, lambda i: (layer,) + (0,) * nd,
                            pipeline_mode=pl.Buffered(1))

    ext_rows = tm + 2 * HALO
    return pl.pallas_call(
        functools.partial(_ffn_kernel, tm=tm, tiles_per_seq=tiles_per_seq, final_norm=final_norm),
        grid=(t // tm,),
        in_specs=[main(D_MODEL), prev(D_MODEL), nxt(D_MODEL),
                  main(ATTN_W), prev(ATTN_W), nxt(ATTN_W),
                  main(SC_W + SSD_W), prev(SC_W + SSD_W), nxt(SC_W + SSD_W),
                  resident((D_MODEL, D_MODEL)),
                  resident((1, D_MODEL)),
                  resident((N_FF_CHUNKS, D_MODEL, FF_CHUNK)),
                  resident((N_FF_CHUNKS, D_MODEL, FF_CHUNK)),
                  resident((N_FF_CHUNKS, 3, FF_CHUNK)),
                  resident((N_FF_CHUNKS, 3, FF_CHUNK)),
                  resident((N_FF_CHUNKS, 1, FF_CHUNK)),
                  resident((N_FF_CHUNKS, 1, FF_CHUNK)),
                  resident((N_FF_CHUNKS, FF_CHUNK, D_MODEL)),
                  pl.BlockSpec((1, D_MODEL), lambda i: (0, 0))],
        out_specs=main(D_MODEL),
        out_shape=jax.ShapeDtypeStruct((t, D_MODEL), F32),
        scratch_shapes=[pltpu.VMEM((ext_rows, D_MODEL), BF16),
                        pltpu.VMEM((tm, D_MODEL), F32),
                        pltpu.VMEM((tm, D_MODEL), F32),
                        pltpu.VMEM((2, ext_rows, FF_CHUNK), F32),
                        pltpu.VMEM((2, ext_rows, FF_CHUNK), F32)],
        compiler_params=pltpu.CompilerParams(dimension_semantics=("arbitrary",),
                                             vmem_limit_bytes=VMEM_LIMIT),
        name="ffn",
    )(x, x, x, ya, ya, ya, ym, ym, ym, w_out, g, wg, wu, cwg, cwu, cbg, cbu, wd, fg)


def kernel(x, positions, norm_mix_g, w_in, lam_q1, lam_k1, lam_q2, lam_k2, subln_g, sc_conv_w, ssd_conv_w,
           ssd_conv_b, ssd_dt_bias, ssd_a_log, ssd_d, ssd_norm_g, w_out, norm_ffn_g, w_up, ffn_conv_w,
           ffn_conv_b, w_down, final_norm_g):
    batch, seq, _ = x.shape
    depth = w_in.shape[0]
    t = batch * seq
    tm = min(512, seq)
    tq = min(256, seq)
    ssd_rows = min(256, seq)

    w_in_b = jnp.pad(w_in, ((0, 0), (0, 0), (0, IN_PAD - IN_COLS))).astype(BF16)
    w_out_b = w_out.astype(BF16)
    w_up_b = w_up.astype(BF16).reshape(depth, D_MODEL, 2, N_FF_CHUNKS, FF_CHUNK)
    wg = w_up_b[:, :, 0].transpose(0, 2, 1, 3)
    wu = w_up_b[:, :, 1].transpose(0, 2, 1, 3)
    wd = w_down.astype(BF16).reshape(depth, N_FF_CHUNKS, FF_CHUNK, D_MODEL)
    fcw = ffn_conv_w.reshape(depth, 3, 2, N_FF_CHUNKS, FF_CHUNK)
    cwg = fcw[:, :, 0].transpose(0, 2, 1, 3)
    cwu = fcw[:, :, 1].transpose(0, 2, 1, 3)
    fcb = ffn_conv_b.reshape(depth, 2, N_FF_CHUNKS, 1, FF_CHUNK)
    cbg, cbu = fcb[:, 0], fcb[:, 1]
    row3 = lambda a: a.reshape(depth, 1, -1)
    lane_pad = lambda a: jnp.pad(a.reshape(depth, 1, 2 * SSD_HEADS), ((0, 0), (0, 0), (0, LANES - 2 * SSD_HEADS)))
    dtb = lane_pad(ssd_dt_bias)
    alog = lane_pad(ssd_a_log)
    dexp = jnp.repeat(ssd_d, SSD_W // SSD_HEADS, axis=-1).reshape(depth, 1, SSD_W)
    fg = final_norm_g.reshape(1, D_MODEL)

    rope = _rope_tables(positions)
    xf = x.reshape(t, D_MODEL)
    for l in range(depth):
        lam_init = 0.8 - 0.6 * math.exp(-0.3 * l)
        q, k, vt, rest = _in_proj(xf, row3(norm_mix_g), w_in_b, rope, l, tm)
        ya = _attention(q, k, vt, row3(lam_q1), row3(lam_k1), row3(lam_q2), row3(lam_k2),
                        subln_g.reshape(depth, V_DIM, 1), l, lam_init, batch, seq, tq)
        ym = _ssd_mixers(rest, dtb, alog, dexp, row3(ssd_norm_g), sc_conv_w, ssd_conv_w, row3(ssd_conv_b),
                         l, batch, seq, ssd_rows)
        xf = _ffn(xf, ya, ym, w_out_b, row3(norm_ffn_g), wg, wu, cwg, cwu, cbg, cbu, wd, fg,
                  l, seq, tm, l == depth - 1)
    return xf.reshape(batch, seq, D_MODEL)
```

```python
import functools
import math

import jax
import jax.numpy as jnp
from jax import lax
from jax.experimental import pallas as pl
from jax.experimental.pallas import tpu as pltpu

F32 = jnp.float32
BF16 = jnp.bfloat16

D_MODEL = 1024
EPS = 1e-5
N_ATTN_HEADS = 4
QK_DIM = 64
V_DIM = 128
ATTN_W = N_ATTN_HEADS * V_DIM
ROPE_THETA = 500000.0
ROT_DIM = QK_DIM // 4
Q_SCALE = QK_DIM ** -0.5 * math.log2(math.e)
SC_W = 256
SSD_W = 256
SSD_STATE = 128
SSD_GROUPS = 2
SSD_HEADS = 4
SSD_CHUNK = 128
SSD_XBC = SSD_W + 2 * SSD_GROUPS * SSD_STATE
D_FF = 2816
IN_COLS = 3336

QKV_COLS = 3 * ATTN_W
REST_REAL = IN_COLS - QKV_COLS
REST_W = 1920
IN_PAD = QKV_COLS + REST_W
R_SCB, R_SCC, R_SCH, R_Z, R_XBC, R_DT = 0, 256, 512, 768, 1024, 1792

LANES = 128
SUBLANES = 8
BF16_SUBLANES = 16
VMEM_LIMIT = 56 * 1024 * 1024

FF_CHUNK = 256
N_FF_CHUNKS = D_FF // FF_CHUNK
HALO = BF16_SUBLANES


def _sigmoid(x):
    return 1.0 / (1.0 + jnp.exp(-x))


def _rms(x, g):
    ms = jnp.mean(x * x, axis=-1, keepdims=True)
    return x * lax.rsqrt(ms + EPS) * g


def _conv3_rows(ext, w, lo, n):
    tot = ext.shape[0]
    up = pltpu.roll(ext, 1, 0)
    dn = pltpu.roll(ext, tot - 1, 0)
    out = up[lo:lo + n] * w[0:1]
    out = out + ext[lo:lo + n] * w[1:2]
    out = out + dn[lo:lo + n] * w[2:3]
    return out


def _rope_kernel(pos_ref, invf_ref, o_ref):
    pos = pos_ref[...].astype(F32)
    ang = pos * invf_ref[...]
    lane = lax.broadcasted_iota(jnp.int32, ang.shape, 1) & (QK_DIM - 1)
    c = jnp.cos(ang)
    s = jnp.sin(ang)
    half = ROT_DIM // 2
    o_ref[:, 0:LANES] = c
    o_ref[:, LANES:2 * LANES] = jnp.where(lane < half, -s, 0.0)
    o_ref[:, 2 * LANES:3 * LANES] = jnp.where((lane >= half) & (lane < ROT_DIM), s, 0.0)


def _rope_tables(positions):
    t = positions.size
    tm = min(t, 2048)
    half = ROT_DIM // 2
    inv_freq = ROPE_THETA ** (-jnp.arange(0, ROT_DIM, 2, dtype=F32) / ROT_DIM)
    lane = jnp.arange(LANES) % QK_DIM
    invf = jnp.where(lane < ROT_DIM, inv_freq[lane % half], 0.0).astype(F32)[None, :]
    return pl.pallas_call(
        _rope_kernel,
        grid=(t // tm,),
        in_specs=[pl.BlockSpec((tm, 1), lambda i: (i, 0)),
                  pl.BlockSpec((1, LANES), lambda i: (0, 0))],
        out_specs=pl.BlockSpec((tm, 3 * LANES), lambda i: (i, 0)),
        out_shape=jax.ShapeDtypeStruct((t, 3 * LANES), F32),
        name="rope_tables",
    )(positions.reshape(t, 1), invf)


def _inproj_kernel(x_ref, g_ref, w_ref, rope_ref, q_ref, k_ref, v_ref, r_ref, h_scr):
    h_scr[...] = _rms(x_ref[...], g_ref[...]).astype(BF16)
    c = rope_ref[:, 0:LANES]
    s1 = rope_ref[:, LANES:2 * LANES]
    s2 = rope_ref[:, 2 * LANES:3 * LANES]
    half = ROT_DIM // 2

    def rot(t):
        return t * c + pltpu.roll(t, LANES - half, 1) * s1 + pltpu.roll(t, half, 1) * s2

    cw = 2 * LANES
    for ci in range(2 * ATTN_W // cw):
        r = jnp.dot(h_scr[...], w_ref[:, ci * cw:(ci + 1) * cw], preferred_element_type=F32)
        for hf in range(2):
            t = rot(r[:, hf * LANES:(hf + 1) * LANES])
            col = ci * cw + hf * LANES
            if col < ATTN_W:
                q_ref[:, col:col + LANES] = (t * Q_SCALE).astype(BF16)
            else:
                k_ref[:, col - ATTN_W:col - ATTN_W + LANES] = t.astype(BF16)
    for ci in range(ATTN_W // cw):
        c0 = 2 * ATTN_W + ci * cw
        r = jnp.dot(h_scr[...], w_ref[:, c0:c0 + cw], preferred_element_type=F32)
        v_ref[ci * cw:(ci + 1) * cw, :] = r.T.astype(BF16)
    c0 = 0
    while c0 < REST_W:
        w = min(cw, REST_W - c0)
        r_ref[:, c0:c0 + w] = jnp.dot(h_scr[...], w_ref[:, QKV_COLS + c0:QKV_COLS + c0 + w],
                                      preferred_element_type=F32)
        c0 += w


def _in_proj(x, g, w_in, rope, layer, tm):
    t = x.shape[0]
    return pl.pallas_call(
        _inproj_kernel,
        grid=(t // tm,),
        in_specs=[pl.BlockSpec((tm, D_MODEL), lambda i: (i, 0)),
                  pl.BlockSpec((None, 1, D_MODEL), lambda i: (layer, 0, 0)),
                  pl.BlockSpec((None, D_MODEL, IN_PAD), lambda i: (layer, 0, 0)),
                  pl.BlockSpec((tm, 3 * LANES), lambda i: (i, 0))],
        out_specs=[pl.BlockSpec((tm, ATTN_W), lambda i: (i, 0)),
                   pl.BlockSpec((tm, ATTN_W), lambda i: (i, 0)),
                   pl.BlockSpec((None, ATTN_W, tm), lambda i: (i, 0, 0)),
                   pl.BlockSpec((tm, REST_W), lambda i: (i, 0))],
        out_shape=[jax.ShapeDtypeStruct((t, ATTN_W), BF16),
                   jax.ShapeDtypeStruct((t, ATTN_W), BF16),
                   jax.ShapeDtypeStruct((t // tm, ATTN_W, tm), BF16),
                   jax.ShapeDtypeStruct((t, REST_W), F32)],
        scratch_shapes=[pltpu.VMEM((tm, D_MODEL), BF16)],
        compiler_params=pltpu.CompilerParams(dimension_semantics=("arbitrary",),
                                             vmem_limit_bytes=VMEM_LIMIT),
        name="in_proj",
    )(x, g, w_in, rope)


def _attn_kernel(lq1_ref, lk1_ref, lq2_ref, lk2_ref, sg_ref, q_ref, k_ref, vt_ref, o_ref,
                 qp_scr, acc_scr, *, lam_init, tq):
    seq = k_ref.shape[0]
    nk, _, tk = vt_ref.shape
    lam = (jnp.exp(jnp.sum(lq1_ref[...] * lk1_ref[...], axis=-1, keepdims=True))
           - jnp.exp(jnp.sum(lq2_ref[...] * lk2_ref[...], axis=-1, keepdims=True)) + lam_init)
    lane = lax.broadcasted_iota(jnp.int32, (tq, V_DIM), 1)
    nt = (((1,), (1,)), ((), ()))

    def q_tile(qi, carry):
        q0 = pl.multiple_of(qi * tq, tq)
        q = q_ref[pl.ds(q0, tq), :]
        zero = jnp.zeros_like(q)
        qp_scr[0:tq, :] = jnp.where(lane < QK_DIM, q, zero)
        qp_scr[tq:2 * tq, :] = jnp.where(lane >= QK_DIM, q, zero)
        acc_scr[...] = jnp.zeros(acc_scr.shape, F32)

        def scores(j):
            kb = k_ref[j * tk:(j + 1) * tk, :]
            sts = [lax.dot_general(kb, qp_scr[c * tq:(c + 1) * tq, :], nt, preferred_element_type=F32) for c in range(2)]
            return sts, [jnp.max(st, axis=0, keepdims=True) for st in sts]

        m_run = [jnp.full((1, tq), -jnp.inf, F32) for _ in range(2)]
        ones = jnp.ones((BF16_SUBLANES, tk), BF16)
        sts, cmax = scores(0)
        for j in range(nk):
            nxt = scores(j + 1) if j + 1 < nk else None
            vt1 = jnp.concatenate([vt_ref[j], ones], axis=0)
            for c in range(2):
                cols = slice(c * tq, (c + 1) * tq)
                m_new = jnp.maximum(m_run[c], cmax[c])
                alpha = jnp.exp2(m_run[c] - m_new)
                p = jnp.exp2(sts[c] - m_new).astype(BF16)
                acc_scr[:, cols] = alpha * acc_scr[:, cols] + jnp.dot(vt1, p, preferred_element_type=F32)
                m_run[c] = m_new
            if nxt is not None:
                sts, cmax = nxt
        o = acc_scr[0:V_DIM, :] / acc_scr[V_DIM:V_DIM + 1, :]
        o = o[:, 0:tq] - lam * o[:, tq:2 * tq]
        ms = jnp.mean(o * o, axis=0, keepdims=True)
        y = o * lax.rsqrt(ms + EPS) * sg_ref[...] * (1.0 - lam_init)
        o_ref[pl.ds(q0, tq), :] = y.T.astype(BF16)
        return carry

    lax.fori_loop(0, seq // tq, q_tile, 0)


def _attention(q, k, vt, lq1, lk1, lq2, lk2, subg_col, layer, lam_init, batch, seq, tq):
    t = q.shape[0]
    tk = vt.shape[2]
    small = lambda n: pl.BlockSpec((None, 1, n), lambda bi, hi: (layer, 0, 0))
    head = pl.BlockSpec((seq, V_DIM), lambda bi, hi: (bi, hi))
    return pl.pallas_call(
        functools.partial(_attn_kernel, lam_init=lam_init, tq=tq),
        grid=(batch, N_ATTN_HEADS),
        in_specs=[small(QK_DIM), small(QK_DIM), small(QK_DIM), small(QK_DIM),
                  pl.BlockSpec((None, V_DIM, 1), lambda bi, hi: (layer, 0, 0)),
                  head, head,
                  pl.BlockSpec((seq // tk, V_DIM, tk), lambda bi, hi: (bi, hi, 0))],
        out_specs=head,
        out_shape=jax.ShapeDtypeStruct((t, ATTN_W), BF16),
        scratch_shapes=[pltpu.VMEM((2 * tq, V_DIM), BF16),
                        pltpu.VMEM((V_DIM + BF16_SUBLANES, 2 * tq), F32)],
        compiler_params=pltpu.CompilerParams(dimension_semantics=("arbitrary", "arbitrary"),
                                             vmem_limit_bytes=VMEM_LIMIT),
        name="diff_attn",
    )(lq1, lk1, lq2, lk2, subg_col, q, k, vt)


def _ssd_kernel(dtb_ref, alog_ref, dexp_ref, ng_ref, scw_ref, cw_ref, cb_ref, main_ref, prev_ref, next_ref,
                o_ref, ybwd_scr, h_scr, *, rows, nblk):
    sweep = pl.program_id(1)
    i = pl.program_id(2)
    blk = jnp.where(sweep == 0, nblk - 1 - i, i)
    has_prev = blk > 0
    has_next = blk < nblk - 1
    L = SSD_CHUNK
    nsub = rows // L

    @pl.when(i == 0)
    def _():
        h_scr[...] = jnp.zeros(h_scr.shape, F32)

    def ext_cols(c0, c1):
        pv = jnp.where(has_prev, prev_ref[:, c0:c1], 0.0)
        nx = jnp.where(has_next, next_ref[:, c0:c1], 0.0)
        return jnp.concatenate([pv, main_ref[:, c0:c1], nx], axis=0)

    xbc = _conv3_rows(ext_cols(R_XBC, R_XBC + SSD_XBC), cw_ref[...], SUBLANES, rows) + cb_ref[...]
    xbc = xbc * _sigmoid(xbc)
    lane8 = lax.broadcasted_iota(jnp.int32, (1, LANES), 1) < 2 * SSD_HEADS
    xdt = main_ref[:, R_DT:R_DT + LANES] + dtb_ref[...]
    dt = jnp.maximum(xdt, 0.0) + jnp.log1p(jnp.exp(-jnp.abs(xdt)))
    a_all = dt * jnp.where(lane8, -jnp.exp(alog_ref[...]), 0.0)

    rowi = lax.broadcasted_iota(jnp.int32, (L, L), 0)
    coli = lax.broadcasted_iota(jnp.int32, (L, L), 1)
    lo = lax.broadcasted_iota(jnp.int32, (L, LANES), 1) < LANES // 2
    nt = (((1,), (1,)), ((), ()))

    def pick(arr, j0):
        return jnp.where(lo[0:arr.shape[0]], arr[:, j0:j0 + 1], arr[:, j0 + 1:j0 + 2])

    def chunk(r0, fwd):
        mask = (coli <= rowi) if fwd else (coli >= rowi)
        tmat = mask.astype(BF16)
        a = a_all[r0:r0 + L]
        a_hi = a.astype(BF16)
        r1 = a - a_hi.astype(F32)
        a_mid = r1.astype(BF16)
        a_lo = (r1 - a_mid.astype(F32)).astype(BF16)
        cs = (jnp.dot(tmat, a_hi, preferred_element_type=F32)
              + jnp.dot(tmat, a_mid, preferred_element_type=F32)
              + jnp.dot(tmat, a_lo, preferred_element_type=F32))
        cst = cs.T
        tot = cs[L - 1:L] if fwd else cs[0:1]
        ecs = jnp.exp(cs)
        dsd = jnp.exp(tot - cs)
        etot = jnp.exp(tot)
        outs = []
        for g in range(SSD_GROUPS):
            j0 = 2 * g + (0 if fwd else SSD_HEADS)
            xs_g = xbc[r0:r0 + L, g * LANES:(g + 1) * LANES]
            b_g = xbc[r0:r0 + L, SSD_W + g * SSD_STATE:SSD_W + (g + 1) * SSD_STATE]
            c_g = xbc[r0:r0 + L, SSD_W + (SSD_GROUPS + g) * SSD_STATE:SSD_W + (SSD_GROUPS + g + 1) * SSD_STATE]
            c_bf = c_g.astype(BF16)
            xd = xs_g * pick(dt[r0:r0 + L], j0)
            gm = lax.dot_general(c_bf, b_g.astype(BF16), nt, preferred_element_type=F32)

            def decay(j):
                seg = cs[:, j:j + 1] - cst[j:j + 1, :]
                return jnp.where(mask, jnp.exp(seg), 0.0)

            m0 = (gm * decay(j0)).astype(BF16)
            m1 = (gm * decay(j0 + 1)).astype(BF16)
            y = (jnp.dot(m0, jnp.where(lo, xd, 0.0).astype(BF16), preferred_element_type=F32)
                 + jnp.dot(m1, jnp.where(lo, 0.0, xd).astype(BF16), preferred_element_type=F32))
            hg = h_scr[g]
            y = y + jnp.dot(c_bf, hg.astype(BF16), preferred_element_type=F32) * pick(ecs, j0)
            xdw = (xd * pick(dsd, j0)).astype(BF16)
            st = jnp.dot(b_g.T.astype(BF16), xdw, preferred_element_type=F32)
            h_scr[g] = hg * pick(etot, j0) + st
            outs.append((y, xs_g))
        return outs

    @pl.when(sweep == 0)
    def _():
        for si in reversed(range(nsub)):
            r0 = si * L
            outs = chunk(r0, False)
            g0 = pl.multiple_of(blk * rows + r0, L)
            for g in range(SSD_GROUPS):
                ybwd_scr[pl.ds(g0, L), g * LANES:(g + 1) * LANES] = outs[g][0]

    @pl.when(sweep == 1)
    def _():
        u = ext_cols(R_SCC, R_SCC + SC_W) * ext_cols(R_SCH, R_SCH + SC_W)
        yc = main_ref[:, R_SCB:R_SCB + SC_W] * _conv3_rows(u, scw_ref[...], SUBLANES, rows)
        o_ref[:, 0:SC_W] = yc.astype(BF16)
        for si in range(nsub):
            r0 = si * L
            outs = chunk(r0, True)
            g0 = pl.multiple_of(blk * rows + r0, L)
            for g in range(SSD_GROUPS):
                y, xs_g = outs[g]
                gl = slice(g * LANES, (g + 1) * LANES)
                y = y + ybwd_scr[pl.ds(g0, L), gl] + dexp_ref[:, gl] * xs_g
                z = main_ref[r0:r0 + L, R_Z + g * LANES:R_Z + (g + 1) * LANES]
                y = y * (z * _sigmoid(z))
                o_ref[r0:r0 + L, SC_W + g * LANES:SC_W + (g + 1) * LANES] = _rms(y, ng_ref[:, gl]).astype(BF16)


def _ssd_mixers(rest, dtb, alog, dexp, ng, scw, cw, cb, layer, batch, seq, rows):
    t = rest.shape[0]
    nblk = seq // rows
    r8 = rows // SUBLANES

    def blk_of(s, i):
        return jnp.where(s == 0, nblk - 1 - i, i)

    small = lambda r, n: pl.BlockSpec((None, r, n), lambda bi, s, i: (layer, 0, 0))
    return pl.pallas_call(
        functools.partial(_ssd_kernel, rows=rows, nblk=nblk),
        grid=(batch, 2, nblk),
        in_specs=[small(1, LANES), small(1, LANES), small(1, SSD_W), small(1, SSD_W),
                  small(3, SC_W), small(3, SSD_XBC), small(1, SSD_XBC),
                  pl.BlockSpec((rows, REST_W), lambda bi, s, i: (bi * nblk + blk_of(s, i), 0)),
                  pl.BlockSpec((SUBLANES, REST_W),
                               lambda bi, s, i: (jnp.maximum((bi * nblk + blk_of(s, i)) * r8 - 1, 0), 0)),
                  pl.BlockSpec((SUBLANES, REST_W),
                               lambda bi, s, i: (jnp.minimum((bi * nblk + blk_of(s, i) + 1) * r8,
                                                             t // SUBLANES - 1), 0))],
        out_specs=pl.BlockSpec((rows, SC_W + SSD_W), lambda bi, s, i: (bi * nblk + s * i, 0)),
        out_shape=jax.ShapeDtypeStruct((t, SC_W + SSD_W), BF16),
        scratch_shapes=[pltpu.VMEM((seq, SSD_W), F32),
                        pltpu.VMEM((SSD_GROUPS, SSD_STATE, LANES), F32)],
        compiler_params=pltpu.CompilerParams(dimension_semantics=("arbitrary", "arbitrary", "arbitrary"),
                                             vmem_limit_bytes=VMEM_LIMIT),
        name="ssd_mixers",
    )(dtb, alog, dexp, ng, scw, cw, cb, rest, rest, rest)


def _ffn_kernel(x_ref, xp_ref, xn_ref, ya_ref, yap_ref, yan_ref, ym_ref, ymp_ref, ymn_ref,
                wout_ref, g_ref, wup_ref, cwg_ref, cwu_ref, cbg_ref, cbu_ref, wd_ref, fg_ref,
                o_ref, h_scr, xnew_scr, acc_scr, ua_scr, ub_scr, *, tm, tiles_per_seq, final_norm):
    ti = pl.program_id(0) % tiles_per_seq
    has_prev = ti > 0
    has_next = ti < tiles_per_seq - 1
    ext_rows = tm + 2 * HALO

    def ext(prev, main, nxt):
        return jnp.concatenate([prev[...], main[...], nxt[...]], axis=0)

    mix = (jnp.dot(ext(yap_ref, ya_ref, yan_ref), wout_ref[0:ATTN_W, :], preferred_element_type=F32)
           + jnp.dot(ext(ymp_ref, ym_ref, ymn_ref), wout_ref[ATTN_W:2 * ATTN_W, :], preferred_element_type=F32))
    xnew = ext(xp_ref, x_ref, xn_ref) + mix
    row = lax.broadcasted_iota(jnp.int32, (ext_rows, 1), 0)
    valid = (row >= jnp.where(has_prev, 0, HALO)) & (row < jnp.where(has_next, ext_rows, HALO + tm))
    h_scr[...] = jnp.where(valid, _rms(xnew, g_ref[...]), 0.0).astype(BF16)
    xnew_scr[...] = xnew[HALO:HALO + tm]
    acc_scr[...] = jnp.zeros(acc_scr.shape, F32)

    def up_proj(j, u_ref):
        hb = h_scr[...]
        c0 = pl.multiple_of(j * FF_CHUNK, FF_CHUNK)
        u_ref[0] = jnp.dot(hb, wup_ref[:, pl.ds(c0, FF_CHUNK)], preferred_element_type=F32)
        u_ref[1] = jnp.dot(hb, wup_ref[:, pl.ds(D_FF + c0, FF_CHUNK)], preferred_element_type=F32)

    def conv_rows(u_ref, idx, w):
        out = u_ref[idx, HALO - 1:HALO - 1 + tm, :] * w[0:1]
        out = out + u_ref[idx, HALO:HALO + tm, :] * w[1:2]
        out = out + u_ref[idx, HALO + 1:HALO + 1 + tm, :] * w[2:3]
        return out

    def gate_down(j, u_ref):
        cg = conv_rows(u_ref, 0, cwg_ref[j]) + cbg_ref[j]
        cu = conv_rows(u_ref, 1, cwu_ref[j]) + cbu_ref[j]
        act = (cg * _sigmoid(cg)) * cu
        acc_scr[...] += jnp.dot(act.astype(BF16), wd_ref[j], preferred_element_type=F32)

    def ff_pair(i, carry):
        j = 2 * i
        up_proj(j + 1, ub_scr)
        gate_down(j, ua_scr)
        up_proj(j + 2, ua_scr)
        gate_down(j + 1, ub_scr)
        return carry

    assert N_FF_CHUNKS % 2 == 1
    up_proj(0, ua_scr)
    lax.fori_loop(0, N_FF_CHUNKS // 2, ff_pair, 0)
    gate_down(N_FF_CHUNKS - 1, ua_scr)
    out = xnew_scr[...] + acc_scr[...]
    if final_norm:
        out = _rms(out, fg_ref[...])
    o_ref[...] = out


def _ffn(x, ya, ym, w_out, g, w_up, cwg, cwu, cbg, cbu, wd, fg, layer, seq, tm, final_norm):
    t = x.shape[0]
    tiles_per_seq = seq // tm
    hb = tm // HALO
    nh = t // HALO

    def main(w):
        return pl.BlockSpec((tm, w), lambda i: (i, 0))

    def prev(w):
        return pl.BlockSpec((HALO, w), lambda i: (jnp.maximum(i * hb - 1, 0), 0))

    def nxt(w):
        return pl.BlockSpec((HALO, w), lambda i: (jnp.minimum((i + 1) * hb, nh - 1), 0))

    def resident(shape):
        nd = len(shape)
        return pl.BlockSpec((None,) + shape, lambda i: (layer,) + (0,) * nd,
                            pipeline_mode=pl.Buffered(1))

    ext_rows = tm + 2 * HALO
    return pl.pallas_call(
        functools.partial(_ffn_kernel, tm=tm, tiles_per_seq=tiles_per_seq, final_norm=final_norm),
        grid=(t // tm,),
        in_specs=[main(D_MODEL), prev(D_MODEL), nxt(D_MODEL),
                  main(ATTN_W), prev(ATTN_W), nxt(ATTN_W),
                  main(SC_W + SSD_W), prev(SC_W + SSD_W), nxt(SC_W + SSD_W),
                  resident((D_MODEL, D_MODEL)),
                  resident((1, D_MODEL)),
                  resident((D_MODEL, 2 * D_FF)),
                  resident((N_FF_CHUNKS, 3, FF_CHUNK)),
                  resident((N_FF_CHUNKS, 3, FF_CHUNK)),
                  resident((N_FF_CHUNKS, 1, FF_CHUNK)),
                  resident((N_FF_CHUNKS, 1, FF_CHUNK)),
                  resident((N_FF_CHUNKS, FF_CHUNK, D_MODEL)),
                  pl.BlockSpec((1, D_MODEL), lambda i: (0, 0))],
        out_specs=main(D_MODEL),
        out_shape=jax.ShapeDtypeStruct((t, D_MODEL), F32),
        scratch_shapes=[pltpu.VMEM((ext_rows, D_MODEL), BF16),
                        pltpu.VMEM((tm, D_MODEL), F32),
                        pltpu.VMEM((tm, D_MODEL), F32),
                        pltpu.VMEM((2, ext_rows, FF_CHUNK), F32),
                        pltpu.VMEM((2, ext_rows, FF_CHUNK), F32)],
        compiler_params=pltpu.CompilerParams(dimension_semantics=("arbitrary",),
                                             vmem_limit_bytes=VMEM_LIMIT),
        name="ffn",
    )(x, x, x, ya, ya, ya, ym, ym, ym, w_out, g, w_up, cwg, cwu, cbg, cbu, wd, fg)


def kernel(x, positions, norm_mix_g, w_in, lam_q1, lam_k1, lam_q2, lam_k2, subln_g, sc_conv_w, ssd_conv_w,
           ssd_conv_b, ssd_dt_bias, ssd_a_log, ssd_d, ssd_norm_g, w_out, norm_ffn_g, w_up, ffn_conv_w,
           ffn_conv_b, w_down, final_norm_g):
    batch, seq, _ = x.shape
    depth = w_in.shape[0]
    t = batch * seq
    tm = min(512, seq)
    tq = min(512, seq)
    ssd_rows = min(256, seq)

    w_in_b = jnp.pad(w_in, ((0, 0), (0, 0), (0, IN_PAD - IN_COLS))).astype(BF16)
    w_out_b = w_out.astype(BF16)
    w_up_b = w_up.astype(BF16)
    wd = w_down.astype(BF16).reshape(depth, N_FF_CHUNKS, FF_CHUNK, D_MODEL)
    fcw = ffn_conv_w.reshape(depth, 3, 2, N_FF_CHUNKS, FF_CHUNK)
    cwg = fcw[:, :, 0].transpose(0, 2, 1, 3)
    cwu = fcw[:, :, 1].transpose(0, 2, 1, 3)
    fcb = ffn_conv_b.reshape(depth, 2, N_FF_CHUNKS, 1, FF_CHUNK)
    cbg, cbu = fcb[:, 0], fcb[:, 1]
    row3 = lambda a: a.reshape(depth, 1, -1)
    lane_pad = lambda a: jnp.pad(a.reshape(depth, 1, 2 * SSD_HEADS), ((0, 0), (0, 0), (0, LANES - 2 * SSD_HEADS)))
    dtb = lane_pad(ssd_dt_bias)
    alog = lane_pad(ssd_a_log)
    dexp = jnp.repeat(ssd_d, SSD_W // SSD_HEADS, axis=-1).reshape(depth, 1, SSD_W)
    fg = final_norm_g.reshape(1, D_MODEL)

    rope = _rope_tables(positions)
    xf = x.reshape(t, D_MODEL)
    for l in range(depth):
        lam_init = 0.8 - 0.6 * math.exp(-0.3 * l)
        q, k, vt, rest = _in_proj(xf, row3(norm_mix_g), w_in_b, rope, l, tm)
        ya = _attention(q, k, vt, row3(lam_q1), row3(lam_k1), row3(lam_q2), row3(lam_k2),
                        subln_g.reshape(depth, V_DIM, 1), l, lam_init, batch, seq, tq)
        ym = _ssd_mixers(rest, dtb, alog, dexp, row3(ssd_norm_g), sc_conv_w, ssd_conv_w, row3(ssd_conv_b),
                         l, batch, seq, ssd_rows)
        xf = _ffn(xf, ya, ym, w_out_b, row3(norm_ffn_g), w_up_b, cwg, cwu, cbg, cbu, wd, fg,
                  l, seq, tm, l == depth - 1)
    return xf.reshape(batch, seq, D_MODEL)
```

```python
import functools
import math

import jax
import jax.numpy as jnp
from jax import lax
from jax.experimental import pallas as pl
from jax.experimental.pallas import tpu as pltpu

F32 = jnp.float32
BF16 = jnp.bfloat16

D_MODEL = 1024
EPS = 1e-5
N_ATTN_HEADS = 4
QK_DIM = 64
V_DIM = 128
ATTN_W = N_ATTN_HEADS * V_DIM
ROPE_THETA = 500000.0
ROT_DIM = QK_DIM // 4
Q_SCALE = QK_DIM ** -0.5 * math.log2(math.e)
SC_W = 256
SSD_W = 256
SSD_STATE = 128
SSD_GROUPS = 2
SSD_HEADS = 4
SSD_CHUNK = 128
SSD_XBC = SSD_W + 2 * SSD_GROUPS * SSD_STATE
D_FF = 2816
IN_COLS = 3336

QKV_COLS = 3 * ATTN_W
REST_REAL = IN_COLS - QKV_COLS
REST_W = 1920
IN_PAD = QKV_COLS + REST_W
R_SCB, R_SCC, R_SCH, R_Z, R_XBC, R_DT = 0, 256, 512, 768, 1024, 1792

LANES = 128
SUBLANES = 8
BF16_SUBLANES = 16
VMEM_LIMIT = 56 * 1024 * 1024

FF_CHUNK = 256
N_FF_CHUNKS = D_FF // FF_CHUNK
HALO = BF16_SUBLANES


def _sigmoid(x):
    return 1.0 / (1.0 + jnp.exp(-x))


def _rms(x, g):
    ms = jnp.mean(x * x, axis=-1, keepdims=True)
    return x * lax.rsqrt(ms + EPS) * g


def _conv3_rows(ext, w, lo, n):
    tot = ext.shape[0]
    up = pltpu.roll(ext, 1, 0)
    dn = pltpu.roll(ext, tot - 1, 0)
    out = up[lo:lo + n] * w[0:1]
    out = out + ext[lo:lo + n] * w[1:2]
    out = out + dn[lo:lo + n] * w[2:3]
    return out


def _rope_kernel(pos_ref, invf_ref, o_ref):
    pos = pos_ref[...].astype(F32)
    ang = pos * invf_ref[...]
    lane = lax.broadcasted_iota(jnp.int32, ang.shape, 1) & (QK_DIM - 1)
    c = jnp.cos(ang)
    s = jnp.sin(ang)
    half = ROT_DIM // 2
    o_ref[:, 0:LANES] = c
    o_ref[:, LANES:2 * LANES] = jnp.where(lane < half, -s, 0.0)
    o_ref[:, 2 * LANES:3 * LANES] = jnp.where((lane >= half) & (lane < ROT_DIM), s, 0.0)


def _rope_tables(positions):
    t = positions.size
    tm = min(t, 2048)
    half = ROT_DIM // 2
    inv_freq = ROPE_THETA ** (-jnp.arange(0, ROT_DIM, 2, dtype=F32) / ROT_DIM)
    lane = jnp.arange(LANES) % QK_DIM
    invf = jnp.where(lane < ROT_DIM, inv_freq[lane % half], 0.0).astype(F32)[None, :]
    return pl.pallas_call(
        _rope_kernel,
        grid=(t // tm,),
        in_specs=[pl.BlockSpec((tm, 1), lambda i: (i, 0)),
                  pl.BlockSpec((1, LANES), lambda i: (0, 0))],
        out_specs=pl.BlockSpec((tm, 3 * LANES), lambda i: (i, 0)),
        out_shape=jax.ShapeDtypeStruct((t, 3 * LANES), F32),
        name="rope_tables",
    )(positions.reshape(t, 1), invf)


def _inproj_kernel(x_ref, g_ref, w_ref, rope_ref, q_ref, k_ref, v_ref, r_ref, h_scr):
    h_scr[...] = _rms(x_ref[...], g_ref[...]).astype(BF16)
    c = rope_ref[:, 0:LANES]
    s1 = rope_ref[:, LANES:2 * LANES]
    s2 = rope_ref[:, 2 * LANES:3 * LANES]
    half = ROT_DIM // 2

    def rot(t):
        return t * c + pltpu.roll(t, LANES - half, 1) * s1 + pltpu.roll(t, half, 1) * s2

    cw = 2 * LANES
    for ci in range(2 * ATTN_W // cw):
        r = jnp.dot(h_scr[...], w_ref[:, ci * cw:(ci + 1) * cw], preferred_element_type=F32)
        for hf in range(2):
            t = rot(r[:, hf * LANES:(hf + 1) * LANES])
            col = ci * cw + hf * LANES
            if col < ATTN_W:
                q_ref[:, col:col + LANES] = (t * Q_SCALE).astype(BF16)
            else:
                k_ref[:, col - ATTN_W:col - ATTN_W + LANES] = t.astype(BF16)
    for ci in range(ATTN_W // cw):
        c0 = 2 * ATTN_W + ci * cw
        r = jnp.dot(h_scr[...], w_ref[:, c0:c0 + cw], preferred_element_type=F32)
        v_ref[ci * cw:(ci + 1) * cw, :] = r.T.astype(BF16)
    c0 = 0
    while c0 < REST_W:
        w = min(cw, REST_W - c0)
        r_ref[:, c0:c0 + w] = jnp.dot(h_scr[...], w_ref[:, QKV_COLS + c0:QKV_COLS + c0 + w],
                                      preferred_element_type=F32)
        c0 += w


def _in_proj(x, g, w_in, rope, layer, tm):
    t = x.shape[0]
    return pl.pallas_call(
        _inproj_kernel,
        grid=(t // tm,),
        in_specs=[pl.BlockSpec((tm, D_MODEL), lambda i: (i, 0)),
                  pl.BlockSpec((None, 1, D_MODEL), lambda i: (layer, 0, 0)),
                  pl.BlockSpec((None, D_MODEL, IN_PAD), lambda i: (layer, 0, 0)),
                  pl.BlockSpec((tm, 3 * LANES), lambda i: (i, 0))],
        out_specs=[pl.BlockSpec((tm, ATTN_W), lambda i: (i, 0)),
                   pl.BlockSpec((tm, ATTN_W), lambda i: (i, 0)),
                   pl.BlockSpec((None, ATTN_W, tm), lambda i: (i, 0, 0)),
                   pl.BlockSpec((tm, REST_W), lambda i: (i, 0))],
        out_shape=[jax.ShapeDtypeStruct((t, ATTN_W), BF16),
                   jax.ShapeDtypeStruct((t, ATTN_W), BF16),
                   jax.ShapeDtypeStruct((t // tm, ATTN_W, tm), BF16),
                   jax.ShapeDtypeStruct((t, REST_W), F32)],
        scratch_shapes=[pltpu.VMEM((tm, D_MODEL), BF16)],
        compiler_params=pltpu.CompilerParams(dimension_semantics=("arbitrary",),
                                             vmem_limit_bytes=VMEM_LIMIT),
        name="in_proj",
    )(x, g, w_in, rope)


def _attn_kernel(lq1_ref, lk1_ref, lq2_ref, lk2_ref, sg_ref, q_ref, k_ref, vt_ref, o_ref,
                 qp_scr, acc_scr, *, lam_init, tq):
    seq = k_ref.shape[0]
    nk, _, tk = vt_ref.shape
    lam = (jnp.exp(jnp.sum(lq1_ref[...] * lk1_ref[...], axis=-1, keepdims=True))
           - jnp.exp(jnp.sum(lq2_ref[...] * lk2_ref[...], axis=-1, keepdims=True)) + lam_init)
    lane = lax.broadcasted_iota(jnp.int32, (tq, V_DIM), 1)
    nt = (((1,), (1,)), ((), ()))

    def q_tile(qi, carry):
        q0 = pl.multiple_of(qi * tq, tq)
        q = q_ref[pl.ds(q0, tq), :]
        zero = jnp.zeros_like(q)
        qp_scr[0:tq, :] = jnp.where(lane < QK_DIM, q, zero)
        qp_scr[tq:2 * tq, :] = jnp.where(lane >= QK_DIM, q, zero)
        acc_scr[...] = jnp.zeros(acc_scr.shape, F32)

        def scores(j):
            kb = k_ref[j * tk:(j + 1) * tk, :]
            sts = [lax.dot_general(kb, qp_scr[c * tq:(c + 1) * tq, :], nt, preferred_element_type=F32) for c in range(2)]
            return sts, [jnp.max(st, axis=0, keepdims=True) for st in sts]

        m_run = [jnp.full((1, tq), -jnp.inf, F32) for _ in range(2)]
        ones = jnp.ones((BF16_SUBLANES, tk), BF16)
        sts, cmax = scores(0)
        for j in range(nk):
            nxt = scores(j + 1) if j + 1 < nk else None
            vt1 = jnp.concatenate([vt_ref[j], ones], axis=0)
            for c in range(2):
                cols = slice(c * tq, (c + 1) * tq)
                m_new = jnp.maximum(m_run[c], cmax[c])
                alpha = jnp.exp2(m_run[c] - m_new)
                p = jnp.exp2(sts[c] - m_new).astype(BF16)
                acc_scr[:, cols] = alpha * acc_scr[:, cols] + jnp.dot(vt1, p, preferred_element_type=F32)
                m_run[c] = m_new
            if nxt is not None:
                sts, cmax = nxt
        o = acc_scr[0:V_DIM, :] / acc_scr[V_DIM:V_DIM + 1, :]
        o = o[:, 0:tq] - lam * o[:, tq:2 * tq]
        ms = jnp.mean(o * o, axis=0, keepdims=True)
        y = o * lax.rsqrt(ms + EPS) * sg_ref[...] * (1.0 - lam_init)
        o_ref[pl.ds(q0, tq), :] = y.T.astype(BF16)
        return carry

    lax.fori_loop(0, seq // tq, q_tile, 0)


def _attention(q, k, vt, lq1, lk1, lq2, lk2, subg_col, layer, lam_init, batch, seq, tq):
    t = q.shape[0]
    tk = vt.shape[2]
    small = lambda n: pl.BlockSpec((None, 1, n), lambda bi, hi: (layer, 0, 0))
    head = pl.BlockSpec((seq, V_DIM), lambda bi, hi: (bi, hi))
    return pl.pallas_call(
        functools.partial(_attn_kernel, lam_init=lam_init, tq=tq),
        grid=(batch, N_ATTN_HEADS),
        in_specs=[small(QK_DIM), small(QK_DIM), small(QK_DIM), small(QK_DIM),
                  pl.BlockSpec((None, V_DIM, 1), lambda bi, hi: (layer, 0, 0)),
                  head, head,
                  pl.BlockSpec((seq // tk, V_DIM, tk), lambda bi, hi: (bi, hi, 0))],
        out_specs=head,
        out_shape=jax.ShapeDtypeStruct((t, ATTN_W), BF16),
        scratch_shapes=[pltpu.VMEM((2 * tq, V_DIM), BF16),
                        pltpu.VMEM((V_DIM + BF16_SUBLANES, 2 * tq), F32)],
        compiler_params=pltpu.CompilerParams(dimension_semantics=("arbitrary", "arbitrary"),
                                             vmem_limit_bytes=VMEM_LIMIT),
        name="diff_attn",
    )(lq1, lk1, lq2, lk2, subg_col, q, k, vt)


N_SERIES = 2 * SSD_HEADS
N_PICK = 2 * SSD_GROUPS
SEL_COLS = (N_SERIES + N_PICK) * LANES


def _select_matrix():
    lane = jnp.arange(SEL_COLS)
    blk, within = lane // LANES, lane % LANES
    d, g = (blk - N_SERIES) // SSD_GROUPS, (blk - N_SERIES) % SSD_GROUPS
    src = jnp.where(blk < N_SERIES, blk, SSD_HEADS * d + 2 * g + (within >= LANES // 2))
    sel = (jnp.arange(LANES)[:, None] == src[None, :]).astype(BF16)
    return jnp.concatenate([sel, sel, sel], axis=0)


def _split3(x):
    hi = x.astype(BF16)
    r1 = x - hi.astype(F32)
    mid = r1.astype(BF16)
    lo = (r1 - mid.astype(F32)).astype(BF16)
    return hi, mid, lo


def _ssd_kernel(dtb_ref, alog_ref, dexp_ref, ng_ref, scw_ref, cw_ref, cb_ref, sel_ref, main_ref, prev_ref, next_ref,
                o_ref, xbc_scr, bt_scr, dt_scr, hb_scr, h_scr, *, rows, nblk):
    sweep = pl.program_id(1)
    i = pl.program_id(2)
    blk = jnp.where(sweep == 0, nblk - 1 - i, i)
    has_prev = blk > 0
    has_next = blk < nblk - 1
    L = SSD_CHUNK
    nsub = rows // L
    half = LANES // 2

    @pl.when(i == 0)
    def _():
        h_scr[...] = jnp.zeros(h_scr.shape, F32)

    def ext_cols(c0, c1):
        pv = jnp.where(has_prev, prev_ref[:, c0:c1], 0.0)
        nx = jnp.where(has_next, next_ref[:, c0:c1], 0.0)
        return jnp.concatenate([pv, main_ref[:, c0:c1], nx], axis=0)

    rowi = lax.broadcasted_iota(jnp.int32, (L, L), 0)
    coli = lax.broadcasted_iota(jnp.int32, (L, L), 1)
    upper = (rowi <= coli).astype(BF16)
    lower = (rowi >= coli).astype(BF16)
    lo = lax.broadcasted_iota(jnp.int32, (L, LANES), 1) < half
    fwd_rows = lax.broadcasted_iota(jnp.int32, (N_SERIES, 1), 0) < SSD_HEADS
    neg_a = -jnp.exp(alog_ref[...])

    def series(dtt):
        hi, mid, low = _split3(dtt * neg_a)
        pre = sum(jnp.dot(t, upper, preferred_element_type=F32) for t in (hi, mid, low))
        suf = sum(jnp.dot(t, lower, preferred_element_type=F32) for t in (hi, mid, low))
        cst = jnp.where(fwd_rows, pre, suf)
        tot = jnp.where(fwd_rows, cst[:, L - 1:L], cst[:, 0:1])
        return cst, tot

    def chunk_state(bt, xs_g, w, j0):
        return (jnp.dot((bt * w[j0:j0 + 1, :]).astype(BF16), jnp.where(lo, xs_g, 0.0).astype(BF16),
                        preferred_element_type=F32)
                + jnp.dot((bt * w[j0 + 1:j0 + 2, :]).astype(BF16), jnp.where(lo, 0.0, xs_g).astype(BF16),
                          preferred_element_type=F32))

    def pick2(col, j0):
        return jnp.where(lo[0:1], col[j0:j0 + 1, :], col[j0 + 1:j0 + 2, :])

    def chunk_series(dtt_all):
        out = []
        for si in range(nsub):
            dtt = dtt_all[:, si * L:(si + 1) * L]
            cst, tot = series(dtt)
            out.append((dtt, cst, dtt * jnp.exp(tot - cst), jnp.exp(tot)))
        return out

    @pl.when(sweep == 0)
    def _():
        xbc = _conv3_rows(ext_cols(R_XBC, R_XBC + SSD_XBC), cw_ref[...], SUBLANES, rows) + cb_ref[...]
        xbc = xbc * _sigmoid(xbc)
        xbc_scr[blk] = xbc
        xdt = main_ref[:, R_DT:R_DT + LANES].T[0:N_SERIES, :] + dtb_ref[...]
        dtt_all = jnp.maximum(xdt, 0.0) + jnp.log1p(jnp.exp(-jnp.abs(xdt)))
        dt_scr[blk] = dtt_all
        ser = chunk_series(dtt_all)
        sts = {}
        for si in range(nsub):
            r0 = si * L
            for g in range(SSD_GROUPS):
                bt = xbc[r0:r0 + L, SSD_W + g * SSD_STATE:SSD_W + (g + 1) * SSD_STATE].T
                bt_scr[blk, g, :, r0:r0 + L] = bt
                sts[si, g] = chunk_state(bt, xbc[r0:r0 + L, g * LANES:(g + 1) * LANES], ser[si][2],
                                         SSD_HEADS + 2 * g)
        for g in range(SSD_GROUPS):
            hg = h_scr[g]
            for si in reversed(range(nsub)):
                hb_scr[blk * nsub + si, g] = hg
                hg = hg * pick2(ser[si][3], SSD_HEADS + 2 * g) + sts[si, g]
            h_scr[g] = hg

    @pl.when(sweep == 1)
    def _():
        u = ext_cols(R_SCC, R_SCC + SC_W) * ext_cols(R_SCH, R_SCH + SC_W)
        yc = main_ref[:, R_SCB:R_SCB + SC_W] * _conv3_rows(u, scw_ref[...], SUBLANES, rows)
        o_ref[:, 0:SC_W] = yc.astype(BF16)
        zpad = jnp.zeros((LANES - N_SERIES, L), F32)
        masks = (coli <= rowi, coli >= rowi)
        ser = chunk_series(dt_scr[blk])
        bcs = []
        for si in range(nsub):
            cs_col = jnp.concatenate([ser[si][1], zpad], axis=0).T
            bcs.append(jnp.dot(jnp.concatenate(_split3(cs_col), axis=1), sel_ref[...], preferred_element_type=F32))
        pairs = [(si, g) for si in range(nsub) for g in range(SSD_GROUPS)]
        xs, cbf, bts, gms, sts, ys = {}, {}, {}, {}, {}, {}
        for si, g in pairs:
            r0 = si * L
            xs[si, g] = xbc_scr[blk, r0:r0 + L, g * LANES:(g + 1) * LANES]
            cbf[si, g] = xbc_scr[blk, r0:r0 + L, SSD_W + (SSD_GROUPS + g) * SSD_STATE:
                                 SSD_W + (SSD_GROUPS + g + 1) * SSD_STATE].astype(BF16)
            bts[si, g] = bt_scr[blk, g, :, r0:r0 + L]
            gms[si, g] = jnp.dot(cbf[si, g], bts[si, g].astype(BF16), preferred_element_type=F32)
        for si, g in pairs:
            sts[si, g] = chunk_state(bts[si, g], xs[si, g], ser[si][2], 2 * g)
        for si, g in pairs:
            dtt, cst = ser[si][0], ser[si][1]
            xs_lo = jnp.where(lo, xs[si, g], 0.0).astype(BF16)
            xs_hi = jnp.where(lo, 0.0, xs[si, g]).astype(BF16)
            ms = []
            for d in range(2):
                for h in range(2):
                    j = SSD_HEADS * d + 2 * g + h
                    seg = bcs[si][:, j * LANES:(j + 1) * LANES] - cst[j:j + 1, :]
                    decay = jnp.where(masks[d], jnp.exp(seg), 0.0)
                    ms.append((gms[si, g] * decay * dtt[j:j + 1, :]).astype(BF16))
            ys[si, g] = jnp.dot(jnp.concatenate(ms, axis=1), jnp.concatenate([xs_lo, xs_hi, xs_lo, xs_hi], axis=0),
                                preferred_element_type=F32)
        for g in range(SSD_GROUPS):
            hg = h_scr[g]
            for si in range(nsub):
                r0 = si * L
                hb = hb_scr[blk * nsub + si, g]
                ch = jnp.dot(cbf[si, g], jnp.concatenate([hg.astype(BF16), hb.astype(BF16)], axis=1),
                             preferred_element_type=F32)
                hg = hg * pick2(ser[si][3], 2 * g) + sts[si, g]
                pf = (N_SERIES + g) * LANES
                pb = (N_SERIES + SSD_GROUPS + g) * LANES
                y = (ys[si, g] + jnp.exp(bcs[si][:, pf:pf + LANES]) * ch[:, 0:LANES]
                     + jnp.exp(bcs[si][:, pb:pb + LANES]) * ch[:, LANES:])
                gl = slice(g * LANES, (g + 1) * LANES)
                y = y + dexp_ref[:, gl] * xs[si, g]
                z = main_ref[r0:r0 + L, R_Z + g * LANES:R_Z + (g + 1) * LANES]
                y = y * (z * _sigmoid(z))
                o_ref[r0:r0 + L, SC_W + g * LANES:SC_W + (g + 1) * LANES] = _rms(y, ng_ref[:, gl]).astype(BF16)
            h_scr[g] = hg


def _ssd_mixers(rest, dtb_col, alog_col, dexp, ng, scw, cw, cb, sel, layer, batch, seq, rows):
    t = rest.shape[0]
    nblk = seq // rows
    r8 = rows // SUBLANES

    def blk_of(s, i):
        return jnp.where(s == 0, nblk - 1 - i, i)

    small = lambda r, n: pl.BlockSpec((None, r, n), lambda bi, s, i: (layer, 0, 0))
    return pl.pallas_call(
        functools.partial(_ssd_kernel, rows=rows, nblk=nblk),
        grid=(batch, 2, nblk),
        in_specs=[small(N_SERIES, 1), small(N_SERIES, 1), small(1, SSD_W), small(1, SSD_W),
                  small(3, SC_W), small(3, SSD_XBC), small(1, SSD_XBC),
                  pl.BlockSpec((3 * LANES, SEL_COLS), lambda bi, s, i: (0, 0)),
                  pl.BlockSpec((rows, REST_W), lambda bi, s, i: (bi * nblk + blk_of(s, i), 0)),
                  pl.BlockSpec((SUBLANES, REST_W),
                               lambda bi, s, i: (jnp.maximum((bi * nblk + blk_of(s, i)) * r8 - 1, 0), 0)),
                  pl.BlockSpec((SUBLANES, REST_W),
                               lambda bi, s, i: (jnp.minimum((bi * nblk + blk_of(s, i) + 1) * r8,
                                                             t // SUBLANES - 1), 0))],
        out_specs=pl.BlockSpec((rows, SC_W + SSD_W), lambda bi, s, i: (bi * nblk + s * i, 0)),
        out_shape=jax.ShapeDtypeStruct((t, SC_W + SSD_W), BF16),
        scratch_shapes=[pltpu.VMEM((nblk, rows, SSD_XBC), F32),
                        pltpu.VMEM((nblk, SSD_GROUPS, SSD_STATE, rows), F32),
                        pltpu.VMEM((nblk, N_SERIES, rows), F32),
                        pltpu.VMEM((seq // SSD_CHUNK, SSD_GROUPS, SSD_STATE, LANES), F32),
                        pltpu.VMEM((SSD_GROUPS, SSD_STATE, LANES), F32)],
        compiler_params=pltpu.CompilerParams(dimension_semantics=("arbitrary", "arbitrary", "arbitrary"),
                                             vmem_limit_bytes=VMEM_LIMIT),
        name="ssd_mixers",
    )(dtb_col, alog_col, dexp, ng, scw, cw, cb, sel, rest, rest, rest)


def _ffn_kernel(x_ref, xp_ref, xn_ref, ya_ref, yap_ref, yan_ref, ym_ref, ymp_ref, ymn_ref,
                wout_ref, g_ref, wup_ref, cwg_ref, cwu_ref, cbg_ref, cbu_ref, wd_ref, fg_ref,
                o_ref, h_scr, xnew_scr, acc_scr, ua_scr, ub_scr, *, tm, tiles_per_seq, final_norm):
    ti = pl.program_id(0) % tiles_per_seq
    has_prev = ti > 0
    has_next = ti < tiles_per_seq - 1
    ext_rows = tm + 2 * HALO

    def ext(prev, main, nxt):
        return jnp.concatenate([prev[...], main[...], nxt[...]], axis=0)

    mix = (jnp.dot(ext(yap_ref, ya_ref, yan_ref), wout_ref[0:ATTN_W, :], preferred_element_type=F32)
           + jnp.dot(ext(ymp_ref, ym_ref, ymn_ref), wout_ref[ATTN_W:2 * ATTN_W, :], preferred_element_type=F32))
    xnew = ext(xp_ref, x_ref, xn_ref) + mix
    row = lax.broadcasted_iota(jnp.int32, (ext_rows, 1), 0)
    valid = (row >= jnp.where(has_prev, 0, HALO)) & (row < jnp.where(has_next, ext_rows, HALO + tm))
    h_scr[...] = jnp.where(valid, _rms(xnew, g_ref[...]), 0.0).astype(BF16)
    xnew_scr[...] = xnew[HALO:HALO + tm]
    acc_scr[...] = jnp.zeros(acc_scr.shape, F32)

    def up_proj(j, u_ref):
        hb = h_scr[...]
        c0 = pl.multiple_of(j * FF_CHUNK, FF_CHUNK)
        u_ref[0] = jnp.dot(hb, wup_ref[:, pl.ds(c0, FF_CHUNK)], preferred_element_type=F32)
        u_ref[1] = jnp.dot(hb, wup_ref[:, pl.ds(D_FF + c0, FF_CHUNK)], preferred_element_type=F32)

    def conv_rows(u_ref, idx, w):
        out = u_ref[idx, HALO - 1:HALO - 1 + tm, :] * w[0:1]
        out = out + u_ref[idx, HALO:HALO + tm, :] * w[1:2]
        out = out + u_ref[idx, HALO + 1:HALO + 1 + tm, :] * w[2:3]
        return out

    def gate_down(j, u_ref):
        cg = conv_rows(u_ref, 0, cwg_ref[j]) + cbg_ref[j]
        cu = conv_rows(u_ref, 1, cwu_ref[j]) + cbu_ref[j]
        act = (cg * _sigmoid(cg)) * cu
        acc_scr[...] += jnp.dot(act.astype(BF16), wd_ref[j], preferred_element_type=F32)

    def ff_pair(i, carry):
        j = 2 * i
        up_proj(j + 1, ub_scr)
        gate_down(j, ua_scr)
        up_proj(j + 2, ua_scr)
        gate_down(j + 1, ub_scr)
        return carry

    assert N_FF_CHUNKS % 2 == 1
    up_proj(0, ua_scr)
    lax.fori_loop(0, N_FF_CHUNKS // 2, ff_pair, 0)
    gate_down(N_FF_CHUNKS - 1, ua_scr)
    out = xnew_scr[...] + acc_scr[...]
    if final_norm:
        out = _rms(out, fg_ref[...])
    o_ref[...] = out


def _ffn(x, ya, ym, w_out, g, w_up, cwg, cwu, cbg, cbu, wd, fg, layer, seq, tm, final_norm):
    t = x.shape[0]
    tiles_per_seq = seq // tm
    hb = tm // HALO
    nh = t // HALO

    def main(w):
        return pl.BlockSpec((tm, w), lambda i: (i, 0))

    def prev(w):
        return pl.BlockSpec((HALO, w), lambda i: (jnp.maximum(i * hb - 1, 0), 0))

    def nxt(w):
        return pl.BlockSpec((HALO, w), lambda i: (jnp.minimum((i + 1) * hb, nh - 1), 0))

    def resident(shape):
        nd = len(shape)
        return pl.BlockSpec((None,) + shape, lambda i: (layer,) + (0,) * nd,
                            pipeline_mode=pl.Buffered(1))

    ext_rows = tm + 2 * HALO
    return pl.pallas_call(
        functools.partial(_ffn_kernel, tm=tm, tiles_per_seq=tiles_per_seq, final_norm=final_norm),
        grid=(t // tm,),
        in_specs=[main(D_MODEL), prev(D_MODEL), nxt(D_MODEL),
                  main(ATTN_W), prev(ATTN_W), nxt(ATTN_W),
                  main(SC_W + SSD_W), prev(SC_W + SSD_W), nxt(SC_W + SSD_W),
                  resident((D_MODEL, D_MODEL)),
                  resident((1, D_MODEL)),
                  resident((D_MODEL, 2 * D_FF)),
                  resident((N_FF_CHUNKS, 3, FF_CHUNK)),
                  resident((N_FF_CHUNKS, 3, FF_CHUNK)),
                  resident((N_FF_CHUNKS, 1, FF_CHUNK)),
                  resident((N_FF_CHUNKS, 1, FF_CHUNK)),
                  resident((N_FF_CHUNKS, FF_CHUNK, D_MODEL)),
                  pl.BlockSpec((1, D_MODEL), lambda i: (0, 0))],
        out_specs=main(D_MODEL),
        out_shape=jax.ShapeDtypeStruct((t, D_MODEL), F32),
        scratch_shapes=[pltpu.VMEM((ext_rows, D_MODEL), BF16),
                        pltpu.VMEM((tm, D_MODEL), F32),
                        pltpu.VMEM((tm, D_MODEL), F32),
                        pltpu.VMEM((2, ext_rows, FF_CHUNK), F32),
                        pltpu.VMEM((2, ext_rows, FF_CHUNK), F32)],
        compiler_params=pltpu.CompilerParams(dimension_semantics=("arbitrary",),
                                             vmem_limit_bytes=VMEM_LIMIT),
        name="ffn",
    )(x, x, x, ya, ya, ya, ym, ym, ym, w_out, g, w_up, cwg, cwu, cbg, cbu, wd, fg)


def kernel(x, positions, norm_mix_g, w_in, lam_q1, lam_k1, lam_q2, lam_k2, subln_g, sc_conv_w, ssd_conv_w,
           ssd_conv_b, ssd_dt_bias, ssd_a_log, ssd_d, ssd_norm_g, w_out, norm_ffn_g, w_up, ffn_conv_w,
           ffn_conv_b, w_down, final_norm_g):
    batch, seq, _ = x.shape
    depth = w_in.shape[0]
    t = batch * seq
    tm = min(512, seq)
    tq = min(512, seq)
    ssd_rows = min(512, seq)

    w_in_b = jnp.pad(w_in, ((0, 0), (0, 0), (0, IN_PAD - IN_COLS))).astype(BF16)
    w_out_b = w_out.astype(BF16)
    w_up_b = w_up.astype(BF16)
    wd = w_down.astype(BF16).reshape(depth, N_FF_CHUNKS, FF_CHUNK, D_MODEL)
    fcw = ffn_conv_w.reshape(depth, 3, 2, N_FF_CHUNKS, FF_CHUNK)
    cwg = fcw[:, :, 0].transpose(0, 2, 1, 3)
    cwu = fcw[:, :, 1].transpose(0, 2, 1, 3)
    fcb = ffn_conv_b.reshape(depth, 2, N_FF_CHUNKS, 1, FF_CHUNK)
    cbg, cbu = fcb[:, 0], fcb[:, 1]
    row3 = lambda a: a.reshape(depth, 1, -1)
    dtb = ssd_dt_bias.reshape(depth, N_SERIES, 1)
    alog = ssd_a_log.reshape(depth, N_SERIES, 1)
    sel = _select_matrix()
    dexp = jnp.repeat(ssd_d, SSD_W // SSD_HEADS, axis=-1).reshape(depth, 1, SSD_W)
    fg = final_norm_g.reshape(1, D_MODEL)

    rope = _rope_tables(positions)
    xf = x.reshape(t, D_MODEL)
    for l in range(depth):
        lam_init = 0.8 - 0.6 * math.exp(-0.3 * l)
        q, k, vt, rest = _in_proj(xf, row3(norm_mix_g), w_in_b, rope, l, tm)
        ya = _attention(q, k, vt, row3(lam_q1), row3(lam_k1), row3(lam_q2), row3(lam_k2),
                        subln_g.reshape(depth, V_DIM, 1), l, lam_init, batch, seq, tq)
        ym = _ssd_mixers(rest, dtb, alog, dexp, row3(ssd_norm_g), sc_conv_w, ssd_conv_w, row3(ssd_conv_b), sel,
                         l, batch, seq, ssd_rows)
        xf = _ffn(xf, ya, ym, w_out_b, row3(norm_ffn_g), w_up_b, cwg, cwu, cbg, cbu, wd, fg,
                  l, seq, tm, l == depth - 1)
    return xf.reshape(batch, seq, D_MODEL)
```

```python
import functools
import math

import jax
import jax.numpy as jnp
from jax import lax
from jax.experimental import pallas as pl
from jax.experimental.pallas import tpu as pltpu

F32 = jnp.float32
BF16 = jnp.bfloat16

D_MODEL = 1024
EPS = 1e-5
N_ATTN_HEADS = 4
QK_DIM = 64
V_DIM = 128
ATTN_W = N_ATTN_HEADS * V_DIM
ROPE_THETA = 500000.0
ROT_DIM = QK_DIM // 4
Q_SCALE = QK_DIM ** -0.5 * math.log2(math.e)
SC_W = 256
SSD_W = 256
SSD_STATE = 128
SSD_GROUPS = 2
SSD_HEADS = 4
SSD_CHUNK = 128
SSD_XBC = SSD_W + 2 * SSD_GROUPS * SSD_STATE
D_FF = 2816
IN_COLS = 3336

QKV_COLS = 3 * ATTN_W
REST_REAL = IN_COLS - QKV_COLS
REST_W = 1920
IN_PAD = QKV_COLS + REST_W
R_SCB, R_SCC, R_SCH, R_Z, R_XBC, R_DT = 0, 256, 512, 768, 1024, 1792

LANES = 128
SUBLANES = 8
BF16_SUBLANES = 16
VMEM_LIMIT = 56 * 1024 * 1024

FF_CHUNK = 256
N_FF_CHUNKS = D_FF // FF_CHUNK
HALO = BF16_SUBLANES
CONV_HALO = SUBLANES


def _sigmoid(x):
    return 1.0 / (1.0 + jnp.exp(-x))


def _rms(x, g):
    ms = jnp.mean(x * x, axis=-1, keepdims=True)
    return x * lax.rsqrt(ms + EPS) * g


def _conv3_rows(ext, w, lo, n):
    tot = ext.shape[0]
    up = pltpu.roll(ext, 1, 0)
    dn = pltpu.roll(ext, tot - 1, 0)
    out = up[lo:lo + n] * w[0:1]
    out = out + ext[lo:lo + n] * w[1:2]
    out = out + dn[lo:lo + n] * w[2:3]
    return out


def _rope_kernel(pos_ref, invf_ref, o_ref):
    pos = pos_ref[...].astype(F32)
    ang = pos * invf_ref[...]
    lane = lax.broadcasted_iota(jnp.int32, ang.shape, 1) & (QK_DIM - 1)
    c = jnp.cos(ang)
    s = jnp.sin(ang)
    half = ROT_DIM // 2
    o_ref[:, 0:LANES] = c
    o_ref[:, LANES:2 * LANES] = jnp.where(lane < half, -s, 0.0)
    o_ref[:, 2 * LANES:3 * LANES] = jnp.where((lane >= half) & (lane < ROT_DIM), s, 0.0)


def _rope_tables(positions):
    t = positions.size
    tm = min(t, 2048)
    half = ROT_DIM // 2
    inv_freq = ROPE_THETA ** (-jnp.arange(0, ROT_DIM, 2, dtype=F32) / ROT_DIM)
    lane = jnp.arange(LANES) % QK_DIM
    invf = jnp.where(lane < ROT_DIM, inv_freq[lane % half], 0.0).astype(F32)[None, :]
    return pl.pallas_call(
        _rope_kernel,
        grid=(t // tm,),
        in_specs=[pl.BlockSpec((tm, 1), lambda i: (i, 0)),
                  pl.BlockSpec((1, LANES), lambda i: (0, 0))],
        out_specs=pl.BlockSpec((tm, 3 * LANES), lambda i: (i, 0)),
        out_shape=jax.ShapeDtypeStruct((t, 3 * LANES), F32),
        name="rope_tables",
    )(positions.reshape(t, 1), invf)


def _inproj_kernel(x_ref, g_ref, w_ref, rope_ref, q_ref, k_ref, v_ref, r_ref, h_scr):
    h_scr[...] = _rms(x_ref[...], g_ref[...]).astype(BF16)
    c = rope_ref[:, 0:LANES]
    s1 = rope_ref[:, LANES:2 * LANES]
    s2 = rope_ref[:, 2 * LANES:3 * LANES]
    half = ROT_DIM // 2

    def rot(t):
        return t * c + pltpu.roll(t, LANES - half, 1) * s1 + pltpu.roll(t, half, 1) * s2

    cw = 2 * LANES
    for ci in range(2 * ATTN_W // cw):
        r = jnp.dot(h_scr[...], w_ref[:, ci * cw:(ci + 1) * cw], preferred_element_type=F32)
        for hf in range(2):
            t = rot(r[:, hf * LANES:(hf + 1) * LANES])
            col = ci * cw + hf * LANES
            if col < ATTN_W:
                q_ref[:, col:col + LANES] = (t * Q_SCALE).astype(BF16)
            else:
                k_ref[:, col - ATTN_W:col - ATTN_W + LANES] = t.astype(BF16)
    for ci in range(ATTN_W // cw):
        c0 = 2 * ATTN_W + ci * cw
        r = jnp.dot(h_scr[...], w_ref[:, c0:c0 + cw], preferred_element_type=F32)
        v_ref[ci * cw:(ci + 1) * cw, :] = r.T.astype(BF16)
    c0 = 0
    while c0 < REST_W:
        w = min(cw, REST_W - c0)
        r_ref[:, c0:c0 + w] = jnp.dot(h_scr[...], w_ref[:, QKV_COLS + c0:QKV_COLS + c0 + w],
                                      preferred_element_type=F32)
        c0 += w


def _in_proj(x, g, w_in, rope, layer, tm):
    t = x.shape[0]
    return pl.pallas_call(
        _inproj_kernel,
        grid=(t // tm,),
        in_specs=[pl.BlockSpec((tm, D_MODEL), lambda i: (i, 0)),
                  pl.BlockSpec((None, 1, D_MODEL), lambda i: (layer, 0, 0)),
                  pl.BlockSpec((None, D_MODEL, IN_PAD), lambda i: (layer, 0, 0)),
                  pl.BlockSpec((tm, 3 * LANES), lambda i: (i, 0))],
        out_specs=[pl.BlockSpec((tm, ATTN_W), lambda i: (i, 0)),
                   pl.BlockSpec((tm, ATTN_W), lambda i: (i, 0)),
                   pl.BlockSpec((None, ATTN_W, tm), lambda i: (i, 0, 0)),
                   pl.BlockSpec((tm, REST_W), lambda i: (i, 0))],
        out_shape=[jax.ShapeDtypeStruct((t, ATTN_W), BF16),
                   jax.ShapeDtypeStruct((t, ATTN_W), BF16),
                   jax.ShapeDtypeStruct((t // tm, ATTN_W, tm), BF16),
                   jax.ShapeDtypeStruct((t, REST_W), F32)],
        scratch_shapes=[pltpu.VMEM((tm, D_MODEL), BF16)],
        compiler_params=pltpu.CompilerParams(dimension_semantics=("arbitrary",),
                                             vmem_limit_bytes=VMEM_LIMIT),
        name="in_proj",
    )(x, g, w_in, rope)


def _attn_kernel(lq1_ref, lk1_ref, lq2_ref, lk2_ref, sg_ref, q_ref, k_ref, vt_ref, o_ref,
                 qp_scr, acc_scr, *, lam_init, tq):
    seq = k_ref.shape[0]
    nk, _, tk = vt_ref.shape
    lam = (jnp.exp(jnp.sum(lq1_ref[...] * lk1_ref[...], axis=-1, keepdims=True))
           - jnp.exp(jnp.sum(lq2_ref[...] * lk2_ref[...], axis=-1, keepdims=True)) + lam_init)
    lane = lax.broadcasted_iota(jnp.int32, (tq, V_DIM), 1)
    nt = (((1,), (1,)), ((), ()))

    def q_tile(qi, carry):
        q0 = pl.multiple_of(qi * tq, tq)
        q = q_ref[pl.ds(q0, tq), :]
        zero = jnp.zeros_like(q)
        qp_scr[0:tq, :] = jnp.where(lane < QK_DIM, q, zero)
        qp_scr[tq:2 * tq, :] = jnp.where(lane >= QK_DIM, q, zero)
        acc_scr[...] = jnp.zeros(acc_scr.shape, F32)

        def scores(j):
            kb = k_ref[j * tk:(j + 1) * tk, :]
            sts = [lax.dot_general(kb, qp_scr[c * tq:(c + 1) * tq, :], nt, preferred_element_type=F32) for c in range(2)]
            return sts, [jnp.max(st, axis=0, keepdims=True) for st in sts]

        m_run = [jnp.full((1, tq), -jnp.inf, F32) for _ in range(2)]
        ones = jnp.ones((BF16_SUBLANES, tk), BF16)
        sts, cmax = scores(0)
        for j in range(nk):
            nxt = scores(j + 1) if j + 1 < nk else None
            vt1 = jnp.concatenate([vt_ref[j], ones], axis=0)
            for c in range(2):
                cols = slice(c * tq, (c + 1) * tq)
                m_new = jnp.maximum(m_run[c], cmax[c])
                alpha = jnp.exp2(m_run[c] - m_new)
                p = jnp.exp2(sts[c] - m_new).astype(BF16)
                acc_scr[:, cols] = alpha * acc_scr[:, cols] + jnp.dot(vt1, p, preferred_element_type=F32)
                m_run[c] = m_new
            if nxt is not None:
                sts, cmax = nxt
        o = acc_scr[0:V_DIM, :] / acc_scr[V_DIM:V_DIM + 1, :]
        o = o[:, 0:tq] - lam * o[:, tq:2 * tq]
        ms = jnp.mean(o * o, axis=0, keepdims=True)
        y = o * lax.rsqrt(ms + EPS) * sg_ref[...] * (1.0 - lam_init)
        o_ref[pl.ds(q0, tq), :] = y.T.astype(BF16)
        return carry

    lax.fori_loop(0, seq // tq, q_tile, 0)


def _attention(q, k, vt, lq1, lk1, lq2, lk2, subg_col, layer, lam_init, batch, seq, tq):
    t = q.shape[0]
    tk = vt.shape[2]
    small = lambda n: pl.BlockSpec((None, 1, n), lambda bi, hi: (layer, 0, 0))
    head = pl.BlockSpec((seq, V_DIM), lambda bi, hi: (bi, hi))
    return pl.pallas_call(
        functools.partial(_attn_kernel, lam_init=lam_init, tq=tq),
        grid=(batch, N_ATTN_HEADS),
        in_specs=[small(QK_DIM), small(QK_DIM), small(QK_DIM), small(QK_DIM),
                  pl.BlockSpec((None, V_DIM, 1), lambda bi, hi: (layer, 0, 0)),
                  head, head,
                  pl.BlockSpec((seq // tk, V_DIM, tk), lambda bi, hi: (bi, hi, 0))],
        out_specs=head,
        out_shape=jax.ShapeDtypeStruct((t, ATTN_W), BF16),
        scratch_shapes=[pltpu.VMEM((2 * tq, V_DIM), BF16),
                        pltpu.VMEM((V_DIM + BF16_SUBLANES, 2 * tq), F32)],
        compiler_params=pltpu.CompilerParams(dimension_semantics=("arbitrary", "arbitrary"),
                                             vmem_limit_bytes=VMEM_LIMIT),
        name="diff_attn",
    )(lq1, lk1, lq2, lk2, subg_col, q, k, vt)


N_SERIES = 2 * SSD_HEADS
N_PICK = 2 * SSD_GROUPS
SEL_COLS = (N_SERIES + N_PICK) * LANES


def _select_matrix():
    lane = jnp.arange(SEL_COLS)
    blk, within = lane // LANES, lane % LANES
    d, g = (blk - N_SERIES) // SSD_GROUPS, (blk - N_SERIES) % SSD_GROUPS
    src = jnp.where(blk < N_SERIES, blk, SSD_HEADS * d + 2 * g + (within >= LANES // 2))
    sel = (jnp.arange(LANES)[:, None] == src[None, :]).astype(BF16)
    return jnp.concatenate([sel, sel, sel], axis=0)


def _split3(x):
    hi = x.astype(BF16)
    r1 = x - hi.astype(F32)
    mid = r1.astype(BF16)
    lo = (r1 - mid.astype(F32)).astype(BF16)
    return hi, mid, lo


def _ssd_kernel(dtb_ref, alog_ref, dexp_ref, ng_ref, scw_ref, cw_ref, cb_ref, sel_ref, main_ref, prev_ref, next_ref,
                o_ref, xbc_scr, bt_scr, dt_scr, hb_scr, h_scr, *, rows, nblk):
    sweep = pl.program_id(1)
    i = pl.program_id(2)
    blk = jnp.where(sweep == 0, nblk - 1 - i, i)
    has_prev = blk > 0
    has_next = blk < nblk - 1
    L = SSD_CHUNK
    nsub = rows // L
    half = LANES // 2

    @pl.when(i == 0)
    def _():
        h_scr[...] = jnp.zeros(h_scr.shape, F32)

    def ext_cols(c0, c1):
        pv = jnp.where(has_prev, prev_ref[:, c0:c1], 0.0)
        nx = jnp.where(has_next, next_ref[:, c0:c1], 0.0)
        return jnp.concatenate([pv, main_ref[:, c0:c1], nx], axis=0)

    rowi = lax.broadcasted_iota(jnp.int32, (L, L), 0)
    coli = lax.broadcasted_iota(jnp.int32, (L, L), 1)
    upper = (rowi <= coli).astype(BF16)
    lower = (rowi >= coli).astype(BF16)
    lo = lax.broadcasted_iota(jnp.int32, (L, LANES), 1) < half
    fwd_rows = lax.broadcasted_iota(jnp.int32, (N_SERIES, 1), 0) < SSD_HEADS
    neg_a = -jnp.exp(alog_ref[...])

    def series(dtt):
        hi, mid, low = _split3(dtt * neg_a)
        pre = sum(jnp.dot(t, upper, preferred_element_type=F32) for t in (hi, mid, low))
        suf = sum(jnp.dot(t, lower, preferred_element_type=F32) for t in (hi, mid, low))
        cst = jnp.where(fwd_rows, pre, suf)
        tot = jnp.where(fwd_rows, cst[:, L - 1:L], cst[:, 0:1])
        return cst, tot

    def chunk_state(bt, xs_g, w, j0):
        return (jnp.dot((bt * w[j0:j0 + 1, :]).astype(BF16), jnp.where(lo, xs_g, 0.0).astype(BF16),
                        preferred_element_type=F32)
                + jnp.dot((bt * w[j0 + 1:j0 + 2, :]).astype(BF16), jnp.where(lo, 0.0, xs_g).astype(BF16),
                          preferred_element_type=F32))

    def pick2(col, j0):
        return jnp.where(lo[0:1], col[j0:j0 + 1, :], col[j0 + 1:j0 + 2, :])

    def chunk_series(dtt_all):
        out = []
        for si in range(nsub):
            dtt = dtt_all[:, si * L:(si + 1) * L]
            cst, tot = series(dtt)
            out.append((dtt, cst, dtt * jnp.exp(tot - cst), jnp.exp(tot)))
        return out

    @pl.when(sweep == 0)
    def _():
        xbc = _conv3_rows(ext_cols(R_XBC, R_XBC + SSD_XBC), cw_ref[...], SUBLANES, rows) + cb_ref[...]
        xbc = xbc * _sigmoid(xbc)
        xbc_scr[blk] = xbc
        xdt = main_ref[:, R_DT:R_DT + LANES].T[0:N_SERIES, :] + dtb_ref[...]
        dtt_all = jnp.maximum(xdt, 0.0) + jnp.log1p(jnp.exp(-jnp.abs(xdt)))
        dt_scr[blk] = dtt_all
        ser = chunk_series(dtt_all)
        sts = {}
        for si in range(nsub):
            r0 = si * L
            for g in range(SSD_GROUPS):
                bt = xbc[r0:r0 + L, SSD_W + g * SSD_STATE:SSD_W + (g + 1) * SSD_STATE].T
                bt_scr[blk, g, :, r0:r0 + L] = bt
                sts[si, g] = chunk_state(bt, xbc[r0:r0 + L, g * LANES:(g + 1) * LANES], ser[si][2],
                                         SSD_HEADS + 2 * g)
        for g in range(SSD_GROUPS):
            hg = h_scr[g]
            for si in reversed(range(nsub)):
                hb_scr[blk * nsub + si, g] = hg
                hg = hg * pick2(ser[si][3], SSD_HEADS + 2 * g) + sts[si, g]
            h_scr[g] = hg

    @pl.when(sweep == 1)
    def _():
        u = ext_cols(R_SCC, R_SCC + SC_W) * ext_cols(R_SCH, R_SCH + SC_W)
        yc = main_ref[:, R_SCB:R_SCB + SC_W] * _conv3_rows(u, scw_ref[...], SUBLANES, rows)
        o_ref[:, 0:SC_W] = yc.astype(BF16)
        zpad = jnp.zeros((LANES - N_SERIES, L), F32)
        masks = (coli <= rowi, coli >= rowi)
        ser = chunk_series(dt_scr[blk])
        bcs = []
        for si in range(nsub):
            cs_col = jnp.concatenate([ser[si][1], zpad], axis=0).T
            bcs.append(jnp.dot(jnp.concatenate(_split3(cs_col), axis=1), sel_ref[...], preferred_element_type=F32))
        pairs = [(si, g) for si in range(nsub) for g in range(SSD_GROUPS)]
        xs, cbf, bts, gms, sts, ys = {}, {}, {}, {}, {}, {}
        for si, g in pairs:
            r0 = si * L
            xs[si, g] = xbc_scr[blk, r0:r0 + L, g * LANES:(g + 1) * LANES]
            cbf[si, g] = xbc_scr[blk, r0:r0 + L, SSD_W + (SSD_GROUPS + g) * SSD_STATE:
                                 SSD_W + (SSD_GROUPS + g + 1) * SSD_STATE].astype(BF16)
            bts[si, g] = bt_scr[blk, g, :, r0:r0 + L]
            gms[si, g] = jnp.dot(cbf[si, g], bts[si, g].astype(BF16), preferred_element_type=F32)
        for si, g in pairs:
            sts[si, g] = chunk_state(bts[si, g], xs[si, g], ser[si][2], 2 * g)
        for si, g in pairs:
            dtt, cst = ser[si][0], ser[si][1]
            xs_lo = jnp.where(lo, xs[si, g], 0.0).astype(BF16)
            xs_hi = jnp.where(lo, 0.0, xs[si, g]).astype(BF16)
            ms = []
            for d in range(2):
                for h in range(2):
                    j = SSD_HEADS * d + 2 * g + h
                    seg = bcs[si][:, j * LANES:(j + 1) * LANES] - cst[j:j + 1, :]
                    decay = jnp.where(masks[d], jnp.exp(seg), 0.0)
                    ms.append((gms[si, g] * decay * dtt[j:j + 1, :]).astype(BF16))
            ys[si, g] = jnp.dot(jnp.concatenate(ms, axis=1), jnp.concatenate([xs_lo, xs_hi, xs_lo, xs_hi], axis=0),
                                preferred_element_type=F32)
        for g in range(SSD_GROUPS):
            hg = h_scr[g]
            for si in range(nsub):
                r0 = si * L
                hb = hb_scr[blk * nsub + si, g]
                ch = jnp.dot(cbf[si, g], jnp.concatenate([hg.astype(BF16), hb.astype(BF16)], axis=1),
                             preferred_element_type=F32)
                hg = hg * pick2(ser[si][3], 2 * g) + sts[si, g]
                pf = (N_SERIES + g) * LANES
                pb = (N_SERIES + SSD_GROUPS + g) * LANES
                y = (ys[si, g] + jnp.exp(bcs[si][:, pf:pf + LANES]) * ch[:, 0:LANES]
                     + jnp.exp(bcs[si][:, pb:pb + LANES]) * ch[:, LANES:])
                gl = slice(g * LANES, (g + 1) * LANES)
                y = y + dexp_ref[:, gl] * xs[si, g]
                z = main_ref[r0:r0 + L, R_Z + g * LANES:R_Z + (g + 1) * LANES]
                y = y * (z * _sigmoid(z))
                o_ref[r0:r0 + L, SC_W + g * LANES:SC_W + (g + 1) * LANES] = _rms(y, ng_ref[:, gl]).astype(BF16)
            h_scr[g] = hg


def _ssd_mixers(rest, dtb_col, alog_col, dexp, ng, scw, cw, cb, sel, layer, batch, seq, rows):
    t = rest.shape[0]
    nblk = seq // rows
    r8 = rows // SUBLANES

    def blk_of(s, i):
        return jnp.where(s == 0, nblk - 1 - i, i)

    small = lambda r, n: pl.BlockSpec((None, r, n), lambda bi, s, i: (layer, 0, 0))
    return pl.pallas_call(
        functools.partial(_ssd_kernel, rows=rows, nblk=nblk),
        grid=(batch, 2, nblk),
        in_specs=[small(N_SERIES, 1), small(N_SERIES, 1), small(1, SSD_W), small(1, SSD_W),
                  small(3, SC_W), small(3, SSD_XBC), small(1, SSD_XBC),
                  pl.BlockSpec((3 * LANES, SEL_COLS), lambda bi, s, i: (0, 0)),
                  pl.BlockSpec((rows, REST_W), lambda bi, s, i: (bi * nblk + blk_of(s, i), 0)),
                  pl.BlockSpec((SUBLANES, REST_W),
                               lambda bi, s, i: (jnp.maximum((bi * nblk + blk_of(s, i)) * r8 - 1, 0), 0)),
                  pl.BlockSpec((SUBLANES, REST_W),
                               lambda bi, s, i: (jnp.minimum((bi * nblk + blk_of(s, i) + 1) * r8,
                                                             t // SUBLANES - 1), 0))],
        out_specs=pl.BlockSpec((rows, SC_W + SSD_W), lambda bi, s, i: (bi * nblk + s * i, 0)),
        out_shape=jax.ShapeDtypeStruct((t, SC_W + SSD_W), BF16),
        scratch_shapes=[pltpu.VMEM((nblk, rows, SSD_XBC), F32),
                        pltpu.VMEM((nblk, SSD_GROUPS, SSD_STATE, rows), F32),
                        pltpu.VMEM((nblk, N_SERIES, rows), F32),
                        pltpu.VMEM((seq // SSD_CHUNK, SSD_GROUPS, SSD_STATE, LANES), F32),
                        pltpu.VMEM((SSD_GROUPS, SSD_STATE, LANES), F32)],
        compiler_params=pltpu.CompilerParams(dimension_semantics=("arbitrary", "arbitrary", "arbitrary"),
                                             vmem_limit_bytes=VMEM_LIMIT),
        name="ssd_mixers",
    )(dtb_col, alog_col, dexp, ng, scw, cw, cb, sel, rest, rest, rest)


def _ffn_kernel(x_ref, xp_ref, xn_ref, ya_ref, yap_ref, yan_ref, ym_ref, ymp_ref, ymn_ref,
                wout_ref, g_ref, wup_ref, cwg_ref, cwu_ref, cbg_ref, cbu_ref, wd_ref, fg_ref,
                o_ref, slab_scr, hp_scr, acc_scr, ua_scr, ub_scr, *, tm, tiles_per_seq, final_norm):
    ti = pl.program_id(0) % tiles_per_seq
    has_prev = ti > 0
    has_next = ti < tiles_per_seq - 1
    e_rows = tm + 2 * CONV_HALO
    seg = e_rows // SUBLANES
    n_slabs = D_MODEL // LANES

    def ext(prev, main, nxt):
        return jnp.concatenate([prev[...], main[...], nxt[...]], axis=0)

    mix = (jnp.dot(ext(yap_ref, ya_ref, yan_ref), wout_ref[0:ATTN_W, :], preferred_element_type=F32)
           + jnp.dot(ext(ymp_ref, ym_ref, ymn_ref), wout_ref[ATTN_W:2 * ATTN_W, :], preferred_element_type=F32))
    xnew = (ext(xp_ref, x_ref, xn_ref) + mix)[HALO - CONV_HALO:HALO - CONV_HALO + e_rows]
    row = lax.broadcasted_iota(jnp.int32, (e_rows, 1), 0)
    valid = (row >= jnp.where(has_prev, 0, CONV_HALO)) & (row < jnp.where(has_next, e_rows, CONV_HALO + tm))
    h = jnp.where(valid, _rms(xnew, g_ref[...]), 0.0)
    o_ref[...] = xnew[CONV_HALO:CONV_HALO + tm]
    acc_scr[...] = jnp.zeros(acc_scr.shape, F32)

    for c in range(n_slabs):
        slab_scr[c] = h[:, c * LANES:(c + 1) * LANES]
    for i in range(0, seg, 2):
        blk = [jnp.concatenate([slab_scr[c, pl.ds(i + d, SUBLANES, stride=seg), :] for c in range(n_slabs)], axis=1)
               for d in range(2)]
        hp_scr[SUBLANES * i:SUBLANES * (i + 2), :] = jnp.concatenate(blk, axis=0).astype(BF16)

    def up_proj(j, u_ref):
        hb = hp_scr[...]
        c0 = j * FF_CHUNK
        u_ref[0] = jnp.dot(hb, wup_ref[:, c0:c0 + FF_CHUNK], preferred_element_type=F32)
        u_ref[1] = jnp.dot(hb, wup_ref[:, D_FF + c0:D_FF + c0 + FF_CHUNK], preferred_element_type=F32)

    def conv_rows(u_ref, idx, w):
        last = e_rows - SUBLANES
        mid = (u_ref[idx, 0:last - SUBLANES, :] * w[0:1] + u_ref[idx, SUBLANES:last, :] * w[1:2]
               + u_ref[idx, 2 * SUBLANES:e_rows, :] * w[2:3])
        first = (pltpu.roll(u_ref[idx, last:e_rows, :], 1, 0) * w[0:1] + u_ref[idx, 0:SUBLANES, :] * w[1:2]
                 + u_ref[idx, SUBLANES:2 * SUBLANES, :] * w[2:3])
        end = (u_ref[idx, last - SUBLANES:last, :] * w[0:1] + u_ref[idx, last:e_rows, :] * w[1:2]
               + pltpu.roll(u_ref[idx, 0:SUBLANES, :], SUBLANES - 1, 0) * w[2:3])
        return jnp.concatenate([first, mid, end], axis=0)

    def gate_down(j, u_ref):
        cg = conv_rows(u_ref, 0, cwg_ref[j]) + cbg_ref[j]
        cu = conv_rows(u_ref, 1, cwu_ref[j]) + cbu_ref[j]
        act = (cg * _sigmoid(cg)) * cu
        acc_scr[...] += jnp.dot(act.astype(BF16), wd_ref[j], preferred_element_type=F32)

    bufs = (ua_scr, ub_scr)
    up_proj(0, bufs[0])
    for j in range(N_FF_CHUNKS):
        if j + 1 < N_FF_CHUNKS:
            up_proj(j + 1, bufs[(j + 1) % 2])
        gate_down(j, bufs[j % 2])
    for i in range(seg):
        for c in range(n_slabs):
            slab_scr[c, pl.ds(i, SUBLANES, stride=seg), :] = acc_scr[SUBLANES * i:SUBLANES * (i + 1),
                                                                    c * LANES:(c + 1) * LANES]
    for c in range(n_slabs):
        o_ref[:, c * LANES:(c + 1) * LANES] += slab_scr[c, CONV_HALO:CONV_HALO + tm, :]
    if final_norm:
        o_ref[...] = _rms(o_ref[...], fg_ref[...])


def _ffn(x, ya, ym, w_out, g, w_up, cwg, cwu, cbg, cbu, wd, fg, layer, seq, tm, final_norm):
    t = x.shape[0]
    tiles_per_seq = seq // tm
    hb = tm // HALO
    nh = t // HALO

    def main(w):
        return pl.BlockSpec((tm, w), lambda i: (i, 0))

    def prev(w):
        return pl.BlockSpec((HALO, w), lambda i: (jnp.maximum(i * hb - 1, 0), 0))

    def nxt(w):
        return pl.BlockSpec((HALO, w), lambda i: (jnp.minimum((i + 1) * hb, nh - 1), 0))

    def resident(shape):
        nd = len(shape)
        return pl.BlockSpec((None,) + shape, lambda i: (layer,) + (0,) * nd,
                            pipeline_mode=pl.Buffered(1))

    e_rows = tm + 2 * CONV_HALO
    assert e_rows % SUBLANES == 0 and (e_rows // SUBLANES) % 8 != 0
    return pl.pallas_call(
        functools.partial(_ffn_kernel, tm=tm, tiles_per_seq=tiles_per_seq, final_norm=final_norm),
        grid=(t // tm,),
        in_specs=[main(D_MODEL), prev(D_MODEL), nxt(D_MODEL),
                  main(ATTN_W), prev(ATTN_W), nxt(ATTN_W),
                  main(SC_W + SSD_W), prev(SC_W + SSD_W), nxt(SC_W + SSD_W),
                  resident((D_MODEL, D_MODEL)),
                  resident((1, D_MODEL)),
                  resident((D_MODEL, 2 * D_FF)),
                  resident((N_FF_CHUNKS, 3, FF_CHUNK)),
                  resident((N_FF_CHUNKS, 3, FF_CHUNK)),
                  resident((N_FF_CHUNKS, 1, FF_CHUNK)),
                  resident((N_FF_CHUNKS, 1, FF_CHUNK)),
                  resident((N_FF_CHUNKS, FF_CHUNK, D_MODEL)),
                  pl.BlockSpec((1, D_MODEL), lambda i: (0, 0))],
        out_specs=main(D_MODEL),
        out_shape=jax.ShapeDtypeStruct((t, D_MODEL), F32),
        scratch_shapes=[pltpu.VMEM((D_MODEL // LANES, e_rows, LANES), F32),
                        pltpu.VMEM((e_rows, D_MODEL), BF16),
                        pltpu.VMEM((e_rows, D_MODEL), F32),
                        pltpu.VMEM((2, e_rows, FF_CHUNK), F32),
                        pltpu.VMEM((2, e_rows, FF_CHUNK), F32)],
        compiler_params=pltpu.CompilerParams(dimension_semantics=("arbitrary",),
                                             vmem_limit_bytes=VMEM_LIMIT),
        name="ffn",
    )(x, x, x, ya, ya, ya, ym, ym, ym, w_out, g, w_up, cwg, cwu, cbg, cbu, wd, fg)


def kernel(x, positions, norm_mix_g, w_in, lam_q1, lam_k1, lam_q2, lam_k2, subln_g, sc_conv_w, ssd_conv_w,
           ssd_conv_b, ssd_dt_bias, ssd_a_log, ssd_d, ssd_norm_g, w_out, norm_ffn_g, w_up, ffn_conv_w,
           ffn_conv_b, w_down, final_norm_g):
    batch, seq, _ = x.shape
    depth = w_in.shape[0]
    t = batch * seq
    tm = min(512, seq)
    tq = min(512, seq)
    ssd_rows = min(512, seq)

    w_in_b = jnp.pad(w_in, ((0, 0), (0, 0), (0, IN_PAD - IN_COLS))).astype(BF16)
    w_out_b = w_out.astype(BF16)
    w_up_b = w_up.astype(BF16)
    wd = w_down.astype(BF16).reshape(depth, N_FF_CHUNKS, FF_CHUNK, D_MODEL)
    fcw = ffn_conv_w.reshape(depth, 3, 2, N_FF_CHUNKS, FF_CHUNK)
    cwg = fcw[:, :, 0].transpose(0, 2, 1, 3)
    cwu = fcw[:, :, 1].transpose(0, 2, 1, 3)
    fcb = ffn_conv_b.reshape(depth, 2, N_FF_CHUNKS, 1, FF_CHUNK)
    cbg, cbu = fcb[:, 0], fcb[:, 1]
    row3 = lambda a: a.reshape(depth, 1, -1)
    dtb = ssd_dt_bias.reshape(depth, N_SERIES, 1)
    alog = ssd_a_log.reshape(depth, N_SERIES, 1)
    sel = _select_matrix()
    dexp = jnp.repeat(ssd_d, SSD_W // SSD_HEADS, axis=-1).reshape(depth, 1, SSD_W)
    fg = final_norm_g.reshape(1, D_MODEL)

    rope = _rope_tables(positions)
    xf = x.reshape(t, D_MODEL)
    for l in range(depth):
        lam_init = 0.8 - 0.6 * math.exp(-0.3 * l)
        q, k, vt, rest = _in_proj(xf, row3(norm_mix_g), w_in_b, rope, l, tm)
        ya = _attention(q, k, vt, row3(lam_q1), row3(lam_k1), row3(lam_q2), row3(lam_k2),
                        subln_g.reshape(depth, V_DIM, 1), l, lam_init, batch, seq, tq)
        ym = _ssd_mixers(rest, dtb, alog, dexp, row3(ssd_norm_g), sc_conv_w, ssd_conv_w, row3(ssd_conv_b), sel,
                         l, batch, seq, ssd_rows)
        xf = _ffn(xf, ya, ym, w_out_b, row3(norm_ffn_g), w_up_b, cwg, cwu, cbg, cbu, wd, fg,
                  l, seq, tm, l == depth - 1)
    return xf.reshape(batch, seq, D_MODEL)
```

```python
import functools
import math

import jax
import jax.numpy as jnp
from jax import lax
from jax.experimental import pallas as pl
from jax.experimental.pallas import tpu as pltpu

F32 = jnp.float32
BF16 = jnp.bfloat16

D_MODEL = 1024
EPS = 1e-5
N_ATTN_HEADS = 4
QK_DIM = 64
V_DIM = 128
ATTN_W = N_ATTN_HEADS * V_DIM
ROPE_THETA = 500000.0
ROT_DIM = QK_DIM // 4
Q_SCALE = QK_DIM ** -0.5 * math.log2(math.e)
SC_W = 256
SSD_W = 256
SSD_STATE = 128
SSD_GROUPS = 2
SSD_HEADS = 4
SSD_CHUNK = 128
SSD_XBC = SSD_W + 2 * SSD_GROUPS * SSD_STATE
D_FF = 2816
IN_COLS = 3336

QKV_COLS = 3 * ATTN_W
REST_REAL = IN_COLS - QKV_COLS
REST_W = 1920
IN_PAD = QKV_COLS + REST_W
R_SCB, R_SCC, R_SCH, R_Z, R_XBC, R_DT = 0, 256, 512, 768, 1024, 1792

LANES = 128
SUBLANES = 8
BF16_SUBLANES = 16
VMEM_LIMIT = 56 * 1024 * 1024

FF_CHUNK = 256
N_FF_CHUNKS = D_FF // FF_CHUNK
HALO = BF16_SUBLANES
CONV_HALO = SUBLANES


def _sigmoid(x):
    return 1.0 / (1.0 + jnp.exp(-x))


def _rms(x, g):
    ms = jnp.mean(x * x, axis=-1, keepdims=True)
    return x * lax.rsqrt(ms + EPS) * g


def _conv3_rows(ext, w, lo, n):
    tot = ext.shape[0]
    up = pltpu.roll(ext, 1, 0)
    dn = pltpu.roll(ext, tot - 1, 0)
    out = up[lo:lo + n] * w[0:1]
    out = out + ext[lo:lo + n] * w[1:2]
    out = out + dn[lo:lo + n] * w[2:3]
    return out


def _rope_kernel(pos_ref, invf_ref, o_ref):
    pos = pos_ref[...].astype(F32)
    ang = pos * invf_ref[...]
    lane = lax.broadcasted_iota(jnp.int32, ang.shape, 1) & (QK_DIM - 1)
    c = jnp.cos(ang)
    s = jnp.sin(ang)
    half = ROT_DIM // 2
    o_ref[:, 0:LANES] = c
    o_ref[:, LANES:2 * LANES] = jnp.where(lane < half, -s, 0.0)
    o_ref[:, 2 * LANES:3 * LANES] = jnp.where((lane >= half) & (lane < ROT_DIM), s, 0.0)


def _rope_tables(positions):
    t = positions.size
    tm = min(t, 2048)
    half = ROT_DIM // 2
    inv_freq = ROPE_THETA ** (-jnp.arange(0, ROT_DIM, 2, dtype=F32) / ROT_DIM)
    lane = jnp.arange(LANES) % QK_DIM
    invf = jnp.where(lane < ROT_DIM, inv_freq[lane % half], 0.0).astype(F32)[None, :]
    return pl.pallas_call(
        _rope_kernel,
        grid=(t // tm,),
        in_specs=[pl.BlockSpec((tm, 1), lambda i: (i, 0)),
                  pl.BlockSpec((1, LANES), lambda i: (0, 0))],
        out_specs=pl.BlockSpec((tm, 3 * LANES), lambda i: (i, 0)),
        out_shape=jax.ShapeDtypeStruct((t, 3 * LANES), F32),
        name="rope_tables",
    )(positions.reshape(t, 1), invf)


def _inproj_kernel(x_ref, g_ref, w_ref, rope_ref, q_ref, k_ref, v_ref, r_ref, h_scr):
    h_scr[...] = _rms(x_ref[...], g_ref[...]).astype(BF16)
    c = rope_ref[:, 0:LANES]
    s1 = rope_ref[:, LANES:2 * LANES]
    s2 = rope_ref[:, 2 * LANES:3 * LANES]
    half = ROT_DIM // 2

    def rot(t):
        return t * c + pltpu.roll(t, LANES - half, 1) * s1 + pltpu.roll(t, half, 1) * s2

    cw = 2 * LANES
    for ci in range(2 * ATTN_W // cw):
        r = jnp.dot(h_scr[...], w_ref[:, ci * cw:(ci + 1) * cw], preferred_element_type=F32)
        for hf in range(2):
            t = rot(r[:, hf * LANES:(hf + 1) * LANES])
            col = ci * cw + hf * LANES
            if col < ATTN_W:
                q_ref[:, col:col + LANES] = (t * Q_SCALE).astype(BF16)
            else:
                k_ref[:, col - ATTN_W:col - ATTN_W + LANES] = t.astype(BF16)
    for ci in range(ATTN_W // cw):
        c0 = 2 * ATTN_W + ci * cw
        r = jnp.dot(h_scr[...], w_ref[:, c0:c0 + cw], preferred_element_type=F32)
        v_ref[ci * cw:(ci + 1) * cw, :] = r.T.astype(BF16)
    c0 = 0
    while c0 < REST_W:
        w = min(cw, REST_W - c0)
        r_ref[:, c0:c0 + w] = jnp.dot(h_scr[...], w_ref[:, QKV_COLS + c0:QKV_COLS + c0 + w],
                                      preferred_element_type=F32)
        c0 += w


def _in_proj(x, g, w_in, rope, layer, tm):
    t = x.shape[0]
    return pl.pallas_call(
        _inproj_kernel,
        grid=(t // tm,),
        in_specs=[pl.BlockSpec((tm, D_MODEL), lambda i: (i, 0)),
                  pl.BlockSpec((None, 1, D_MODEL), lambda i: (layer, 0, 0)),
                  pl.BlockSpec((None, D_MODEL, IN_PAD), lambda i: (layer, 0, 0)),
                  pl.BlockSpec((tm, 3 * LANES), lambda i: (i, 0))],
        out_specs=[pl.BlockSpec((tm, ATTN_W), lambda i: (i, 0)),
                   pl.BlockSpec((tm, ATTN_W), lambda i: (i, 0)),
                   pl.BlockSpec((None, ATTN_W, tm), lambda i: (i, 0, 0)),
                   pl.BlockSpec((tm, REST_W), lambda i: (i, 0))],
        out_shape=[jax.ShapeDtypeStruct((t, ATTN_W), BF16),
                   jax.ShapeDtypeStruct((t, ATTN_W), BF16),
                   jax.ShapeDtypeStruct((t // tm, ATTN_W, tm), BF16),
                   jax.ShapeDtypeStruct((t, REST_W), F32)],
        scratch_shapes=[pltpu.VMEM((tm, D_MODEL), BF16)],
        compiler_params=pltpu.CompilerParams(dimension_semantics=("arbitrary",),
                                             vmem_limit_bytes=VMEM_LIMIT),
        name="in_proj",
    )(x, g, w_in, rope)


def _attn_kernel(lq1_ref, lk1_ref, lq2_ref, lk2_ref, sg_ref, q_ref, k_ref, vt_ref, o_ref,
                 qp_scr, acc_scr, *, lam_init, tq):
    seq = k_ref.shape[0]
    nk, _, tk = vt_ref.shape
    lam = (jnp.exp(jnp.sum(lq1_ref[...] * lk1_ref[...], axis=-1, keepdims=True))
           - jnp.exp(jnp.sum(lq2_ref[...] * lk2_ref[...], axis=-1, keepdims=True)) + lam_init)
    lane = lax.broadcasted_iota(jnp.int32, (tq, V_DIM), 1)
    nt = (((1,), (1,)), ((), ()))

    def q_tile(qi, carry):
        q0 = pl.multiple_of(qi * tq, tq)
        q = q_ref[pl.ds(q0, tq), :]
        zero = jnp.zeros_like(q)
        qp_scr[0:tq, :] = jnp.where(lane < QK_DIM, q, zero)
        qp_scr[tq:2 * tq, :] = jnp.where(lane >= QK_DIM, q, zero)
        acc_scr[...] = jnp.zeros(acc_scr.shape, F32)

        def scores(j):
            kb = k_ref[j * tk:(j + 1) * tk, :]
            sts = [lax.dot_general(kb, qp_scr[c * tq:(c + 1) * tq, :], nt, preferred_element_type=F32) for c in range(2)]
            return sts, [jnp.max(st, axis=0, keepdims=True) for st in sts]

        m_run = [jnp.full((1, tq), -jnp.inf, F32) for _ in range(2)]
        ones = jnp.ones((BF16_SUBLANES, tk), BF16)
        sts, cmax = scores(0)
        for j in range(nk):
            nxt = scores(j + 1) if j + 1 < nk else None
            vt1 = jnp.concatenate([vt_ref[j], ones], axis=0)
            for c in range(2):
                cols = slice(c * tq, (c + 1) * tq)
                m_new = jnp.maximum(m_run[c], cmax[c])
                alpha = jnp.exp2(m_run[c] - m_new)
                p = jnp.exp2(sts[c] - m_new).astype(BF16)
                acc_scr[:, cols] = alpha * acc_scr[:, cols] + jnp.dot(vt1, p, preferred_element_type=F32)
                m_run[c] = m_new
            if nxt is not None:
                sts, cmax = nxt
        o = acc_scr[0:V_DIM, :] / acc_scr[V_DIM:V_DIM + 1, :]
        o = o[:, 0:tq] - lam * o[:, tq:2 * tq]
        ms = jnp.mean(o * o, axis=0, keepdims=True)
        y = o * lax.rsqrt(ms + EPS) * sg_ref[...] * (1.0 - lam_init)
        o_ref[pl.ds(q0, tq), :] = y.T.astype(BF16)
        return carry

    lax.fori_loop(0, seq // tq, q_tile, 0)


def _attention(q, k, vt, lq1, lk1, lq2, lk2, subg_col, layer, lam_init, batch, seq, tq):
    t = q.shape[0]
    tk = vt.shape[2]
    small = lambda n: pl.BlockSpec((None, 1, n), lambda bi, hi: (layer, 0, 0))
    head = pl.BlockSpec((seq, V_DIM), lambda bi, hi: (bi, hi))
    return pl.pallas_call(
        functools.partial(_attn_kernel, lam_init=lam_init, tq=tq),
        grid=(batch, N_ATTN_HEADS),
        in_specs=[small(QK_DIM), small(QK_DIM), small(QK_DIM), small(QK_DIM),
                  pl.BlockSpec((None, V_DIM, 1), lambda bi, hi: (layer, 0, 0)),
                  head, head,
                  pl.BlockSpec((seq // tk, V_DIM, tk), lambda bi, hi: (bi, hi, 0))],
        out_specs=head,
        out_shape=jax.ShapeDtypeStruct((t, ATTN_W), BF16),
        scratch_shapes=[pltpu.VMEM((2 * tq, V_DIM), BF16),
                        pltpu.VMEM((V_DIM + BF16_SUBLANES, 2 * tq), F32)],
        compiler_params=pltpu.CompilerParams(dimension_semantics=("arbitrary", "arbitrary"),
                                             vmem_limit_bytes=VMEM_LIMIT),
        name="diff_attn",
    )(lq1, lk1, lq2, lk2, subg_col, q, k, vt)


N_SERIES = 2 * SSD_HEADS
N_PICK = 2 * SSD_GROUPS
SEL_COLS = (N_SERIES + N_PICK) * LANES


def _select_matrix():
    lane = jnp.arange(SEL_COLS)
    blk, within = lane // LANES, lane % LANES
    d, g = (blk - N_SERIES) // SSD_GROUPS, (blk - N_SERIES) % SSD_GROUPS
    src = jnp.where(blk < N_SERIES, blk, SSD_HEADS * d + 2 * g + (within >= LANES // 2))
    sel = (jnp.arange(LANES)[:, None] == src[None, :]).astype(BF16)
    return jnp.concatenate([sel, sel, sel], axis=0)


def _split3(x):
    hi = x.astype(BF16)
    r1 = x - hi.astype(F32)
    mid = r1.astype(BF16)
    lo = (r1 - mid.astype(F32)).astype(BF16)
    return hi, mid, lo


def _ssd_kernel(dtb_ref, alog_ref, dexp_ref, ng_ref, scw_ref, cw_ref, cb_ref, sel_ref, main_ref, prev_ref, next_ref,
                o_ref, xbc_scr, bt_scr, dt_scr, hb_scr, h_scr, *, rows, nblk):
    sweep = pl.program_id(1)
    i = pl.program_id(2)
    blk = jnp.where(sweep == 0, nblk - 1 - i, i)
    has_prev = blk > 0
    has_next = blk < nblk - 1
    L = SSD_CHUNK
    nsub = rows // L
    half = LANES // 2

    @pl.when(i == 0)
    def _():
        h_scr[...] = jnp.zeros(h_scr.shape, F32)

    def ext_cols(c0, c1):
        pv = jnp.where(has_prev, prev_ref[:, c0:c1], 0.0)
        nx = jnp.where(has_next, next_ref[:, c0:c1], 0.0)
        return jnp.concatenate([pv, main_ref[:, c0:c1], nx], axis=0)

    rowi = lax.broadcasted_iota(jnp.int32, (L, L), 0)
    coli = lax.broadcasted_iota(jnp.int32, (L, L), 1)
    upper = (rowi <= coli).astype(BF16)
    lower = (rowi >= coli).astype(BF16)
    lo = lax.broadcasted_iota(jnp.int32, (L, LANES), 1) < half
    fwd_rows = lax.broadcasted_iota(jnp.int32, (N_SERIES, 1), 0) < SSD_HEADS
    neg_a = -jnp.exp(alog_ref[...])

    def series(dtt):
        hi, mid, low = _split3(dtt * neg_a)
        pre = sum(jnp.dot(t, upper, preferred_element_type=F32) for t in (hi, mid, low))
        suf = sum(jnp.dot(t, lower, preferred_element_type=F32) for t in (hi, mid, low))
        cst = jnp.where(fwd_rows, pre, suf)
        tot = jnp.where(fwd_rows, cst[:, L - 1:L], cst[:, 0:1])
        return cst, tot

    def chunk_state(bt, xs_g, w, j0):
        return (jnp.dot((bt * w[j0:j0 + 1, :]).astype(BF16), jnp.where(lo, xs_g, 0.0).astype(BF16),
                        preferred_element_type=F32)
                + jnp.dot((bt * w[j0 + 1:j0 + 2, :]).astype(BF16), jnp.where(lo, 0.0, xs_g).astype(BF16),
                          preferred_element_type=F32))

    def pick2(col, j0):
        return jnp.where(lo[0:1], col[j0:j0 + 1, :], col[j0 + 1:j0 + 2, :])

    def chunk_series(dtt_all):
        out = []
        for si in range(nsub):
            dtt = dtt_all[:, si * L:(si + 1) * L]
            cst, tot = series(dtt)
            out.append((dtt, cst, dtt * jnp.exp(tot - cst), jnp.exp(tot)))
        return out

    @pl.when(sweep == 0)
    def _():
        xbc = _conv3_rows(ext_cols(R_XBC, R_XBC + SSD_XBC), cw_ref[...], SUBLANES, rows) + cb_ref[...]
        xbc = xbc * _sigmoid(xbc)
        xbc_scr[blk] = xbc
        xdt = main_ref[:, R_DT:R_DT + LANES].T[0:N_SERIES, :] + dtb_ref[...]
        dtt_all = jnp.maximum(xdt, 0.0) + jnp.log1p(jnp.exp(-jnp.abs(xdt)))
        dt_scr[blk] = dtt_all
        ser = chunk_series(dtt_all)
        sts = {}
        for si in range(nsub):
            r0 = si * L
            for g in range(SSD_GROUPS):
                bt = xbc[r0:r0 + L, SSD_W + g * SSD_STATE:SSD_W + (g + 1) * SSD_STATE].T
                bt_scr[blk, g, :, r0:r0 + L] = bt
                sts[si, g] = chunk_state(bt, xbc[r0:r0 + L, g * LANES:(g + 1) * LANES], ser[si][2],
                                         SSD_HEADS + 2 * g)
        for g in range(SSD_GROUPS):
            hg = h_scr[g]
            for si in reversed(range(nsub)):
                hb_scr[blk * nsub + si, g] = hg
                hg = hg * pick2(ser[si][3], SSD_HEADS + 2 * g) + sts[si, g]
            h_scr[g] = hg

    @pl.when(sweep == 1)
    def _():
        u = ext_cols(R_SCC, R_SCC + SC_W) * ext_cols(R_SCH, R_SCH + SC_W)
        yc = main_ref[:, R_SCB:R_SCB + SC_W] * _conv3_rows(u, scw_ref[...], SUBLANES, rows)
        o_ref[:, 0:SC_W] = yc.astype(BF16)
        zpad = jnp.zeros((LANES - N_SERIES, L), F32)
        masks = (coli <= rowi, coli >= rowi)
        ser = chunk_series(dt_scr[blk])
        bcs = []
        for si in range(nsub):
            cs_col = jnp.concatenate([ser[si][1], zpad], axis=0).T
            bcs.append(jnp.dot(jnp.concatenate(_split3(cs_col), axis=1), sel_ref[...], preferred_element_type=F32))
        pairs = [(si, g) for si in range(nsub) for g in range(SSD_GROUPS)]
        xs, cbf, bts, gms, sts, ys = {}, {}, {}, {}, {}, {}
        for si, g in pairs:
            r0 = si * L
            xs[si, g] = xbc_scr[blk, r0:r0 + L, g * LANES:(g + 1) * LANES]
            cbf[si, g] = xbc_scr[blk, r0:r0 + L, SSD_W + (SSD_GROUPS + g) * SSD_STATE:
                                 SSD_W + (SSD_GROUPS + g + 1) * SSD_STATE].astype(BF16)
            bts[si, g] = bt_scr[blk, g, :, r0:r0 + L]
            gms[si, g] = jnp.dot(cbf[si, g], bts[si, g].astype(BF16), preferred_element_type=F32)
        for si, g in pairs:
            sts[si, g] = chunk_state(bts[si, g], xs[si, g], ser[si][2], 2 * g)
        for si, g in pairs:
            dtt, cst = ser[si][0], ser[si][1]
            xs_lo = jnp.where(lo, xs[si, g], 0.0).astype(BF16)
            xs_hi = jnp.where(lo, 0.0, xs[si, g]).astype(BF16)
            ms = []
            for d in range(2):
                for h in range(2):
                    j = SSD_HEADS * d + 2 * g + h
                    seg = bcs[si][:, j * LANES:(j + 1) * LANES] - cst[j:j + 1, :]
                    decay = jnp.where(masks[d], jnp.exp(seg), 0.0)
                    ms.append((gms[si, g] * decay * dtt[j:j + 1, :]).astype(BF16))
            ys[si, g] = jnp.dot(jnp.concatenate(ms, axis=1), jnp.concatenate([xs_lo, xs_hi, xs_lo, xs_hi], axis=0),
                                preferred_element_type=F32)
        for g in range(SSD_GROUPS):
            hg = h_scr[g]
            for si in range(nsub):
                r0 = si * L
                hb = hb_scr[blk * nsub + si, g]
                ch = jnp.dot(cbf[si, g], jnp.concatenate([hg.astype(BF16), hb.astype(BF16)], axis=1),
                             preferred_element_type=F32)
                hg = hg * pick2(ser[si][3], 2 * g) + sts[si, g]
                pf = (N_SERIES + g) * LANES
                pb = (N_SERIES + SSD_GROUPS + g) * LANES
                y = (ys[si, g] + jnp.exp(bcs[si][:, pf:pf + LANES]) * ch[:, 0:LANES]
                     + jnp.exp(bcs[si][:, pb:pb + LANES]) * ch[:, LANES:])
                gl = slice(g * LANES, (g + 1) * LANES)
                y = y + dexp_ref[:, gl] * xs[si, g]
                z = main_ref[r0:r0 + L, R_Z + g * LANES:R_Z + (g + 1) * LANES]
                y = y * (z * _sigmoid(z))
                o_ref[r0:r0 + L, SC_W + g * LANES:SC_W + (g + 1) * LANES] = _rms(y, ng_ref[:, gl]).astype(BF16)
            h_scr[g] = hg


def _ssd_mixers(rest, dtb_col, alog_col, dexp, ng, scw, cw, cb, sel, layer, batch, seq, rows):
    t = rest.shape[0]
    nblk = seq // rows
    r8 = rows // SUBLANES

    def blk_of(s, i):
        return jnp.where(s == 0, nblk - 1 - i, i)

    small = lambda r, n: pl.BlockSpec((None, r, n), lambda bi, s, i: (layer, 0, 0))
    return pl.pallas_call(
        functools.partial(_ssd_kernel, rows=rows, nblk=nblk),
        grid=(batch, 2, nblk),
        in_specs=[small(N_SERIES, 1), small(N_SERIES, 1), small(1, SSD_W), small(1, SSD_W),
                  small(3, SC_W), small(3, SSD_XBC), small(1, SSD_XBC),
                  pl.BlockSpec((3 * LANES, SEL_COLS), lambda bi, s, i: (0, 0)),
                  pl.BlockSpec((rows, REST_W), lambda bi, s, i: (bi * nblk + blk_of(s, i), 0)),
                  pl.BlockSpec((SUBLANES, REST_W),
                               lambda bi, s, i: (jnp.maximum((bi * nblk + blk_of(s, i)) * r8 - 1, 0), 0)),
                  pl.BlockSpec((SUBLANES, REST_W),
                               lambda bi, s, i: (jnp.minimum((bi * nblk + blk_of(s, i) + 1) * r8,
                                                             t // SUBLANES - 1), 0))],
        out_specs=pl.BlockSpec((rows, SC_W + SSD_W), lambda bi, s, i: (bi * nblk + s * i, 0)),
        out_shape=jax.ShapeDtypeStruct((t, SC_W + SSD_W), BF16),
        scratch_shapes=[pltpu.VMEM((nblk, rows, SSD_XBC), F32),
                        pltpu.VMEM((nblk, SSD_GROUPS, SSD_STATE, rows), F32),
                        pltpu.VMEM((nblk, N_SERIES, rows), F32),
                        pltpu.VMEM((seq // SSD_CHUNK, SSD_GROUPS, SSD_STATE, LANES), F32),
                        pltpu.VMEM((SSD_GROUPS, SSD_STATE, LANES), F32)],
        compiler_params=pltpu.CompilerParams(dimension_semantics=("arbitrary", "arbitrary", "arbitrary"),
                                             vmem_limit_bytes=VMEM_LIMIT),
        name="ssd_mixers",
    )(dtb_col, alog_col, dexp, ng, scw, cw, cb, sel, rest, rest, rest)


def _ffn_kernel(x_ref, xp_ref, xn_ref, ya_ref, yap_ref, yan_ref, ym_ref, ymp_ref, ymn_ref,
                wout_ref, g_ref, wup_ref, cwg_ref, cwu_ref, cbg_ref, cbu_ref, wd_ref, fg_ref,
                o_ref, slab_scr, hp_scr, act_scr, ua_scr, ub_scr, *, tm, tiles_per_seq, final_norm):
    ti = pl.program_id(0) % tiles_per_seq
    has_prev = ti > 0
    has_next = ti < tiles_per_seq - 1
    e_rows = tm + 2 * CONV_HALO
    seg = e_rows // SUBLANES
    n_slabs = D_MODEL // LANES

    def ext(prev, main, nxt):
        return jnp.concatenate([prev[...], main[...], nxt[...]], axis=0)

    mix = (jnp.dot(ext(yap_ref, ya_ref, yan_ref), wout_ref[0:ATTN_W, :], preferred_element_type=F32)
           + jnp.dot(ext(ymp_ref, ym_ref, ymn_ref), wout_ref[ATTN_W:2 * ATTN_W, :], preferred_element_type=F32))
    xnew = (ext(xp_ref, x_ref, xn_ref) + mix)[HALO - CONV_HALO:HALO - CONV_HALO + e_rows]
    row = lax.broadcasted_iota(jnp.int32, (e_rows, 1), 0)
    valid = (row >= jnp.where(has_prev, 0, CONV_HALO)) & (row < jnp.where(has_next, e_rows, CONV_HALO + tm))
    h = jnp.where(valid, _rms(xnew, g_ref[...]), 0.0)
    o_ref[...] = xnew[CONV_HALO:CONV_HALO + tm]

    for c in range(n_slabs):
        slab_scr[c] = h[:, c * LANES:(c + 1) * LANES]
    for i in range(0, seg, 2):
        blk = [jnp.concatenate([slab_scr[c, pl.ds(i + d, SUBLANES, stride=seg), :] for c in range(n_slabs)], axis=1)
               for d in range(2)]
        hp_scr[SUBLANES * i:SUBLANES * (i + 2), :] = jnp.concatenate(blk, axis=0).astype(BF16)

    def up_proj(j, u_ref):
        hb = hp_scr[...]
        c0 = j * FF_CHUNK
        u_ref[0] = jnp.dot(hb, wup_ref[:, c0:c0 + FF_CHUNK], preferred_element_type=F32)
        u_ref[1] = jnp.dot(hb, wup_ref[:, D_FF + c0:D_FF + c0 + FF_CHUNK], preferred_element_type=F32)

    def conv_rows(u_ref, idx, w):
        last = e_rows - SUBLANES
        mid = (u_ref[idx, 0:last - SUBLANES, :] * w[0:1] + u_ref[idx, SUBLANES:last, :] * w[1:2]
               + u_ref[idx, 2 * SUBLANES:e_rows, :] * w[2:3])
        first = (pltpu.roll(u_ref[idx, last:e_rows, :], 1, 0) * w[0:1] + u_ref[idx, 0:SUBLANES, :] * w[1:2]
                 + u_ref[idx, SUBLANES:2 * SUBLANES, :] * w[2:3])
        end = (u_ref[idx, last - SUBLANES:last, :] * w[0:1] + u_ref[idx, last:e_rows, :] * w[1:2]
               + pltpu.roll(u_ref[idx, 0:SUBLANES, :], SUBLANES - 1, 0) * w[2:3])
        return jnp.concatenate([first, mid, end], axis=0)

    def gate_down(j, u_ref):
        cg = conv_rows(u_ref, 0, cwg_ref[j]) + cbg_ref[j]
        cu = conv_rows(u_ref, 1, cwu_ref[j]) + cbu_ref[j]
        act_scr[:, j * FF_CHUNK:(j + 1) * FF_CHUNK] = ((cg * _sigmoid(cg)) * cu).astype(BF16)

    bufs = (ua_scr, ub_scr)
    up_proj(0, bufs[0])
    for j in range(N_FF_CHUNKS):
        if j + 1 < N_FF_CHUNKS:
            up_proj(j + 1, bufs[(j + 1) % 2])
        gate_down(j, bufs[j % 2])
    down = jnp.dot(act_scr[...], wd_ref[...], preferred_element_type=F32)
    for i in range(seg):
        for c in range(n_slabs):
            slab_scr[c, pl.ds(i, SUBLANES, stride=seg), :] = down[SUBLANES * i:SUBLANES * (i + 1),
                                                                c * LANES:(c + 1) * LANES]
    for c in range(n_slabs):
        o_ref[:, c * LANES:(c + 1) * LANES] += slab_scr[c, CONV_HALO:CONV_HALO + tm, :]
    if final_norm:
        o_ref[...] = _rms(o_ref[...], fg_ref[...])


def _ffn(x, ya, ym, w_out, g, w_up, cwg, cwu, cbg, cbu, wd, fg, layer, seq, tm, final_norm):
    t = x.shape[0]
    tiles_per_seq = seq // tm
    hb = tm // HALO
    nh = t // HALO

    def main(w):
        return pl.BlockSpec((tm, w), lambda i: (i, 0))

    def prev(w):
        return pl.BlockSpec((HALO, w), lambda i: (jnp.maximum(i * hb - 1, 0), 0))

    def nxt(w):
        return pl.BlockSpec((HALO, w), lambda i: (jnp.minimum((i + 1) * hb, nh - 1), 0))

    def resident(shape):
        nd = len(shape)
        return pl.BlockSpec((None,) + shape, lambda i: (layer,) + (0,) * nd,
                            pipeline_mode=pl.Buffered(1))

    e_rows = tm + 2 * CONV_HALO
    assert e_rows % SUBLANES == 0 and (e_rows // SUBLANES) % 8 != 0
    return pl.pallas_call(
        functools.partial(_ffn_kernel, tm=tm, tiles_per_seq=tiles_per_seq, final_norm=final_norm),
        grid=(t // tm,),
        in_specs=[main(D_MODEL), prev(D_MODEL), nxt(D_MODEL),
                  main(ATTN_W), prev(ATTN_W), nxt(ATTN_W),
                  main(SC_W + SSD_W), prev(SC_W + SSD_W), nxt(SC_W + SSD_W),
                  resident((D_MODEL, D_MODEL)),
                  resident((1, D_MODEL)),
                  resident((D_MODEL, 2 * D_FF)),
                  resident((N_FF_CHUNKS, 3, FF_CHUNK)),
                  resident((N_FF_CHUNKS, 3, FF_CHUNK)),
                  resident((N_FF_CHUNKS, 1, FF_CHUNK)),
                  resident((N_FF_CHUNKS, 1, FF_CHUNK)),
                  resident((D_FF, D_MODEL)),
                  pl.BlockSpec((1, D_MODEL), lambda i: (0, 0))],
        out_specs=main(D_MODEL),
        out_shape=jax.ShapeDtypeStruct((t, D_MODEL), F32),
        scratch_shapes=[pltpu.VMEM((D_MODEL // LANES, e_rows, LANES), F32),
                        pltpu.VMEM((e_rows, D_MODEL), BF16),
                        pltpu.VMEM((e_rows, D_FF), BF16),
                        pltpu.VMEM((2, e_rows, FF_CHUNK), F32),
                        pltpu.VMEM((2, e_rows, FF_CHUNK), F32)],
        compiler_params=pltpu.CompilerParams(dimension_semantics=("arbitrary",),
                                             vmem_limit_bytes=VMEM_LIMIT),
        name="ffn",
    )(x, x, x, ya, ya, ya, ym, ym, ym, w_out, g, w_up, cwg, cwu, cbg, cbu, wd, fg)


def kernel(x, positions, norm_mix_g, w_in, lam_q1, lam_k1, lam_q2, lam_k2, subln_g, sc_conv_w, ssd_conv_w,
           ssd_conv_b, ssd_dt_bias, ssd_a_log, ssd_d, ssd_norm_g, w_out, norm_ffn_g, w_up, ffn_conv_w,
           ffn_conv_b, w_down, final_norm_g):
    batch, seq, _ = x.shape
    depth = w_in.shape[0]
    t = batch * seq
    tm = min(512, seq)
    tq = min(1024, seq)
    ssd_rows = min(512, seq)

    w_in_b = jnp.pad(w_in, ((0, 0), (0, 0), (0, IN_PAD - IN_COLS))).astype(BF16)
    w_out_b = w_out.astype(BF16)
    w_up_b = w_up.astype(BF16)
    wd = w_down.astype(BF16)
    fcw = ffn_conv_w.reshape(depth, 3, 2, N_FF_CHUNKS, FF_CHUNK)
    cwg = fcw[:, :, 0].transpose(0, 2, 1, 3)
    cwu = fcw[:, :, 1].transpose(0, 2, 1, 3)
    fcb = ffn_conv_b.reshape(depth, 2, N_FF_CHUNKS, 1, FF_CHUNK)
    cbg, cbu = fcb[:, 0], fcb[:, 1]
    row3 = lambda a: a.reshape(depth, 1, -1)
    dtb = ssd_dt_bias.reshape(depth, N_SERIES, 1)
    alog = ssd_a_log.reshape(depth, N_SERIES, 1)
    sel = _select_matrix()
    dexp = jnp.repeat(ssd_d, SSD_W // SSD_HEADS, axis=-1).reshape(depth, 1, SSD_W)
    fg = final_norm_g.reshape(1, D_MODEL)

    rope = _rope_tables(positions)
    xf = x.reshape(t, D_MODEL)
    for l in range(depth):
        lam_init = 0.8 - 0.6 * math.exp(-0.3 * l)
        q, k, vt, rest = _in_proj(xf, row3(norm_mix_g), w_in_b, rope, l, tm)
        ya = _attention(q, k, vt, row3(lam_q1), row3(lam_k1), row3(lam_q2), row3(lam_k2),
                        subln_g.reshape(depth, V_DIM, 1), l, lam_init, batch, seq, tq)
        ym = _ssd_mixers(rest, dtb, alog, dexp, row3(ssd_norm_g), sc_conv_w, ssd_conv_w, row3(ssd_conv_b), sel,
                         l, batch, seq, ssd_rows)
        xf = _ffn(xf, ya, ym, w_out_b, row3(norm_ffn_g), w_up_b, cwg, cwu, cbg, cbu, wd, fg,
                  l, seq, tm, l == depth - 1)
    return xf.reshape(batch, seq, D_MODEL)
```

```python
import functools
import math

import jax
import jax.numpy as jnp
from jax import lax
from jax.experimental import pallas as pl
from jax.experimental.pallas import tpu as pltpu

F32 = jnp.float32
BF16 = jnp.bfloat16

D_MODEL = 1024
EPS = 1e-5
N_ATTN_HEADS = 4
QK_DIM = 64
V_DIM = 128
ATTN_W = N_ATTN_HEADS * V_DIM
ROPE_THETA = 500000.0
ROT_DIM = QK_DIM // 4
Q_SCALE = QK_DIM ** -0.5 * math.log2(math.e)
SC_W = 256
SSD_W = 256
SSD_STATE = 128
SSD_GROUPS = 2
SSD_HEADS = 4
SSD_CHUNK = 128
SSD_XBC = SSD_W + 2 * SSD_GROUPS * SSD_STATE
D_FF = 2816
IN_COLS = 3336

QKV_COLS = 3 * ATTN_W
REST_REAL = IN_COLS - QKV_COLS
REST_W = 1920
IN_PAD = QKV_COLS + REST_W
R_SCB, R_SCC, R_SCH, R_Z, R_XBC, R_DT = 0, 256, 512, 768, 1024, 1792

LANES = 128
SUBLANES = 8
BF16_SUBLANES = 16
VMEM_LIMIT = 56 * 1024 * 1024

FF_CHUNK = 256
N_FF_CHUNKS = D_FF // FF_CHUNK
HALO = BF16_SUBLANES
CONV_HALO = SUBLANES


def _sigmoid(x):
    return 1.0 / (1.0 + jnp.exp(-x))


def _rms(x, g):
    ms = jnp.mean(x * x, axis=-1, keepdims=True)
    return x * lax.rsqrt(ms + EPS) * g


def _conv3_rows(ext, w, lo, n):
    tot = ext.shape[0]
    up = pltpu.roll(ext, 1, 0)
    dn = pltpu.roll(ext, tot - 1, 0)
    out = up[lo:lo + n] * w[0:1]
    out = out + ext[lo:lo + n] * w[1:2]
    out = out + dn[lo:lo + n] * w[2:3]
    return out


def _rope_kernel(pos_ref, invf_ref, o_ref):
    pos = pos_ref[...].astype(F32)
    ang = pos * invf_ref[...]
    lane = lax.broadcasted_iota(jnp.int32, ang.shape, 1) & (QK_DIM - 1)
    c = jnp.cos(ang)
    s = jnp.sin(ang)
    half = ROT_DIM // 2
    o_ref[:, 0:LANES] = c
    o_ref[:, LANES:2 * LANES] = jnp.where(lane < half, -s, 0.0)
    o_ref[:, 2 * LANES:3 * LANES] = jnp.where((lane >= half) & (lane < ROT_DIM), s, 0.0)


def _rope_tables(positions):
    t = positions.size
    tm = min(t, 2048)
    half = ROT_DIM // 2
    inv_freq = ROPE_THETA ** (-jnp.arange(0, ROT_DIM, 2, dtype=F32) / ROT_DIM)
    lane = jnp.arange(LANES) % QK_DIM
    invf = jnp.where(lane < ROT_DIM, inv_freq[lane % half], 0.0).astype(F32)[None, :]
    return pl.pallas_call(
        _rope_kernel,
        grid=(t // tm,),
        in_specs=[pl.BlockSpec((tm, 1), lambda i: (i, 0)),
                  pl.BlockSpec((1, LANES), lambda i: (0, 0))],
        out_specs=pl.BlockSpec((tm, 3 * LANES), lambda i: (i, 0)),
        out_shape=jax.ShapeDtypeStruct((t, 3 * LANES), F32),
        name="rope_tables",
    )(positions.reshape(t, 1), invf)


def _inproj_kernel(x_ref, g_ref, w_ref, rope_ref, q_ref, k_ref, v_ref, r_ref, h_scr):
    h_scr[...] = _rms(x_ref[...], g_ref[...]).astype(BF16)
    c = rope_ref[:, 0:LANES]
    s1 = rope_ref[:, LANES:2 * LANES]
    s2 = rope_ref[:, 2 * LANES:3 * LANES]
    half = ROT_DIM // 2

    def rot(t):
        return t * c + pltpu.roll(t, LANES - half, 1) * s1 + pltpu.roll(t, half, 1) * s2

    cw = 2 * LANES
    for ci in range(2 * ATTN_W // cw):
        r = jnp.dot(h_scr[...], w_ref[:, ci * cw:(ci + 1) * cw], preferred_element_type=F32)
        for hf in range(2):
            t = rot(r[:, hf * LANES:(hf + 1) * LANES])
            col = ci * cw + hf * LANES
            if col < ATTN_W:
                q_ref[:, col:col + LANES] = (t * Q_SCALE).astype(BF16)
            else:
                k_ref[:, col - ATTN_W:col - ATTN_W + LANES] = t.astype(BF16)
    for ci in range(ATTN_W // cw):
        c0 = 2 * ATTN_W + ci * cw
        r = jnp.dot(h_scr[...], w_ref[:, c0:c0 + cw], preferred_element_type=F32)
        v_ref[ci * cw:(ci + 1) * cw, :] = r.T.astype(BF16)
    c0 = 0
    while c0 < REST_W:
        w = min(cw, REST_W - c0)
        r_ref[:, c0:c0 + w] = jnp.dot(h_scr[...], w_ref[:, QKV_COLS + c0:QKV_COLS + c0 + w],
                                      preferred_element_type=F32)
        c0 += w


def _in_proj(x, g, w_in, rope, layer, tm):
    t = x.shape[0]
    return pl.pallas_call(
        _inproj_kernel,
        grid=(t // tm,),
        in_specs=[pl.BlockSpec((tm, D_MODEL), lambda i: (i, 0)),
                  pl.BlockSpec((None, 1, D_MODEL), lambda i: (layer, 0, 0)),
                  pl.BlockSpec((None, D_MODEL, IN_PAD), lambda i: (layer, 0, 0)),
                  pl.BlockSpec((tm, 3 * LANES), lambda i: (i, 0))],
        out_specs=[pl.BlockSpec((tm, ATTN_W), lambda i: (i, 0)),
                   pl.BlockSpec((tm, ATTN_W), lambda i: (i, 0)),
                   pl.BlockSpec((None, ATTN_W, tm), lambda i: (i, 0, 0)),
                   pl.BlockSpec((tm, REST_W), lambda i: (i, 0))],
        out_shape=[jax.ShapeDtypeStruct((t, ATTN_W), BF16),
                   jax.ShapeDtypeStruct((t, ATTN_W), BF16),
                   jax.ShapeDtypeStruct((t // tm, ATTN_W, tm), BF16),
                   jax.ShapeDtypeStruct((t, REST_W), F32)],
        scratch_shapes=[pltpu.VMEM((tm, D_MODEL), BF16)],
        compiler_params=pltpu.CompilerParams(dimension_semantics=("arbitrary",),
                                             vmem_limit_bytes=VMEM_LIMIT),
        name="in_proj",
    )(x, g, w_in, rope)


def _attn_kernel(lq1_ref, lk1_ref, lq2_ref, lk2_ref, sg_ref, q_ref, k_ref, vt_ref, o_ref,
                 qp_scr, acc_scr, *, lam_init, tq):
    seq = k_ref.shape[0]
    nk, _, tk = vt_ref.shape
    lam = (jnp.exp(jnp.sum(lq1_ref[...] * lk1_ref[...], axis=-1, keepdims=True))
           - jnp.exp(jnp.sum(lq2_ref[...] * lk2_ref[...], axis=-1, keepdims=True)) + lam_init)
    lane = lax.broadcasted_iota(jnp.int32, (tq, V_DIM), 1)
    nt = (((1,), (1,)), ((), ()))

    def q_tile(qi, carry):
        q0 = pl.multiple_of(qi * tq, tq)
        q = q_ref[pl.ds(q0, tq), :]
        zero = jnp.zeros_like(q)
        qp_scr[0:tq, :] = jnp.where(lane < QK_DIM, q, zero)
        qp_scr[tq:2 * tq, :] = jnp.where(lane >= QK_DIM, q, zero)
        acc_scr[...] = jnp.zeros(acc_scr.shape, F32)

        def scores(j):
            kb = k_ref[j * tk:(j + 1) * tk, :]
            sts = [lax.dot_general(kb, qp_scr[c * tq:(c + 1) * tq, :], nt, preferred_element_type=F32) for c in range(2)]
            return sts, [jnp.max(st, axis=0, keepdims=True) for st in sts]

        m_run = [jnp.full((1, tq), -jnp.inf, F32) for _ in range(2)]
        ones = jnp.ones((BF16_SUBLANES, tk), BF16)
        sts, cmax = scores(0)
        for j in range(nk):
            nxt = scores(j + 1) if j + 1 < nk else None
            vt1 = jnp.concatenate([vt_ref[j], ones], axis=0)
            for c in range(2):
                cols = slice(c * tq, (c + 1) * tq)
                m_new = jnp.maximum(m_run[c], cmax[c])
                alpha = jnp.exp2(m_run[c] - m_new)
                p = jnp.exp2(sts[c] - m_new).astype(BF16)
                acc_scr[:, cols] = alpha * acc_scr[:, cols] + jnp.dot(vt1, p, preferred_element_type=F32)
                m_run[c] = m_new
            if nxt is not None:
                sts, cmax = nxt
        o = acc_scr[0:V_DIM, :] / acc_scr[V_DIM:V_DIM + 1, :]
        o = o[:, 0:tq] - lam * o[:, tq:2 * tq]
        ms = jnp.mean(o * o, axis=0, keepdims=True)
        y = o * lax.rsqrt(ms + EPS) * sg_ref[...] * (1.0 - lam_init)
        o_ref[pl.ds(q0, tq), :] = y.T.astype(BF16)
        return carry

    lax.fori_loop(0, seq // tq, q_tile, 0)


def _attention(q, k, vt, lq1, lk1, lq2, lk2, subg_col, layer, lam_init, batch, seq, tq):
    t = q.shape[0]
    tk = vt.shape[2]
    small = lambda n: pl.BlockSpec((None, 1, n), lambda bi, hi: (layer, 0, 0))
    head = pl.BlockSpec((seq, V_DIM), lambda bi, hi: (bi, hi))
    return pl.pallas_call(
        functools.partial(_attn_kernel, lam_init=lam_init, tq=tq),
        grid=(batch, N_ATTN_HEADS),
        in_specs=[small(QK_DIM), small(QK_DIM), small(QK_DIM), small(QK_DIM),
                  pl.BlockSpec((None, V_DIM, 1), lambda bi, hi: (layer, 0, 0)),
                  head, head,
                  pl.BlockSpec((seq // tk, V_DIM, tk), lambda bi, hi: (bi, hi, 0))],
        out_specs=head,
        out_shape=jax.ShapeDtypeStruct((t, ATTN_W), BF16),
        scratch_shapes=[pltpu.VMEM((2 * tq, V_DIM), BF16),
                        pltpu.VMEM((V_DIM + BF16_SUBLANES, 2 * tq), F32)],
        compiler_params=pltpu.CompilerParams(dimension_semantics=("arbitrary", "arbitrary"),
                                             vmem_limit_bytes=VMEM_LIMIT),
        name="diff_attn",
    )(lq1, lk1, lq2, lk2, subg_col, q, k, vt)


N_SERIES = 2 * SSD_HEADS
N_PICK = 2 * SSD_GROUPS
SEL_COLS = (N_SERIES + N_PICK) * LANES


def _select_matrix():
    lane = jnp.arange(SEL_COLS)
    blk, within = lane // LANES, lane % LANES
    d, g = (blk - N_SERIES) // SSD_GROUPS, (blk - N_SERIES) % SSD_GROUPS
    src = jnp.where(blk < N_SERIES, blk, SSD_HEADS * d + 2 * g + (within >= LANES // 2))
    row = jnp.arange(LANES)[:, None]
    return ((row < 3 * N_SERIES) & (row % N_SERIES == src[None, :])).astype(BF16)


def _split3(x):
    hi = x.astype(BF16)
    r1 = x - hi.astype(F32)
    mid = r1.astype(BF16)
    lo = (r1 - mid.astype(F32)).astype(BF16)
    return hi, mid, lo


def _ssd_kernel(dtb_ref, alog_ref, dexp_ref, ng_ref, scw_ref, cw_ref, cb_ref, sel_ref, main_ref, prev_ref, next_ref,
                o_ref, xbc_scr, bt_scr, dt_scr, hb_scr, h_scr, *, rows, nblk):
    sweep = pl.program_id(1)
    i = pl.program_id(2)
    blk = jnp.where(sweep == 0, nblk - 1 - i, i)
    has_prev = blk > 0
    has_next = blk < nblk - 1
    L = SSD_CHUNK
    nsub = rows // L
    half = LANES // 2

    @pl.when(i == 0)
    def _():
        h_scr[...] = jnp.zeros(h_scr.shape, F32)

    def ext_cols(c0, c1):
        pv = jnp.where(has_prev, prev_ref[:, c0:c1], 0.0)
        nx = jnp.where(has_next, next_ref[:, c0:c1], 0.0)
        return jnp.concatenate([pv, main_ref[:, c0:c1], nx], axis=0)

    rowi = lax.broadcasted_iota(jnp.int32, (L, L), 0)
    coli = lax.broadcasted_iota(jnp.int32, (L, L), 1)
    upper = (rowi <= coli).astype(BF16)
    lower = (rowi >= coli).astype(BF16)
    lo = lax.broadcasted_iota(jnp.int32, (L, LANES), 1) < half
    fwd_rows = lax.broadcasted_iota(jnp.int32, (N_SERIES, 1), 0) < SSD_HEADS
    neg_a = -jnp.exp(alog_ref[...])

    def series(dtt):
        hi, mid, low = _split3(dtt * neg_a)
        pre = sum(jnp.dot(t, upper, preferred_element_type=F32) for t in (hi, mid, low))
        suf = sum(jnp.dot(t, lower, preferred_element_type=F32) for t in (hi, mid, low))
        cst = jnp.where(fwd_rows, pre, suf)
        tot = jnp.where(fwd_rows, cst[:, L - 1:L], cst[:, 0:1])
        return cst, tot

    def chunk_state(bt, xs_g, w, j0):
        return (jnp.dot((bt * w[j0:j0 + 1, :]).astype(BF16), jnp.where(lo, xs_g, 0.0).astype(BF16),
                        preferred_element_type=F32)
                + jnp.dot((bt * w[j0 + 1:j0 + 2, :]).astype(BF16), jnp.where(lo, 0.0, xs_g).astype(BF16),
                          preferred_element_type=F32))

    def pick2(col, j0):
        return jnp.where(lo[0:1], col[j0:j0 + 1, :], col[j0 + 1:j0 + 2, :])

    def chunk_series(dtt_all):
        out = []
        for si in range(nsub):
            dtt = dtt_all[:, si * L:(si + 1) * L]
            cst, tot = series(dtt)
            out.append((dtt, cst, dtt * jnp.exp(tot - cst), jnp.exp(tot)))
        return out

    @pl.when(sweep == 0)
    def _():
        xbc = _conv3_rows(ext_cols(R_XBC, R_XBC + SSD_XBC), cw_ref[...], SUBLANES, rows) + cb_ref[...]
        xbc = xbc * _sigmoid(xbc)
        xbc_scr[blk] = xbc
        xdt = main_ref[:, R_DT:R_DT + LANES].T[0:N_SERIES, :] + dtb_ref[...]
        dtt_all = jnp.maximum(xdt, 0.0) + jnp.log1p(jnp.exp(-jnp.abs(xdt)))
        dt_scr[blk] = dtt_all
        ser = chunk_series(dtt_all)
        sts = {}
        for si in range(nsub):
            r0 = si * L
            for g in range(SSD_GROUPS):
                bt = xbc[r0:r0 + L, SSD_W + g * SSD_STATE:SSD_W + (g + 1) * SSD_STATE].T
                bt_scr[blk, g, :, r0:r0 + L] = bt
                sts[si, g] = chunk_state(bt, xbc[r0:r0 + L, g * LANES:(g + 1) * LANES], ser[si][2],
                                         SSD_HEADS + 2 * g)
        for g in range(SSD_GROUPS):
            hg = h_scr[g]
            for si in reversed(range(nsub)):
                hb_scr[blk * nsub + si, g] = hg
                hg = hg * pick2(ser[si][3], SSD_HEADS + 2 * g) + sts[si, g]
            h_scr[g] = hg

    @pl.when(sweep == 1)
    def _():
        u = ext_cols(R_SCC, R_SCC + SC_W) * ext_cols(R_SCH, R_SCH + SC_W)
        yc = main_ref[:, R_SCB:R_SCB + SC_W] * _conv3_rows(u, scw_ref[...], SUBLANES, rows)
        o_ref[:, 0:SC_W] = yc.astype(BF16)
        zpad = jnp.zeros((LANES - 3 * N_SERIES, L), F32)
        masks = (coli <= rowi, coli >= rowi)
        ser = chunk_series(dt_scr[blk])
        bcs = []
        for si in range(nsub):
            hi, mid, low = _split3(ser[si][1])
            cs_col = jnp.concatenate([hi.astype(F32), mid.astype(F32), low.astype(F32), zpad], axis=0).T
            bcs.append(jnp.dot(cs_col.astype(BF16), sel_ref[...], preferred_element_type=F32))
        pairs = [(si, g) for si in range(nsub) for g in range(SSD_GROUPS)]
        xs, cbf, bts, gms, sts, ys = {}, {}, {}, {}, {}, {}
        for si, g in pairs:
            r0 = si * L
            xs[si, g] = xbc_scr[blk, r0:r0 + L, g * LANES:(g + 1) * LANES]
            cbf[si, g] = xbc_scr[blk, r0:r0 + L, SSD_W + (SSD_GROUPS + g) * SSD_STATE:
                                 SSD_W + (SSD_GROUPS + g + 1) * SSD_STATE].astype(BF16)
            bts[si, g] = bt_scr[blk, g, :, r0:r0 + L]
            gms[si, g] = jnp.dot(cbf[si, g], bts[si, g].astype(BF16), preferred_element_type=F32)
        for si, g in pairs:
            sts[si, g] = chunk_state(bts[si, g], xs[si, g], ser[si][2], 2 * g)
        for si, g in pairs:
            dtt, cst = ser[si][0], ser[si][1]
            xs_lo = jnp.where(lo, xs[si, g], 0.0).astype(BF16)
            xs_hi = jnp.where(lo, 0.0, xs[si, g]).astype(BF16)
            ms = []
            for h in range(2):
                tot_decay = None
                for d in range(2):
                    j = SSD_HEADS * d + 2 * g + h
                    seg = bcs[si][:, j * LANES:(j + 1) * LANES] - cst[j:j + 1, :]
                    term = jnp.where(masks[d], jnp.exp(seg), 0.0) * dtt[j:j + 1, :]
                    tot_decay = term if tot_decay is None else tot_decay + term
                ms.append((gms[si, g] * tot_decay).astype(BF16))
            ys[si, g] = jnp.dot(jnp.concatenate(ms, axis=1), jnp.concatenate([xs_lo, xs_hi], axis=0),
                                preferred_element_type=F32)
        for g in range(SSD_GROUPS):
            hg = h_scr[g]
            for si in range(nsub):
                r0 = si * L
                hb = hb_scr[blk * nsub + si, g]
                ch = jnp.dot(cbf[si, g], jnp.concatenate([hg.astype(BF16), hb.astype(BF16)], axis=1),
                             preferred_element_type=F32)
                hg = hg * pick2(ser[si][3], 2 * g) + sts[si, g]
                pf = (N_SERIES + g) * LANES
                pb = (N_SERIES + SSD_GROUPS + g) * LANES
                y = (ys[si, g] + jnp.exp(bcs[si][:, pf:pf + LANES]) * ch[:, 0:LANES]
                     + jnp.exp(bcs[si][:, pb:pb + LANES]) * ch[:, LANES:])
                gl = slice(g * LANES, (g + 1) * LANES)
                y = y + dexp_ref[:, gl] * xs[si, g]
                z = main_ref[r0:r0 + L, R_Z + g * LANES:R_Z + (g + 1) * LANES]
                y = y * (z * _sigmoid(z))
                o_ref[r0:r0 + L, SC_W + g * LANES:SC_W + (g + 1) * LANES] = _rms(y, ng_ref[:, gl]).astype(BF16)
            h_scr[g] = hg


def _ssd_mixers(rest, dtb_col, alog_col, dexp, ng, scw, cw, cb, sel, layer, batch, seq, rows):
    t = rest.shape[0]
    nblk = seq // rows
    r8 = rows // SUBLANES

    def blk_of(s, i):
        return jnp.where(s == 0, nblk - 1 - i, i)

    small = lambda r, n: pl.BlockSpec((None, r, n), lambda bi, s, i: (layer, 0, 0))
    return pl.pallas_call(
        functools.partial(_ssd_kernel, rows=rows, nblk=nblk),
        grid=(batch, 2, nblk),
        in_specs=[small(N_SERIES, 1), small(N_SERIES, 1), small(1, SSD_W), small(1, SSD_W),
                  small(3, SC_W), small(3, SSD_XBC), small(1, SSD_XBC),
                  pl.BlockSpec((LANES, SEL_COLS), lambda bi, s, i: (0, 0)),
                  pl.BlockSpec((rows, REST_W), lambda bi, s, i: (bi * nblk + blk_of(s, i), 0)),
                  pl.BlockSpec((SUBLANES, REST_W),
                               lambda bi, s, i: (jnp.maximum((bi * nblk + blk_of(s, i)) * r8 - 1, 0), 0)),
                  pl.BlockSpec((SUBLANES, REST_W),
                               lambda bi, s, i: (jnp.minimum((bi * nblk + blk_of(s, i) + 1) * r8,
                                                             t // SUBLANES - 1), 0))],
        out_specs=pl.BlockSpec((rows, SC_W + SSD_W), lambda bi, s, i: (bi * nblk + s * i, 0)),
        out_shape=jax.ShapeDtypeStruct((t, SC_W + SSD_W), BF16),
        scratch_shapes=[pltpu.VMEM((nblk, rows, SSD_XBC), F32),
                        pltpu.VMEM((nblk, SSD_GROUPS, SSD_STATE, rows), F32),
                        pltpu.VMEM((nblk, N_SERIES, rows), F32),
                        pltpu.VMEM((seq // SSD_CHUNK, SSD_GROUPS, SSD_STATE, LANES), F32),
                        pltpu.VMEM((SSD_GROUPS, SSD_STATE, LANES), F32)],
        compiler_params=pltpu.CompilerParams(dimension_semantics=("arbitrary", "arbitrary", "arbitrary"),
                                             vmem_limit_bytes=VMEM_LIMIT),
        name="ssd_mixers",
    )(dtb_col, alog_col, dexp, ng, scw, cw, cb, sel, rest, rest, rest)


def _ffn_kernel(x_ref, xp_ref, xn_ref, ya_ref, yap_ref, yan_ref, ym_ref, ymp_ref, ymn_ref,
                wout_ref, g_ref, wup_ref, cwg_ref, cwu_ref, cbg_ref, cbu_ref, wd_ref, fg_ref,
                o_ref, slab_scr, hp_scr, act_scr, ua_scr, ub_scr, *, tm, tiles_per_seq, final_norm):
    ti = pl.program_id(0) % tiles_per_seq
    has_prev = ti > 0
    has_next = ti < tiles_per_seq - 1
    e_rows = tm + 2 * CONV_HALO
    seg = e_rows // SUBLANES
    n_slabs = D_MODEL // LANES

    def ext(prev, main, nxt):
        return jnp.concatenate([prev[...], main[...], nxt[...]], axis=0)

    mix = (jnp.dot(ext(yap_ref, ya_ref, yan_ref), wout_ref[0:ATTN_W, :], preferred_element_type=F32)
           + jnp.dot(ext(ymp_ref, ym_ref, ymn_ref), wout_ref[ATTN_W:2 * ATTN_W, :], preferred_element_type=F32))
    xnew = (ext(xp_ref, x_ref, xn_ref) + mix)[HALO - CONV_HALO:HALO - CONV_HALO + e_rows]
    row = lax.broadcasted_iota(jnp.int32, (e_rows, 1), 0)
    valid = (row >= jnp.where(has_prev, 0, CONV_HALO)) & (row < jnp.where(has_next, e_rows, CONV_HALO + tm))
    h = jnp.where(valid, _rms(xnew, g_ref[...]), 0.0)
    o_ref[...] = xnew[CONV_HALO:CONV_HALO + tm]

    for c in range(n_slabs):
        slab_scr[c] = h[:, c * LANES:(c + 1) * LANES]
    for i in range(0, seg, 2):
        blk = [jnp.concatenate([slab_scr[c, pl.ds(i + d, SUBLANES, stride=seg), :] for c in range(n_slabs)], axis=1)
               for d in range(2)]
        hp_scr[SUBLANES * i:SUBLANES * (i + 2), :] = jnp.concatenate(blk, axis=0).astype(BF16)

    def up_proj(j, u_ref):
        hb = hp_scr[...]
        c0 = j * FF_CHUNK
        u_ref[0] = jnp.dot(hb, wup_ref[:, c0:c0 + FF_CHUNK], preferred_element_type=F32)
        u_ref[1] = jnp.dot(hb, wup_ref[:, D_FF + c0:D_FF + c0 + FF_CHUNK], preferred_element_type=F32)

    def conv_rows(u_ref, idx, w):
        last = e_rows - SUBLANES
        mid = (u_ref[idx, 0:last - SUBLANES, :] * w[0:1] + u_ref[idx, SUBLANES:last, :] * w[1:2]
               + u_ref[idx, 2 * SUBLANES:e_rows, :] * w[2:3])
        first = (pltpu.roll(u_ref[idx, last:e_rows, :], 1, 0) * w[0:1] + u_ref[idx, 0:SUBLANES, :] * w[1:2]
                 + u_ref[idx, SUBLANES:2 * SUBLANES, :] * w[2:3])
        end = (u_ref[idx, last - SUBLANES:last, :] * w[0:1] + u_ref[idx, last:e_rows, :] * w[1:2]
               + pltpu.roll(u_ref[idx, 0:SUBLANES, :], SUBLANES - 1, 0) * w[2:3])
        return jnp.concatenate([first, mid, end], axis=0)

    def gate_down(j, u_ref):
        cg = conv_rows(u_ref, 0, cwg_ref[j]) + cbg_ref[j]
        cu = conv_rows(u_ref, 1, cwu_ref[j]) + cbu_ref[j]
        act_scr[:, j * FF_CHUNK:(j + 1) * FF_CHUNK] = ((cg * _sigmoid(cg)) * cu).astype(BF16)

    bufs = (ua_scr, ub_scr)
    up_proj(0, bufs[0])
    for j in range(N_FF_CHUNKS):
        if j + 1 < N_FF_CHUNKS:
            up_proj(j + 1, bufs[(j + 1) % 2])
        gate_down(j, bufs[j % 2])
    down = jnp.dot(act_scr[...], wd_ref[...], preferred_element_type=F32)
    for i in range(seg):
        for c in range(n_slabs):
            slab_scr[c, pl.ds(i, SUBLANES, stride=seg), :] = down[SUBLANES * i:SUBLANES * (i + 1),
                                                                c * LANES:(c + 1) * LANES]
    for c in range(n_slabs):
        o_ref[:, c * LANES:(c + 1) * LANES] += slab_scr[c, CONV_HALO:CONV_HALO + tm, :]
    if final_norm:
        o_ref[...] = _rms(o_ref[...], fg_ref[...])


def _ffn(x, ya, ym, w_out, g, w_up, cwg, cwu, cbg, cbu, wd, fg, layer, seq, tm, final_norm):
    t = x.shape[0]
    tiles_per_seq = seq // tm
    hb = tm // HALO
    nh = t // HALO

    def main(w):
        return pl.BlockSpec((tm, w), lambda i: (i, 0))

    def prev(w):
        return pl.BlockSpec((HALO, w), lambda i: (jnp.maximum(i * hb - 1, 0), 0))

    def nxt(w):
        return pl.BlockSpec((HALO, w), lambda i: (jnp.minimum((i + 1) * hb, nh - 1), 0))

    def resident(shape):
        nd = len(shape)
        return pl.BlockSpec((None,) + shape, lambda i: (layer,) + (0,) * nd,
                            pipeline_mode=pl.Buffered(1))

    e_rows = tm + 2 * CONV_HALO
    assert e_rows % SUBLANES == 0 and (e_rows // SUBLANES) % 8 != 0
    return pl.pallas_call(
        functools.partial(_ffn_kernel, tm=tm, tiles_per_seq=tiles_per_seq, final_norm=final_norm),
        grid=(t // tm,),
        in_specs=[main(D_MODEL), prev(D_MODEL), nxt(D_MODEL),
                  main(ATTN_W), prev(ATTN_W), nxt(ATTN_W),
                  main(SC_W + SSD_W), prev(SC_W + SSD_W), nxt(SC_W + SSD_W),
                  resident((D_MODEL, D_MODEL)),
                  resident((1, D_MODEL)),
                  resident((D_MODEL, 2 * D_FF)),
                  resident((N_FF_CHUNKS, 3, FF_CHUNK)),
                  resident((N_FF_CHUNKS, 3, FF_CHUNK)),
                  resident((N_FF_CHUNKS, 1, FF_CHUNK)),
                  resident((N_FF_CHUNKS, 1, FF_CHUNK)),
                  resident((D_FF, D_MODEL)),
                  pl.BlockSpec((1, D_MODEL), lambda i: (0, 0))],
        out_specs=main(D_MODEL),
        out_shape=jax.ShapeDtypeStruct((t, D_MODEL), F32),
        scratch_shapes=[pltpu.VMEM((D_MODEL // LANES, e_rows, LANES), F32),
                        pltpu.VMEM((e_rows, D_MODEL), BF16),
                        pltpu.VMEM((e_rows, D_FF), BF16),
                        pltpu.VMEM((2, e_rows, FF_CHUNK), F32),
                        pltpu.VMEM((2, e_rows, FF_CHUNK), F32)],
        compiler_params=pltpu.CompilerParams(dimension_semantics=("arbitrary",),
                                             vmem_limit_bytes=VMEM_LIMIT),
        name="ffn",
    )(x, x, x, ya, ya, ya, ym, ym, ym, w_out, g, w_up, cwg, cwu, cbg, cbu, wd, fg)


def kernel(x, positions, norm_mix_g, w_in, lam_q1, lam_k1, lam_q2, lam_k2, subln_g, sc_conv_w, ssd_conv_w,
           ssd_conv_b, ssd_dt_bias, ssd_a_log, ssd_d, ssd_norm_g, w_out, norm_ffn_g, w_up, ffn_conv_w,
           ffn_conv_b, w_down, final_norm_g):
    batch, seq, _ = x.shape
    depth = w_in.shape[0]
    t = batch * seq
    tm = min(512, seq)
    tq = min(1024, seq)
    ssd_rows = min(512, seq)

    w_in_b = jnp.pad(w_in, ((0, 0), (0, 0), (0, IN_PAD - IN_COLS))).astype(BF16)
    w_out_b = w_out.astype(BF16)
    w_up_b = w_up.astype(BF16)
    wd = w_down.astype(BF16)
    fcw = ffn_conv_w.reshape(depth, 3, 2, N_FF_CHUNKS, FF_CHUNK)
    cwg = fcw[:, :, 0].transpose(0, 2, 1, 3)
    cwu = fcw[:, :, 1].transpose(0, 2, 1, 3)
    fcb = ffn_conv_b.reshape(depth, 2, N_FF_CHUNKS, 1, FF_CHUNK)
    cbg, cbu = fcb[:, 0], fcb[:, 1]
    row3 = lambda a: a.reshape(depth, 1, -1)
    dtb = ssd_dt_bias.reshape(depth, N_SERIES, 1)
    alog = ssd_a_log.reshape(depth, N_SERIES, 1)
    sel = _select_matrix()
    dexp = jnp.repeat(ssd_d, SSD_W // SSD_HEADS, axis=-1).reshape(depth, 1, SSD_W)
    fg = final_norm_g.reshape(1, D_MODEL)

    rope = _rope_tables(positions)
    xf = x.reshape(t, D_MODEL)
    for l in range(depth):
        lam_init = 0.8 - 0.6 * math.exp(-0.3 * l)
        q, k, vt, rest = _in_proj(xf, row3(norm_mix_g), w_in_b, rope, l, tm)
        ya = _attention(q, k, vt, row3(lam_q1), row3(lam_k1), row3(lam_q2), row3(lam_k2),
                        subln_g.reshape(depth, V_DIM, 1), l, lam_init, batch, seq, tq)
        ym = _ssd_mixers(rest, dtb, alog, dexp, row3(ssd_norm_g), sc_conv_w, ssd_conv_w, row3(ssd_conv_b), sel,
                         l, batch, seq, ssd_rows)
        xf = _ffn(xf, ya, ym, w_out_b, row3(norm_ffn_g), w_up_b, cwg, cwu, cbg, cbu, wd, fg,
                  l, seq, tm, l == depth - 1)
    return xf.reshape(batch, seq, D_MODEL)
```

```python
import functools
import math

import jax
import jax.numpy as jnp
from jax import lax
from jax.experimental import pallas as pl
from jax.experimental.pallas import tpu as pltpu

F32 = jnp.float32
BF16 = jnp.bfloat16

D_MODEL = 1024
EPS = 1e-5
N_ATTN_HEADS = 4
QK_DIM = 64
V_DIM = 128
ATTN_W = N_ATTN_HEADS * V_DIM
ROPE_THETA = 500000.0
ROT_DIM = QK_DIM // 4
Q_SCALE = QK_DIM ** -0.5 * math.log2(math.e)
SC_W = 256
SSD_W = 256
SSD_STATE = 128
SSD_GROUPS = 2
SSD_HEADS = 4
SSD_CHUNK = 128
SSD_XBC = SSD_W + 2 * SSD_GROUPS * SSD_STATE
D_FF = 2816
IN_COLS = 3336

QKV_COLS = 3 * ATTN_W
REST_REAL = IN_COLS - QKV_COLS
REST_W = 1920
IN_PAD = QKV_COLS + REST_W
R_SCB, R_SCC, R_SCH, R_Z, R_XBC, R_DT = 0, 256, 512, 768, 1024, 1792

LANES = 128
SUBLANES = 8
BF16_SUBLANES = 16
VMEM_LIMIT = 56 * 1024 * 1024

FF_CHUNK = 256
N_FF_CHUNKS = D_FF // FF_CHUNK
HALO = BF16_SUBLANES
CONV_HALO = SUBLANES


def _sigmoid(x):
    return 1.0 / (1.0 + jnp.exp(-x))


def _rms(x, g):
    ms = jnp.mean(x * x, axis=-1, keepdims=True)
    return x * lax.rsqrt(ms + EPS) * g


def _rope_kernel(pos_ref, invf_ref, o_ref):
    pos = pos_ref[...].astype(F32)
    ang = pos * invf_ref[...]
    lane = lax.broadcasted_iota(jnp.int32, ang.shape, 1) & (QK_DIM - 1)
    c = jnp.cos(ang)
    s = jnp.sin(ang)
    half = ROT_DIM // 2
    o_ref[:, 0:LANES] = c
    o_ref[:, LANES:2 * LANES] = jnp.where(lane < half, -s, 0.0)
    o_ref[:, 2 * LANES:3 * LANES] = jnp.where((lane >= half) & (lane < ROT_DIM), s, 0.0)


def _rope_tables(positions):
    t = positions.size
    tm = min(t, 2048)
    half = ROT_DIM // 2
    inv_freq = ROPE_THETA ** (-jnp.arange(0, ROT_DIM, 2, dtype=F32) / ROT_DIM)
    lane = jnp.arange(LANES) % QK_DIM
    invf = jnp.where(lane < ROT_DIM, inv_freq[lane % half], 0.0).astype(F32)[None, :]
    return pl.pallas_call(
        _rope_kernel,
        grid=(t // tm,),
        in_specs=[pl.BlockSpec((tm, 1), lambda i: (i, 0)),
                  pl.BlockSpec((1, LANES), lambda i: (0, 0))],
        out_specs=pl.BlockSpec((tm, 3 * LANES), lambda i: (i, 0)),
        out_shape=jax.ShapeDtypeStruct((t, 3 * LANES), F32),
        name="rope_tables",
    )(positions.reshape(t, 1), invf)


def _inproj_kernel(x_ref, xp_ref, xn_ref, g_ref, w_ref, rope_ref, scw_ref, cw_ref, cb_ref,
                   q_ref, k_ref, v_ref, z_ref, xbc_ref, dt_ref, yc_ref, h_scr, pr_scr, *, tm, tiles_per_seq):
    ti = pl.program_id(0) % tiles_per_seq
    has_prev = ti > 0
    has_next = ti < tiles_per_seq - 1
    e_rows = tm + 2 * SUBLANES
    xe = jnp.concatenate([xp_ref[...], x_ref[...], xn_ref[...]], axis=0)
    row = lax.broadcasted_iota(jnp.int32, (e_rows, 1), 0)
    valid = (row >= jnp.where(has_prev, 0, SUBLANES)) & (row < jnp.where(has_next, e_rows, SUBLANES + tm))
    h_scr[...] = jnp.where(valid, _rms(xe, g_ref[...]), 0.0).astype(BF16)
    c = rope_ref[:, 0:LANES]
    s1 = rope_ref[:, LANES:2 * LANES]
    s2 = rope_ref[:, 2 * LANES:3 * LANES]
    half = ROT_DIM // 2

    def rot(t):
        return t * c + pltpu.roll(t, LANES - half, 1) * s1 + pltpu.roll(t, half, 1) * s2

    def proj(c0, w, halo=False):
        lhs = h_scr[...] if halo else h_scr[SUBLANES:SUBLANES + tm, :]
        return jnp.dot(lhs, w_ref[:, c0:c0 + w], preferred_element_type=F32)

    def conv_win(slot, w):
        return (pr_scr[slot, SUBLANES - 1:SUBLANES - 1 + tm, :] * w[0:1] + pr_scr[slot, SUBLANES:SUBLANES + tm, :] * w[1:2]
                + pr_scr[slot, SUBLANES + 1:SUBLANES + 1 + tm, :] * w[2:3])

    cw = 2 * LANES
    pr_scr[0] = proj(QKV_COLS + R_SCC, SC_W, True) * proj(QKV_COLS + R_SCH, SC_W, True)
    yc_ref[...] = (proj(QKV_COLS + R_SCB, SC_W) * conv_win(0, scw_ref[...])).astype(BF16)
    z_ref[...] = proj(QKV_COLS + R_Z, SSD_W)
    for ci in range(SSD_XBC // cw):
        cols = slice(ci * cw, (ci + 1) * cw)
        pr_scr[(ci + 1) % 2] = proj(QKV_COLS + R_XBC + ci * cw, cw, True)
        xc = conv_win((ci + 1) % 2, cw_ref[:, cols]) + cb_ref[:, cols]
        xbc_ref[:, cols] = xc * _sigmoid(xc)
    for ci in range(2 * ATTN_W // cw):
        r = proj(ci * cw, cw)
        for hf in range(2):
            t = rot(r[:, hf * LANES:(hf + 1) * LANES])
            col = ci * cw + hf * LANES
            if col < ATTN_W:
                q_ref[:, col:col + LANES] = (t * Q_SCALE).astype(BF16)
            else:
                k_ref[:, col - ATTN_W:col - ATTN_W + LANES] = t.astype(BF16)
    for ci in range(ATTN_W // cw):
        r = proj(2 * ATTN_W + ci * cw, cw)
        v_ref[ci * cw:(ci + 1) * cw, :] = r.T.astype(BF16)
    dt_ref[...] = proj(QKV_COLS + R_DT, LANES)


def _in_proj(x, g, w_in, rope, scw, cw, cb, layer, seq, tm):
    t = x.shape[0]
    tiles_per_seq = seq // tm
    hb = tm // SUBLANES
    nh = t // SUBLANES
    rows = lambda w: pl.BlockSpec((tm, w), lambda i: (i, 0))
    small = lambda r, n: pl.BlockSpec((None, r, n), lambda i: (layer, 0, 0))
    return pl.pallas_call(
        functools.partial(_inproj_kernel, tm=tm, tiles_per_seq=tiles_per_seq),
        grid=(t // tm,),
        in_specs=[rows(D_MODEL),
                  pl.BlockSpec((SUBLANES, D_MODEL), lambda i: (jnp.maximum(i * hb - 1, 0), 0)),
                  pl.BlockSpec((SUBLANES, D_MODEL), lambda i: (jnp.minimum((i + 1) * hb, nh - 1), 0)),
                  small(1, D_MODEL),
                  pl.BlockSpec((None, D_MODEL, IN_PAD), lambda i: (layer, 0, 0)),
                  rows(3 * LANES),
                  small(3, SC_W), small(3, SSD_XBC), small(1, SSD_XBC)],
        out_specs=[rows(ATTN_W), rows(ATTN_W),
                   pl.BlockSpec((None, ATTN_W, tm), lambda i: (i, 0, 0)),
                   rows(SSD_W), rows(SSD_XBC), rows(LANES), rows(SC_W)],
        out_shape=[jax.ShapeDtypeStruct((t, ATTN_W), BF16),
                   jax.ShapeDtypeStruct((t, ATTN_W), BF16),
                   jax.ShapeDtypeStruct((t // tm, ATTN_W, tm), BF16),
                   jax.ShapeDtypeStruct((t, SSD_W), F32),
                   jax.ShapeDtypeStruct((t, SSD_XBC), F32),
                   jax.ShapeDtypeStruct((t, LANES), F32),
                   jax.ShapeDtypeStruct((t, SC_W), BF16)],
        scratch_shapes=[pltpu.VMEM((tm + 2 * SUBLANES, D_MODEL), BF16),
                        pltpu.VMEM((2, tm + 2 * SUBLANES, 2 * LANES), F32)],
        compiler_params=pltpu.CompilerParams(dimension_semantics=("arbitrary",),
                                             vmem_limit_bytes=VMEM_LIMIT),
        name="in_proj",
    )(x, x, x, g, w_in, rope, scw, cw, cb)


def _attn_kernel(lq1_ref, lk1_ref, lq2_ref, lk2_ref, sg_ref, q_ref, k_ref, vt_ref, o_ref,
                 qp_scr, acc_scr, *, lam_init, tq):
    seq = k_ref.shape[0]
    nk, _, tk = vt_ref.shape
    lam = (jnp.exp(jnp.sum(lq1_ref[...] * lk1_ref[...], axis=-1, keepdims=True))
           - jnp.exp(jnp.sum(lq2_ref[...] * lk2_ref[...], axis=-1, keepdims=True)) + lam_init)
    lane = lax.broadcasted_iota(jnp.int32, (tq, V_DIM), 1)
    nt = (((1,), (1,)), ((), ()))

    def q_tile(qi, carry):
        q0 = pl.multiple_of(qi * tq, tq)
        q = q_ref[pl.ds(q0, tq), :]
        zero = jnp.zeros_like(q)
        qp_scr[0:tq, :] = jnp.where(lane < QK_DIM, q, zero)
        qp_scr[tq:2 * tq, :] = jnp.where(lane >= QK_DIM, q, zero)
        acc_scr[...] = jnp.zeros(acc_scr.shape, F32)

        def scores(j):
            kb = k_ref[j * tk:(j + 1) * tk, :]
            sts = [lax.dot_general(kb, qp_scr[c * tq:(c + 1) * tq, :], nt, preferred_element_type=F32) for c in range(2)]
            return sts, [jnp.max(st, axis=0, keepdims=True) for st in sts]

        m_run = [jnp.full((1, tq), -jnp.inf, F32) for _ in range(2)]
        ones = jnp.ones((BF16_SUBLANES, tk), BF16)
        sts, cmax = scores(0)
        for j in range(nk):
            nxt = scores(j + 1) if j + 1 < nk else None
            vt1 = jnp.concatenate([vt_ref[j], ones], axis=0)
            for c in range(2):
                cols = slice(c * tq, (c + 1) * tq)
                m_new = jnp.maximum(m_run[c], cmax[c])
                alpha = jnp.exp2(m_run[c] - m_new)
                p = jnp.exp2(sts[c] - m_new).astype(BF16)
                acc_scr[:, cols] = alpha * acc_scr[:, cols] + jnp.dot(vt1, p, preferred_element_type=F32)
                m_run[c] = m_new
            if nxt is not None:
                sts, cmax = nxt
        o = acc_scr[0:V_DIM, :] / acc_scr[V_DIM:V_DIM + 1, :]
        o = o[:, 0:tq] - lam * o[:, tq:2 * tq]
        ms = jnp.mean(o * o, axis=0, keepdims=True)
        y = o * lax.rsqrt(ms + EPS) * sg_ref[...] * (1.0 - lam_init)
        o_ref[pl.ds(q0, tq), :] = y.T.astype(BF16)
        return carry

    lax.fori_loop(0, seq // tq, q_tile, 0)


def _attention(q, k, vt, lq1, lk1, lq2, lk2, subg_col, layer, lam_init, batch, seq, tq):
    t = q.shape[0]
    tk = vt.shape[2]
    small = lambda n: pl.BlockSpec((None, 1, n), lambda bi, hi: (layer, 0, 0))
    head = pl.BlockSpec((seq, V_DIM), lambda bi, hi: (bi, hi))
    return pl.pallas_call(
        functools.partial(_attn_kernel, lam_init=lam_init, tq=tq),
        grid=(batch, N_ATTN_HEADS),
        in_specs=[small(QK_DIM), small(QK_DIM), small(QK_DIM), small(QK_DIM),
                  pl.BlockSpec((None, V_DIM, 1), lambda bi, hi: (layer, 0, 0)),
                  head, head,
                  pl.BlockSpec((seq // tk, V_DIM, tk), lambda bi, hi: (bi, hi, 0))],
        out_specs=head,
        out_shape=jax.ShapeDtypeStruct((t, ATTN_W), BF16),
        scratch_shapes=[pltpu.VMEM((2 * tq, V_DIM), BF16),
                        pltpu.VMEM((V_DIM + BF16_SUBLANES, 2 * tq), F32)],
        compiler_params=pltpu.CompilerParams(dimension_semantics=("arbitrary", "arbitrary"),
                                             vmem_limit_bytes=VMEM_LIMIT),
        name="diff_attn",
    )(lq1, lk1, lq2, lk2, subg_col, q, k, vt)


N_SERIES = 2 * SSD_HEADS
N_PICK = 2 * SSD_GROUPS
SEL_COLS = (N_SERIES + N_PICK) * LANES


def _select_matrix():
    lane = jnp.arange(SEL_COLS)
    blk, within = lane // LANES, lane % LANES
    d, g = (blk - N_SERIES) // SSD_GROUPS, (blk - N_SERIES) % SSD_GROUPS
    src = jnp.where(blk < N_SERIES, blk, SSD_HEADS * d + 2 * g + (within >= LANES // 2))
    row = jnp.arange(LANES)[:, None]
    return ((row < 3 * N_SERIES) & (row % N_SERIES == src[None, :])).astype(BF16)


def _split3(x):
    hi = x.astype(BF16)
    r1 = x - hi.astype(F32)
    mid = r1.astype(BF16)
    lo = (r1 - mid.astype(F32)).astype(BF16)
    return hi, mid, lo


def _ssd_kernel(dtb_ref, alog_ref, dexp_ref, ng_ref, sel_ref, z_ref, xbc_ref, dtraw_ref,
                o_ref, bt_scr, dt_scr, hb_scr, h_scr, *, rows, nblk):
    sweep = pl.program_id(1)
    i = pl.program_id(2)
    blk = jnp.where(sweep == 0, nblk - 1 - i, i)
    L = SSD_CHUNK
    nsub = rows // L
    half = LANES // 2

    @pl.when(i == 0)
    def _():
        h_scr[...] = jnp.zeros(h_scr.shape, F32)

    rowi = lax.broadcasted_iota(jnp.int32, (L, L), 0)
    coli = lax.broadcasted_iota(jnp.int32, (L, L), 1)
    upper = (rowi <= coli).astype(BF16)
    lower = (rowi >= coli).astype(BF16)
    lo = lax.broadcasted_iota(jnp.int32, (L, LANES), 1) < half
    fwd_rows = lax.broadcasted_iota(jnp.int32, (N_SERIES, 1), 0) < SSD_HEADS
    neg_a = -jnp.exp(alog_ref[...])

    def series(dtt):
        hi, mid, low = _split3(dtt * neg_a)
        pre = sum(jnp.dot(t, upper, preferred_element_type=F32) for t in (hi, mid, low))
        suf = sum(jnp.dot(t, lower, preferred_element_type=F32) for t in (hi, mid, low))
        cst = jnp.where(fwd_rows, pre, suf)
        tot = jnp.where(fwd_rows, cst[:, L - 1:L], cst[:, 0:1])
        return cst, tot

    def chunk_state(bt, xs_g, w, j0):
        return (jnp.dot((bt * w[j0:j0 + 1, :]).astype(BF16), jnp.where(lo, xs_g, 0.0).astype(BF16),
                        preferred_element_type=F32)
                + jnp.dot((bt * w[j0 + 1:j0 + 2, :]).astype(BF16), jnp.where(lo, 0.0, xs_g).astype(BF16),
                          preferred_element_type=F32))

    def pick2(col, j0):
        return jnp.where(lo[0:1], col[j0:j0 + 1, :], col[j0 + 1:j0 + 2, :])

    def chunk_series(dtt_all):
        out = []
        for si in range(nsub):
            dtt = dtt_all[:, si * L:(si + 1) * L]
            cst, tot = series(dtt)
            out.append((dtt, cst, dtt * jnp.exp(tot - cst), jnp.exp(tot)))
        return out

    @pl.when(sweep == 0)
    def _():
        xbc = xbc_ref[...]
        xdt = dtraw_ref[...].T[0:N_SERIES, :] + dtb_ref[...]
        dtt_all = jnp.maximum(xdt, 0.0) + jnp.log1p(jnp.exp(-jnp.abs(xdt)))
        dt_scr[blk] = dtt_all
        ser = chunk_series(dtt_all)
        sts = {}
        for si in range(nsub):
            r0 = si * L
            for g in range(SSD_GROUPS):
                bt = xbc[r0:r0 + L, SSD_W + g * SSD_STATE:SSD_W + (g + 1) * SSD_STATE].T
                bt_scr[blk, g, :, r0:r0 + L] = bt
                sts[si, g] = chunk_state(bt, xbc[r0:r0 + L, g * LANES:(g + 1) * LANES], ser[si][2],
                                         SSD_HEADS + 2 * g)
        for g in range(SSD_GROUPS):
            hg = h_scr[g]
            for si in reversed(range(nsub)):
                hb_scr[blk * nsub + si, g] = hg
                hg = hg * pick2(ser[si][3], SSD_HEADS + 2 * g) + sts[si, g]
            h_scr[g] = hg

    @pl.when(sweep == 1)
    def _():
        zpad = jnp.zeros((LANES - 3 * N_SERIES, L), F32)
        masks = (coli <= rowi, coli >= rowi)
        ser = chunk_series(dt_scr[blk])
        bcs = []
        for si in range(nsub):
            hi, mid, low = _split3(ser[si][1])
            cs_col = jnp.concatenate([hi.astype(F32), mid.astype(F32), low.astype(F32), zpad], axis=0).T
            bcs.append(jnp.dot(cs_col.astype(BF16), sel_ref[...], preferred_element_type=F32))
        pairs = [(si, g) for si in range(nsub) for g in range(SSD_GROUPS)]
        xs, cbf, bts, gms, sts, ys = {}, {}, {}, {}, {}, {}
        for si, g in pairs:
            r0 = si * L
            xs[si, g] = xbc_ref[r0:r0 + L, g * LANES:(g + 1) * LANES]
            cbf[si, g] = xbc_ref[r0:r0 + L, SSD_W + (SSD_GROUPS + g) * SSD_STATE:
                                 SSD_W + (SSD_GROUPS + g + 1) * SSD_STATE].astype(BF16)
            bts[si, g] = bt_scr[blk, g, :, r0:r0 + L]
            gms[si, g] = jnp.dot(cbf[si, g], bts[si, g].astype(BF16), preferred_element_type=F32)
        for si, g in pairs:
            sts[si, g] = chunk_state(bts[si, g], xs[si, g], ser[si][2], 2 * g)
        for si, g in pairs:
            dtt, cst = ser[si][0], ser[si][1]
            xs_lo = jnp.where(lo, xs[si, g], 0.0).astype(BF16)
            xs_hi = jnp.where(lo, 0.0, xs[si, g]).astype(BF16)
            ms = []
            for h in range(2):
                tot_decay = None
                for d in range(2):
                    j = SSD_HEADS * d + 2 * g + h
                    seg = bcs[si][:, j * LANES:(j + 1) * LANES] - cst[j:j + 1, :]
                    term = jnp.where(masks[d], jnp.exp(seg), 0.0) * dtt[j:j + 1, :]
                    tot_decay = term if tot_decay is None else tot_decay + term
                ms.append((gms[si, g] * tot_decay).astype(BF16))
            ys[si, g] = jnp.dot(jnp.concatenate(ms, axis=1), jnp.concatenate([xs_lo, xs_hi], axis=0),
                                preferred_element_type=F32)
        for g in range(SSD_GROUPS):
            hg = h_scr[g]
            for si in range(nsub):
                r0 = si * L
                hb = hb_scr[blk * nsub + si, g]
                ch = jnp.dot(cbf[si, g], jnp.concatenate([hg.astype(BF16), hb.astype(BF16)], axis=1),
                             preferred_element_type=F32)
                hg = hg * pick2(ser[si][3], 2 * g) + sts[si, g]
                pf = (N_SERIES + g) * LANES
                pb = (N_SERIES + SSD_GROUPS + g) * LANES
                y = (ys[si, g] + jnp.exp(bcs[si][:, pf:pf + LANES]) * ch[:, 0:LANES]
                     + jnp.exp(bcs[si][:, pb:pb + LANES]) * ch[:, LANES:])
                gl = slice(g * LANES, (g + 1) * LANES)
                y = y + dexp_ref[:, gl] * xs[si, g]
                z = z_ref[r0:r0 + L, g * LANES:(g + 1) * LANES]
                y = y * (z * _sigmoid(z))
                o_ref[r0:r0 + L, gl] = _rms(y, ng_ref[:, gl]).astype(BF16)
            h_scr[g] = hg


def _ssd_mixers(z, xbc, dtraw, dtb_col, alog_col, dexp, ng, sel, layer, batch, seq, rows):
    t = z.shape[0]
    nblk = seq // rows

    def blk_rows(w):
        return pl.BlockSpec((rows, w), lambda bi, s, i: (bi * nblk + jnp.where(s == 0, nblk - 1 - i, i), 0))

    small = lambda r, n: pl.BlockSpec((None, r, n), lambda bi, s, i: (layer, 0, 0))
    return pl.pallas_call(
        functools.partial(_ssd_kernel, rows=rows, nblk=nblk),
        grid=(batch, 2, nblk),
        in_specs=[small(N_SERIES, 1), small(N_SERIES, 1), small(1, SSD_W), small(1, SSD_W),
                  pl.BlockSpec((LANES, SEL_COLS), lambda bi, s, i: (0, 0)),
                  blk_rows(SSD_W), blk_rows(SSD_XBC), blk_rows(LANES)],
        out_specs=pl.BlockSpec((rows, SSD_W), lambda bi, s, i: (bi * nblk + s * i, 0)),
        out_shape=jax.ShapeDtypeStruct((t, SSD_W), BF16),
        scratch_shapes=[pltpu.VMEM((nblk, SSD_GROUPS, SSD_STATE, rows), F32),
                        pltpu.VMEM((nblk, N_SERIES, rows), F32),
                        pltpu.VMEM((seq // SSD_CHUNK, SSD_GROUPS, SSD_STATE, LANES), F32),
                        pltpu.VMEM((SSD_GROUPS, SSD_STATE, LANES), F32)],
        compiler_params=pltpu.CompilerParams(dimension_semantics=("arbitrary", "arbitrary", "arbitrary"),
                                             vmem_limit_bytes=VMEM_LIMIT),
        name="ssd_mixers",
    )(dtb_col, alog_col, dexp, ng, sel, z, xbc, dtraw)


def _ffn_kernel(x_ref, xp_ref, xn_ref, ya_ref, yap_ref, yan_ref, yc_ref, ycp_ref, ycn_ref, ys_ref, ysp_ref, ysn_ref,
                wout_ref, g_ref, wup_ref, cwg_ref, cwu_ref, cbg_ref, cbu_ref, wd_ref, fg_ref,
                o_ref, slab_scr, hp_scr, act_scr, ua_scr, ub_scr, *, tm, tiles_per_seq, final_norm):
    ti = pl.program_id(0) % tiles_per_seq
    has_prev = ti > 0
    has_next = ti < tiles_per_seq - 1
    e_rows = tm + 2 * CONV_HALO
    seg = e_rows // SUBLANES
    n_slabs = D_MODEL // LANES

    def ext(prev, main, nxt):
        return jnp.concatenate([prev[...], main[...], nxt[...]], axis=0)

    mix = (jnp.dot(ext(yap_ref, ya_ref, yan_ref), wout_ref[0:ATTN_W, :], preferred_element_type=F32)
           + jnp.dot(ext(ycp_ref, yc_ref, ycn_ref), wout_ref[ATTN_W:ATTN_W + SC_W, :], preferred_element_type=F32)
           + jnp.dot(ext(ysp_ref, ys_ref, ysn_ref), wout_ref[ATTN_W + SC_W:, :], preferred_element_type=F32))
    xnew = (ext(xp_ref, x_ref, xn_ref) + mix)[HALO - CONV_HALO:HALO - CONV_HALO + e_rows]
    row = lax.broadcasted_iota(jnp.int32, (e_rows, 1), 0)
    valid = (row >= jnp.where(has_prev, 0, CONV_HALO)) & (row < jnp.where(has_next, e_rows, CONV_HALO + tm))
    h = jnp.where(valid, _rms(xnew, g_ref[...]), 0.0)
    o_ref[...] = xnew[CONV_HALO:CONV_HALO + tm]

    for c in range(n_slabs):
        slab_scr[c] = h[:, c * LANES:(c + 1) * LANES]
    for i in range(0, seg, 2):
        blk = [jnp.concatenate([slab_scr[c, pl.ds(i + d, SUBLANES, stride=seg), :] for c in range(n_slabs)], axis=1)
               for d in range(2)]
        hp_scr[SUBLANES * i:SUBLANES * (i + 2), :] = jnp.concatenate(blk, axis=0).astype(BF16)

    def up_proj(j, u_ref):
        hb = hp_scr[...]
        c0 = j * FF_CHUNK
        u_ref[0] = jnp.dot(hb, wup_ref[:, c0:c0 + FF_CHUNK], preferred_element_type=F32)
        u_ref[1] = jnp.dot(hb, wup_ref[:, D_FF + c0:D_FF + c0 + FF_CHUNK], preferred_element_type=F32)

    def conv_rows(u_ref, idx, w):
        last = e_rows - SUBLANES
        mid = (u_ref[idx, 0:last - SUBLANES, :] * w[0:1] + u_ref[idx, SUBLANES:last, :] * w[1:2]
               + u_ref[idx, 2 * SUBLANES:e_rows, :] * w[2:3])
        first = (pltpu.roll(u_ref[idx, last:e_rows, :], 1, 0) * w[0:1] + u_ref[idx, 0:SUBLANES, :] * w[1:2]
                 + u_ref[idx, SUBLANES:2 * SUBLANES, :] * w[2:3])
        end = (u_ref[idx, last - SUBLANES:last, :] * w[0:1] + u_ref[idx, last:e_rows, :] * w[1:2]
               + pltpu.roll(u_ref[idx, 0:SUBLANES, :], SUBLANES - 1, 0) * w[2:3])
        return jnp.concatenate([first, mid, end], axis=0)

    def gate_down(j, u_ref):
        cg = conv_rows(u_ref, 0, cwg_ref[j]) + cbg_ref[j]
        cu = conv_rows(u_ref, 1, cwu_ref[j]) + cbu_ref[j]
        act_scr[:, j * FF_CHUNK:(j + 1) * FF_CHUNK] = ((cg * _sigmoid(cg)) * cu).astype(BF16)

    bufs = (ua_scr, ub_scr)
    up_proj(0, bufs[0])
    for j in range(N_FF_CHUNKS):
        if j + 1 < N_FF_CHUNKS:
            up_proj(j + 1, bufs[(j + 1) % 2])
        gate_down(j, bufs[j % 2])
    down = jnp.dot(act_scr[...], wd_ref[...], preferred_element_type=F32)
    for i in range(seg):
        for c in range(n_slabs):
            slab_scr[c, pl.ds(i, SUBLANES, stride=seg), :] = down[SUBLANES * i:SUBLANES * (i + 1),
                                                                c * LANES:(c + 1) * LANES]
    for c in range(n_slabs):
        o_ref[:, c * LANES:(c + 1) * LANES] += slab_scr[c, CONV_HALO:CONV_HALO + tm, :]
    if final_norm:
        o_ref[...] = _rms(o_ref[...], fg_ref[...])


def _ffn(x, ya, yc, ys, w_out, g, w_up, cwg, cwu, cbg, cbu, wd, fg, layer, seq, tm, final_norm):
    t = x.shape[0]
    tiles_per_seq = seq // tm
    hb = tm // HALO
    nh = t // HALO

    def main(w):
        return pl.BlockSpec((tm, w), lambda i: (i, 0))

    def prev(w):
        return pl.BlockSpec((HALO, w), lambda i: (jnp.maximum(i * hb - 1, 0), 0))

    def nxt(w):
        return pl.BlockSpec((HALO, w), lambda i: (jnp.minimum((i + 1) * hb, nh - 1), 0))

    def resident(shape):
        nd = len(shape)
        return pl.BlockSpec((None,) + shape, lambda i: (layer,) + (0,) * nd,
                            pipeline_mode=pl.Buffered(1))

    e_rows = tm + 2 * CONV_HALO
    assert e_rows % SUBLANES == 0 and (e_rows // SUBLANES) % 8 != 0
    return pl.pallas_call(
        functools.partial(_ffn_kernel, tm=tm, tiles_per_seq=tiles_per_seq, final_norm=final_norm),
        grid=(t // tm,),
        in_specs=[main(D_MODEL), prev(D_MODEL), nxt(D_MODEL),
                  main(ATTN_W), prev(ATTN_W), nxt(ATTN_W),
                  main(SC_W), prev(SC_W), nxt(SC_W),
                  main(SSD_W), prev(SSD_W), nxt(SSD_W),
                  resident((D_MODEL, D_MODEL)),
                  resident((1, D_MODEL)),
                  resident((D_MODEL, 2 * D_FF)),
                  resident((N_FF_CHUNKS, 3, FF_CHUNK)),
                  resident((N_FF_CHUNKS, 3, FF_CHUNK)),
                  resident((N_FF_CHUNKS, 1, FF_CHUNK)),
                  resident((N_FF_CHUNKS, 1, FF_CHUNK)),
                  resident((D_FF, D_MODEL)),
                  pl.BlockSpec((1, D_MODEL), lambda i: (0, 0))],
        out_specs=main(D_MODEL),
        out_shape=jax.ShapeDtypeStruct((t, D_MODEL), F32),
        scratch_shapes=[pltpu.VMEM((D_MODEL // LANES, e_rows, LANES), F32),
                        pltpu.VMEM((e_rows, D_MODEL), BF16),
                        pltpu.VMEM((e_rows, D_FF), BF16),
                        pltpu.VMEM((2, e_rows, FF_CHUNK), F32),
                        pltpu.VMEM((2, e_rows, FF_CHUNK), F32)],
        compiler_params=pltpu.CompilerParams(dimension_semantics=("arbitrary",),
                                             vmem_limit_bytes=VMEM_LIMIT),
        name="ffn",
    )(x, x, x, ya, ya, ya, yc, yc, yc, ys, ys, ys, w_out, g, w_up, cwg, cwu, cbg, cbu, wd, fg)


def kernel(x, positions, norm_mix_g, w_in, lam_q1, lam_k1, lam_q2, lam_k2, subln_g, sc_conv_w, ssd_conv_w,
           ssd_conv_b, ssd_dt_bias, ssd_a_log, ssd_d, ssd_norm_g, w_out, norm_ffn_g, w_up, ffn_conv_w,
           ffn_conv_b, w_down, final_norm_g):
    batch, seq, _ = x.shape
    depth = w_in.shape[0]
    t = batch * seq
    tm = min(512, seq)
    tq = min(1024, seq)
    ssd_rows = min(512, seq)

    w_in_b = jnp.pad(w_in, ((0, 0), (0, 0), (0, IN_PAD - IN_COLS))).astype(BF16)
    w_out_b = w_out.astype(BF16)
    w_up_b = w_up.astype(BF16)
    wd = w_down.astype(BF16)
    fcw = ffn_conv_w.reshape(depth, 3, 2, N_FF_CHUNKS, FF_CHUNK)
    cwg = fcw[:, :, 0].transpose(0, 2, 1, 3)
    cwu = fcw[:, :, 1].transpose(0, 2, 1, 3)
    fcb = ffn_conv_b.reshape(depth, 2, N_FF_CHUNKS, 1, FF_CHUNK)
    cbg, cbu = fcb[:, 0], fcb[:, 1]
    row3 = lambda a: a.reshape(depth, 1, -1)
    dtb = ssd_dt_bias.reshape(depth, N_SERIES, 1)
    alog = ssd_a_log.reshape(depth, N_SERIES, 1)
    sel = _select_matrix()
    dexp = jnp.repeat(ssd_d, SSD_W // SSD_HEADS, axis=-1).reshape(depth, 1, SSD_W)
    fg = final_norm_g.reshape(1, D_MODEL)

    rope = _rope_tables(positions)
    xf = x.reshape(t, D_MODEL)
    for l in range(depth):
        lam_init = 0.8 - 0.6 * math.exp(-0.3 * l)
        q, k, vt, z, xbc, dtraw, yc = _in_proj(xf, row3(norm_mix_g), w_in_b, rope, sc_conv_w, ssd_conv_w,
                                               row3(ssd_conv_b), l, seq, tm)
        ya = _attention(q, k, vt, row3(lam_q1), row3(lam_k1), row3(lam_q2), row3(lam_k2),
                        subln_g.reshape(depth, V_DIM, 1), l, lam_init, batch, seq, tq)
        ys = _ssd_mixers(z, xbc, dtraw, dtb, alog, dexp, row3(ssd_norm_g), sel, l, batch, seq, ssd_rows)
        xf = _ffn(xf, ya, yc, ys, w_out_b, row3(norm_ffn_g), w_up_b, cwg, cwu, cbg, cbu, wd, fg,
                  l, seq, tm, l == depth - 1)
    return xf.reshape(batch, seq, D_MODEL)
```

```python
import functools
import math

import jax
import jax.numpy as jnp
from jax import lax
from jax.experimental import pallas as pl
from jax.experimental.pallas import tpu as pltpu

F32 = jnp.float32
BF16 = jnp.bfloat16

D_MODEL = 1024
EPS = 1e-5
N_ATTN_HEADS = 4
QK_DIM = 64
V_DIM = 128
ATTN_W = N_ATTN_HEADS * V_DIM
ROPE_THETA = 500000.0
ROT_DIM = QK_DIM // 4
Q_SCALE = QK_DIM ** -0.5 * math.log2(math.e)
SC_W = 256
SSD_W = 256
SSD_STATE = 128
SSD_GROUPS = 2
SSD_HEADS = 4
SSD_CHUNK = 128
SSD_XBC = SSD_W + 2 * SSD_GROUPS * SSD_STATE
D_FF = 2816
IN_COLS = 3336

QKV_COLS = 3 * ATTN_W
REST_REAL = IN_COLS - QKV_COLS
REST_W = 1920
R_SCB, R_SCC, R_SCH, R_Z, R_XBC, R_DT = 0, 256, 512, 768, 1024, 1792
W_MAIN_COLS = QKV_COLS + R_DT

LANES = 128
SUBLANES = 8
BF16_SUBLANES = 16
VMEM_LIMIT = 56 * 1024 * 1024

FF_CHUNK = 256
N_FF_CHUNKS = D_FF // FF_CHUNK
HALO = BF16_SUBLANES
CONV_HALO = SUBLANES


def _sigmoid(x):
    return 1.0 / (1.0 + jnp.exp(-x))


def _rms(x, g):
    ms = jnp.mean(x * x, axis=-1, keepdims=True)
    return x * lax.rsqrt(ms + EPS) * g


def _conv3_rows(ext, w, lo, n):
    tot = ext.shape[0]
    up = pltpu.roll(ext, 1, 0)
    dn = pltpu.roll(ext, tot - 1, 0)
    out = up[lo:lo + n] * w[0:1]
    out = out + ext[lo:lo + n] * w[1:2]
    out = out + dn[lo:lo + n] * w[2:3]
    return out


def _rope_kernel(pos_ref, invf_ref, o_ref):
    pos = pos_ref[...].astype(F32)
    ang = pos * invf_ref[...]
    lane = lax.broadcasted_iota(jnp.int32, ang.shape, 1) & (QK_DIM - 1)
    c = jnp.cos(ang)
    s = jnp.sin(ang)
    half = ROT_DIM // 2
    o_ref[:, 0:LANES] = c
    o_ref[:, LANES:2 * LANES] = jnp.where(lane < half, -s, 0.0)
    o_ref[:, 2 * LANES:3 * LANES] = jnp.where((lane >= half) & (lane < ROT_DIM), s, 0.0)


def _rope_tables(positions):
    t = positions.size
    tm = min(t, 2048)
    half = ROT_DIM // 2
    inv_freq = ROPE_THETA ** (-jnp.arange(0, ROT_DIM, 2, dtype=F32) / ROT_DIM)
    lane = jnp.arange(LANES) % QK_DIM
    invf = jnp.where(lane < ROT_DIM, inv_freq[lane % half], 0.0).astype(F32)[None, :]
    return pl.pallas_call(
        _rope_kernel,
        grid=(t // tm,),
        in_specs=[pl.BlockSpec((tm, 1), lambda i: (i, 0)),
                  pl.BlockSpec((1, LANES), lambda i: (0, 0))],
        out_specs=pl.BlockSpec((tm, 3 * LANES), lambda i: (i, 0)),
        out_shape=jax.ShapeDtypeStruct((t, 3 * LANES), F32),
        name="rope_tables",
    )(positions.reshape(t, 1), invf)


def _inproj_kernel(x_ref, g_ref, w_ref, wdt_ref, rope_ref, q_ref, k_ref, v_ref, r_ref, h_scr):
    h_scr[...] = _rms(x_ref[...], g_ref[...]).astype(BF16)
    c = rope_ref[:, 0:LANES]
    s1 = rope_ref[:, LANES:2 * LANES]
    s2 = rope_ref[:, 2 * LANES:3 * LANES]
    half = ROT_DIM // 2

    def rot(t):
        return t * c + pltpu.roll(t, LANES - half, 1) * s1 + pltpu.roll(t, half, 1) * s2

    cw = 2 * LANES
    for ci in range(2 * ATTN_W // cw):
        r = jnp.dot(h_scr[...], w_ref[:, ci * cw:(ci + 1) * cw], preferred_element_type=F32)
        for hf in range(2):
            t = rot(r[:, hf * LANES:(hf + 1) * LANES])
            col = ci * cw + hf * LANES
            if col < ATTN_W:
                q_ref[:, col:col + LANES] = (t * Q_SCALE).astype(BF16)
            else:
                k_ref[:, col - ATTN_W:col - ATTN_W + LANES] = t.astype(BF16)
    for ci in range(ATTN_W // cw):
        c0 = 2 * ATTN_W + ci * cw
        r = jnp.dot(h_scr[...], w_ref[:, c0:c0 + cw], preferred_element_type=F32)
        v_ref[ci * cw:(ci + 1) * cw, :] = r.T.astype(BF16)
    for c0 in range(0, R_DT, cw):
        r_ref[:, c0:c0 + cw] = jnp.dot(h_scr[...], w_ref[:, QKV_COLS + c0:QKV_COLS + c0 + cw],
                                       preferred_element_type=F32)
    r_ref[:, R_DT:R_DT + LANES] = jnp.dot(h_scr[...], wdt_ref[...], preferred_element_type=F32)


def _in_proj(x, g, w_in, w_dt, rope, layer, tm):
    t = x.shape[0]
    return pl.pallas_call(
        _inproj_kernel,
        grid=(t // tm,),
        in_specs=[pl.BlockSpec((tm, D_MODEL), lambda i: (i, 0)),
                  pl.BlockSpec((None, 1, D_MODEL), lambda i: (layer, 0, 0)),
                  pl.BlockSpec((None, D_MODEL, W_MAIN_COLS), lambda i: (layer, 0, 0)),
                  pl.BlockSpec((None, D_MODEL, LANES), lambda i: (layer, 0, 0)),
                  pl.BlockSpec((tm, 3 * LANES), lambda i: (i, 0))],
        out_specs=[pl.BlockSpec((tm, ATTN_W), lambda i: (i, 0)),
                   pl.BlockSpec((tm, ATTN_W), lambda i: (i, 0)),
                   pl.BlockSpec((None, ATTN_W, tm), lambda i: (i, 0, 0)),
                   pl.BlockSpec((tm, REST_W), lambda i: (i, 0))],
        out_shape=[jax.ShapeDtypeStruct((t, ATTN_W), BF16),
                   jax.ShapeDtypeStruct((t, ATTN_W), BF16),
                   jax.ShapeDtypeStruct((t // tm, ATTN_W, tm), BF16),
                   jax.ShapeDtypeStruct((t, REST_W), F32)],
        scratch_shapes=[pltpu.VMEM((tm, D_MODEL), BF16)],
        compiler_params=pltpu.CompilerParams(dimension_semantics=("arbitrary",),
                                             vmem_limit_bytes=VMEM_LIMIT),
        name="in_proj",
    )(x, g, w_in, w_dt, rope)


def _attn_kernel(lq1_ref, lk1_ref, lq2_ref, lk2_ref, sg_ref, q_ref, k_ref, vt_ref, o_ref,
                 qp_scr, acc_scr, *, lam_init, tq):
    seq = k_ref.shape[0]
    nk, _, tk = vt_ref.shape
    lam = (jnp.exp(jnp.sum(lq1_ref[...] * lk1_ref[...], axis=-1, keepdims=True))
           - jnp.exp(jnp.sum(lq2_ref[...] * lk2_ref[...], axis=-1, keepdims=True)) + lam_init)
    lane = lax.broadcasted_iota(jnp.int32, (tq, V_DIM), 1)
    nt = (((1,), (1,)), ((), ()))

    def q_tile(qi, carry):
        q0 = pl.multiple_of(qi * tq, tq)
        q = q_ref[pl.ds(q0, tq), :]
        zero = jnp.zeros_like(q)
        qp_scr[0:tq, :] = jnp.where(lane < QK_DIM, q, zero)
        qp_scr[tq:2 * tq, :] = jnp.where(lane >= QK_DIM, q, zero)
        acc_scr[...] = jnp.zeros(acc_scr.shape, F32)

        def scores(j):
            kb = k_ref[j * tk:(j + 1) * tk, :]
            sts = [lax.dot_general(kb, qp_scr[c * tq:(c + 1) * tq, :], nt, preferred_element_type=F32) for c in range(2)]
            return sts, [jnp.max(st, axis=0, keepdims=True) for st in sts]

        m_run = [jnp.full((1, tq), -jnp.inf, F32) for _ in range(2)]
        ones = jnp.ones((BF16_SUBLANES, tk), BF16)
        sts, cmax = scores(0)
        for j in range(nk):
            nxt = scores(j + 1) if j + 1 < nk else None
            vt1 = jnp.concatenate([vt_ref[j], ones], axis=0)
            for c in range(2):
                cols = slice(c * tq, (c + 1) * tq)
                m_new = jnp.maximum(m_run[c], cmax[c])
                alpha = jnp.exp2(m_run[c] - m_new)
                p = jnp.exp2(sts[c] - m_new).astype(BF16)
                acc_scr[:, cols] = alpha * acc_scr[:, cols] + jnp.dot(vt1, p, preferred_element_type=F32)
                m_run[c] = m_new
            if nxt is not None:
                sts, cmax = nxt
        o = acc_scr[0:V_DIM, :] / acc_scr[V_DIM:V_DIM + 1, :]
        o = o[:, 0:tq] - lam * o[:, tq:2 * tq]
        ms = jnp.mean(o * o, axis=0, keepdims=True)
        y = o * lax.rsqrt(ms + EPS) * sg_ref[...] * (1.0 - lam_init)
        o_ref[pl.ds(q0, tq), :] = y.T.astype(BF16)
        return carry

    lax.fori_loop(0, seq // tq, q_tile, 0)


def _attention(q, k, vt, lq1, lk1, lq2, lk2, subg_col, layer, lam_init, batch, seq, tq):
    t = q.shape[0]
    tk = vt.shape[2]
    small = lambda n: pl.BlockSpec((None, 1, n), lambda bi, hi: (layer, 0, 0))
    head = pl.BlockSpec((seq, V_DIM), lambda bi, hi: (bi, hi))
    return pl.pallas_call(
        functools.partial(_attn_kernel, lam_init=lam_init, tq=tq),
        grid=(batch, N_ATTN_HEADS),
        in_specs=[small(QK_DIM), small(QK_DIM), small(QK_DIM), small(QK_DIM),
                  pl.BlockSpec((None, V_DIM, 1), lambda bi, hi: (layer, 0, 0)),
                  head, head,
                  pl.BlockSpec((seq // tk, V_DIM, tk), lambda bi, hi: (bi, hi, 0))],
        out_specs=head,
        out_shape=jax.ShapeDtypeStruct((t, ATTN_W), BF16),
        scratch_shapes=[pltpu.VMEM((2 * tq, V_DIM), BF16),
                        pltpu.VMEM((V_DIM + BF16_SUBLANES, 2 * tq), F32)],
        compiler_params=pltpu.CompilerParams(dimension_semantics=("arbitrary", "arbitrary"),
                                             vmem_limit_bytes=VMEM_LIMIT),
        name="diff_attn",
    )(lq1, lk1, lq2, lk2, subg_col, q, k, vt)


N_SERIES = 2 * SSD_HEADS
N_PICK = 2 * SSD_GROUPS
SEL_COLS = (N_SERIES + N_PICK) * LANES


def _select_matrix():
    lane = jnp.arange(SEL_COLS)
    blk, within = lane // LANES, lane % LANES
    d, g = (blk - N_SERIES) // SSD_GROUPS, (blk - N_SERIES) % SSD_GROUPS
    src = jnp.where(blk < N_SERIES, blk, SSD_HEADS * d + 2 * g + (within >= LANES // 2))
    row = jnp.arange(LANES)[:, None]
    return ((row < 3 * N_SERIES) & (row % N_SERIES == src[None, :])).astype(BF16)


def _split3(x):
    hi = x.astype(BF16)
    r1 = x - hi.astype(F32)
    mid = r1.astype(BF16)
    lo = (r1 - mid.astype(F32)).astype(BF16)
    return hi, mid, lo


def _ssd_kernel(dtb_ref, alog_ref, dexp_ref, ng_ref, scw_ref, cw_ref, cb_ref, sel_ref, main_ref, prev_ref, next_ref,
                o_ref, xbc_scr, bt_scr, dt_scr, hb_scr, h_scr, *, rows, nblk):
    sweep = pl.program_id(1)
    i = pl.program_id(2)
    blk = jnp.where(sweep == 0, nblk - 1 - i, i)
    has_prev = blk > 0
    has_next = blk < nblk - 1
    L = SSD_CHUNK
    nsub = rows // L
    half = LANES // 2

    @pl.when(i == 0)
    def _():
        h_scr[...] = jnp.zeros(h_scr.shape, F32)

    def ext_cols(c0, c1):
        pv = jnp.where(has_prev, prev_ref[:, c0:c1], 0.0)
        nx = jnp.where(has_next, next_ref[:, c0:c1], 0.0)
        return jnp.concatenate([pv, main_ref[:, c0:c1], nx], axis=0)

    rowi = lax.broadcasted_iota(jnp.int32, (L, L), 0)
    coli = lax.broadcasted_iota(jnp.int32, (L, L), 1)
    upper = (rowi <= coli).astype(BF16)
    lower = (rowi >= coli).astype(BF16)
    lo = lax.broadcasted_iota(jnp.int32, (L, LANES), 1) < half
    fwd_rows = lax.broadcasted_iota(jnp.int32, (N_SERIES, 1), 0) < SSD_HEADS
    neg_a = -jnp.exp(alog_ref[...])

    def series(dtt):
        hi, mid, low = _split3(dtt * neg_a)
        pre = sum(jnp.dot(t, upper, preferred_element_type=F32) for t in (hi, mid, low))
        suf = sum(jnp.dot(t, lower, preferred_element_type=F32) for t in (hi, mid, low))
        cst = jnp.where(fwd_rows, pre, suf)
        tot = jnp.where(fwd_rows, cst[:, L - 1:L], cst[:, 0:1])
        return cst, tot

    def chunk_state(bt, xs_g, w, j0):
        return (jnp.dot((bt * w[j0:j0 + 1, :]).astype(BF16), jnp.where(lo, xs_g, 0.0).astype(BF16),
                        preferred_element_type=F32)
                + jnp.dot((bt * w[j0 + 1:j0 + 2, :]).astype(BF16), jnp.where(lo, 0.0, xs_g).astype(BF16),
                          preferred_element_type=F32))

    def pick2(col, j0):
        return jnp.where(lo[0:1], col[j0:j0 + 1, :], col[j0 + 1:j0 + 2, :])

    def chunk_series(dtt_all):
        out = []
        for si in range(nsub):
            dtt = dtt_all[:, si * L:(si + 1) * L]
            cst, tot = series(dtt)
            out.append((dtt, cst, dtt * jnp.exp(tot - cst), jnp.exp(tot)))
        return out

    @pl.when(sweep == 0)
    def _():
        xbc = _conv3_rows(ext_cols(R_XBC, R_XBC + SSD_XBC), cw_ref[...], SUBLANES, rows) + cb_ref[...]
        xbc = xbc * _sigmoid(xbc)
        xbc_scr[blk] = xbc
        xdt = main_ref[:, R_DT:R_DT + LANES].T[0:N_SERIES, :] + dtb_ref[...]
        dtt_all = jnp.maximum(xdt, 0.0) + jnp.log1p(jnp.exp(-jnp.abs(xdt)))
        dt_scr[blk] = dtt_all
        ser = chunk_series(dtt_all)
        sts = {}
        for si in range(nsub):
            r0 = si * L
            for g in range(SSD_GROUPS):
                bt = xbc[r0:r0 + L, SSD_W + g * SSD_STATE:SSD_W + (g + 1) * SSD_STATE].T
                bt_scr[blk, g, :, r0:r0 + L] = bt
                sts[si, g] = chunk_state(bt, xbc[r0:r0 + L, g * LANES:(g + 1) * LANES], ser[si][2],
                                         SSD_HEADS + 2 * g)
        for g in range(SSD_GROUPS):
            hg = h_scr[g]
            for si in reversed(range(nsub)):
                hb_scr[blk * nsub + si, g] = hg
                hg = hg * pick2(ser[si][3], SSD_HEADS + 2 * g) + sts[si, g]
            h_scr[g] = hg

    @pl.when(sweep == 1)
    def _():
        u = ext_cols(R_SCC, R_SCC + SC_W) * ext_cols(R_SCH, R_SCH + SC_W)
        yc = main_ref[:, R_SCB:R_SCB + SC_W] * _conv3_rows(u, scw_ref[...], SUBLANES, rows)
        o_ref[:, 0:SC_W] = yc.astype(BF16)
        zpad = jnp.zeros((LANES - 3 * N_SERIES, L), F32)
        masks = (coli <= rowi, coli >= rowi)
        ser = chunk_series(dt_scr[blk])
        bcs = []
        for si in range(nsub):
            hi, mid, low = _split3(ser[si][1])
            cs_col = jnp.concatenate([hi.astype(F32), mid.astype(F32), low.astype(F32), zpad], axis=0).T
            bcs.append(jnp.dot(cs_col.astype(BF16), sel_ref[...], preferred_element_type=F32))
        pairs = [(si, g) for si in range(nsub) for g in range(SSD_GROUPS)]
        xs, cbf, bts, gms, sts, ys = {}, {}, {}, {}, {}, {}
        for si, g in pairs:
            r0 = si * L
            xs[si, g] = xbc_scr[blk, r0:r0 + L, g * LANES:(g + 1) * LANES]
            cbf[si, g] = xbc_scr[blk, r0:r0 + L, SSD_W + (SSD_GROUPS + g) * SSD_STATE:
                                 SSD_W + (SSD_GROUPS + g + 1) * SSD_STATE].astype(BF16)
            bts[si, g] = bt_scr[blk, g, :, r0:r0 + L]
            gms[si, g] = jnp.dot(cbf[si, g], bts[si, g].astype(BF16), preferred_element_type=F32)
        for si, g in pairs:
            sts[si, g] = chunk_state(bts[si, g], xs[si, g], ser[si][2], 2 * g)
        for si, g in pairs:
            dtt, cst = ser[si][0], ser[si][1]
            xs_lo = jnp.where(lo, xs[si, g], 0.0).astype(BF16)
            xs_hi = jnp.where(lo, 0.0, xs[si, g]).astype(BF16)
            ms = []
            for h in range(2):
                tot_decay = None
                for d in range(2):
                    j = SSD_HEADS * d + 2 * g + h
                    seg = bcs[si][:, j * LANES:(j + 1) * LANES] - cst[j:j + 1, :]
                    term = jnp.where(masks[d], jnp.exp(seg), 0.0) * dtt[j:j + 1, :]
                    tot_decay = term if tot_decay is None else tot_decay + term
                ms.append((gms[si, g] * tot_decay).astype(BF16))
            ys[si, g] = jnp.dot(jnp.concatenate(ms, axis=1), jnp.concatenate([xs_lo, xs_hi], axis=0),
                                preferred_element_type=F32)
        for g in range(SSD_GROUPS):
            hg = h_scr[g]
            for si in range(nsub):
                r0 = si * L
                hb = hb_scr[blk * nsub + si, g]
                ch = jnp.dot(cbf[si, g], jnp.concatenate([hg.astype(BF16), hb.astype(BF16)], axis=1),
                             preferred_element_type=F32)
                hg = hg * pick2(ser[si][3], 2 * g) + sts[si, g]
                pf = (N_SERIES + g) * LANES
                pb = (N_SERIES + SSD_GROUPS + g) * LANES
                y = (ys[si, g] + jnp.exp(bcs[si][:, pf:pf + LANES]) * ch[:, 0:LANES]
                     + jnp.exp(bcs[si][:, pb:pb + LANES]) * ch[:, LANES:])
                gl = slice(g * LANES, (g + 1) * LANES)
                y = y + dexp_ref[:, gl] * xs[si, g]
                z = main_ref[r0:r0 + L, R_Z + g * LANES:R_Z + (g + 1) * LANES]
                y = y * (z * _sigmoid(z))
                o_ref[r0:r0 + L, SC_W + g * LANES:SC_W + (g + 1) * LANES] = _rms(y, ng_ref[:, gl]).astype(BF16)
            h_scr[g] = hg


def _ssd_mixers(rest, dtb_col, alog_col, dexp, ng, scw, cw, cb, sel, layer, batch, seq, rows):
    t = rest.shape[0]
    nblk = seq // rows
    r8 = rows // SUBLANES

    def blk_of(s, i):
        return jnp.where(s == 0, nblk - 1 - i, i)

    small = lambda r, n: pl.BlockSpec((None, r, n), lambda bi, s, i: (layer, 0, 0))
    return pl.pallas_call(
        functools.partial(_ssd_kernel, rows=rows, nblk=nblk),
        grid=(batch, 2, nblk),
        in_specs=[small(N_SERIES, 1), small(N_SERIES, 1), small(1, SSD_W), small(1, SSD_W),
                  small(3, SC_W), small(3, SSD_XBC), small(1, SSD_XBC),
                  pl.BlockSpec((LANES, SEL_COLS), lambda bi, s, i: (0, 0)),
                  pl.BlockSpec((rows, REST_W), lambda bi, s, i: (bi * nblk + blk_of(s, i), 0)),
                  pl.BlockSpec((SUBLANES, REST_W),
                               lambda bi, s, i: (jnp.maximum((bi * nblk + blk_of(s, i)) * r8 - 1, 0), 0)),
                  pl.BlockSpec((SUBLANES, REST_W),
                               lambda bi, s, i: (jnp.minimum((bi * nblk + blk_of(s, i) + 1) * r8,
                                                             t // SUBLANES - 1), 0))],
        out_specs=pl.BlockSpec((rows, SC_W + SSD_W), lambda bi, s, i: (bi * nblk + s * i, 0)),
        out_shape=jax.ShapeDtypeStruct((t, SC_W + SSD_W), BF16),
        scratch_shapes=[pltpu.VMEM((nblk, rows, SSD_XBC), F32),
                        pltpu.VMEM((nblk, SSD_GROUPS, SSD_STATE, rows), F32),
                        pltpu.VMEM((nblk, N_SERIES, rows), F32),
                        pltpu.VMEM((seq // SSD_CHUNK, SSD_GROUPS, SSD_STATE, LANES), F32),
                        pltpu.VMEM((SSD_GROUPS, SSD_STATE, LANES), F32)],
        compiler_params=pltpu.CompilerParams(dimension_semantics=("arbitrary", "arbitrary", "arbitrary"),
                                             vmem_limit_bytes=VMEM_LIMIT),
        name="ssd_mixers",
    )(dtb_col, alog_col, dexp, ng, scw, cw, cb, sel, rest, rest, rest)


def _ffn_kernel(x_ref, xp_ref, xn_ref, ya_ref, yap_ref, yan_ref, ym_ref, ymp_ref, ymn_ref,
                wout_ref, g_ref, wup_ref, cwg_ref, cwu_ref, cbg_ref, cbu_ref, wd_ref, fg_ref,
                o_ref, slab_scr, hp_scr, act_scr, ua_scr, ub_scr, *, tm, tiles_per_seq, final_norm):
    ti = pl.program_id(0) % tiles_per_seq
    has_prev = ti > 0
    has_next = ti < tiles_per_seq - 1
    e_rows = tm + 2 * CONV_HALO
    seg = e_rows // SUBLANES
    n_slabs = D_MODEL // LANES

    def ext(prev, main, nxt):
        return jnp.concatenate([prev[...], main[...], nxt[...]], axis=0)

    mix = (jnp.dot(ext(yap_ref, ya_ref, yan_ref), wout_ref[0:ATTN_W, :], preferred_element_type=F32)
           + jnp.dot(ext(ymp_ref, ym_ref, ymn_ref), wout_ref[ATTN_W:2 * ATTN_W, :], preferred_element_type=F32))
    xnew = (ext(xp_ref, x_ref, xn_ref) + mix)[HALO - CONV_HALO:HALO - CONV_HALO + e_rows]
    row = lax.broadcasted_iota(jnp.int32, (e_rows, 1), 0)
    valid = (row >= jnp.where(has_prev, 0, CONV_HALO)) & (row < jnp.where(has_next, e_rows, CONV_HALO + tm))
    h = jnp.where(valid, _rms(xnew, g_ref[...]), 0.0)
    o_ref[...] = xnew[CONV_HALO:CONV_HALO + tm]

    for c in range(n_slabs):
        slab_scr[c] = h[:, c * LANES:(c + 1) * LANES]
    for i in range(0, seg, 2):
        blk = [jnp.concatenate([slab_scr[c, pl.ds(i + d, SUBLANES, stride=seg), :] for c in range(n_slabs)], axis=1)
               for d in range(2)]
        hp_scr[SUBLANES * i:SUBLANES * (i + 2), :] = jnp.concatenate(blk, axis=0).astype(BF16)

    def up_proj(j, u_ref):
        hb = hp_scr[...]
        c0 = j * FF_CHUNK
        u_ref[0] = jnp.dot(hb, wup_ref[:, c0:c0 + FF_CHUNK], preferred_element_type=F32)
        u_ref[1] = jnp.dot(hb, wup_ref[:, D_FF + c0:D_FF + c0 + FF_CHUNK], preferred_element_type=F32)

    def conv_rows(u_ref, idx, w):
        last = e_rows - SUBLANES
        mid = (u_ref[idx, 0:last - SUBLANES, :] * w[0:1] + u_ref[idx, SUBLANES:last, :] * w[1:2]
               + u_ref[idx, 2 * SUBLANES:e_rows, :] * w[2:3])
        first = (pltpu.roll(u_ref[idx, last:e_rows, :], 1, 0) * w[0:1] + u_ref[idx, 0:SUBLANES, :] * w[1:2]
                 + u_ref[idx, SUBLANES:2 * SUBLANES, :] * w[2:3])
        end = (u_ref[idx, last - SUBLANES:last, :] * w[0:1] + u_ref[idx, last:e_rows, :] * w[1:2]
               + pltpu.roll(u_ref[idx, 0:SUBLANES, :], SUBLANES - 1, 0) * w[2:3])
        return jnp.concatenate([first, mid, end], axis=0)

    def gate_down(j, u_ref):
        cg = conv_rows(u_ref, 0, cwg_ref[j]) + cbg_ref[j]
        cu = conv_rows(u_ref, 1, cwu_ref[j]) + cbu_ref[j]
        act_scr[:, j * FF_CHUNK:(j + 1) * FF_CHUNK] = ((cg * _sigmoid(cg)) * cu).astype(BF16)

    bufs = (ua_scr, ub_scr)
    up_proj(0, bufs[0])
    for j in range(N_FF_CHUNKS):
        if j + 1 < N_FF_CHUNKS:
            up_proj(j + 1, bufs[(j + 1) % 2])
        gate_down(j, bufs[j % 2])
    down = jnp.dot(act_scr[...], wd_ref[...], preferred_element_type=F32)
    for i in range(seg):
        for c in range(n_slabs):
            slab_scr[c, pl.ds(i, SUBLANES, stride=seg), :] = down[SUBLANES * i:SUBLANES * (i + 1),
                                                                c * LANES:(c + 1) * LANES]
    for c in range(n_slabs):
        o_ref[:, c * LANES:(c + 1) * LANES] += slab_scr[c, CONV_HALO:CONV_HALO + tm, :]
    if final_norm:
        o_ref[...] = _rms(o_ref[...], fg_ref[...])


def _ffn(x, ya, ym, w_out, g, w_up, cwg, cwu, cbg, cbu, wd, fg, layer, seq, tm, final_norm):
    t = x.shape[0]
    tiles_per_seq = seq // tm
    hb = tm // HALO
    nh = t // HALO

    def main(w):
        return pl.BlockSpec((tm, w), lambda i: (i, 0))

    def prev(w):
        return pl.BlockSpec((HALO, w), lambda i: (jnp.maximum(i * hb - 1, 0), 0))

    def nxt(w):
        return pl.BlockSpec((HALO, w), lambda i: (jnp.minimum((i + 1) * hb, nh - 1), 0))

    def resident(shape):
        nd = len(shape)
        return pl.BlockSpec((None,) + shape, lambda i: (layer,) + (0,) * nd,
                            pipeline_mode=pl.Buffered(1))

    e_rows = tm + 2 * CONV_HALO
    assert e_rows % SUBLANES == 0 and (e_rows // SUBLANES) % 8 != 0
    return pl.pallas_call(
        functools.partial(_ffn_kernel, tm=tm, tiles_per_seq=tiles_per_seq, final_norm=final_norm),
        grid=(t // tm,),
        in_specs=[main(D_MODEL), prev(D_MODEL), nxt(D_MODEL),
                  main(ATTN_W), prev(ATTN_W), nxt(ATTN_W),
                  main(SC_W + SSD_W), prev(SC_W + SSD_W), nxt(SC_W + SSD_W),
                  resident((D_MODEL, D_MODEL)),
                  resident((1, D_MODEL)),
                  resident((D_MODEL, 2 * D_FF)),
                  resident((N_FF_CHUNKS, 3, FF_CHUNK)),
                  resident((N_FF_CHUNKS, 3, FF_CHUNK)),
                  resident((N_FF_CHUNKS, 1, FF_CHUNK)),
                  resident((N_FF_CHUNKS, 1, FF_CHUNK)),
                  resident((D_FF, D_MODEL)),
                  pl.BlockSpec((1, D_MODEL), lambda i: (0, 0))],
        out_specs=main(D_MODEL),
        out_shape=jax.ShapeDtypeStruct((t, D_MODEL), F32),
        scratch_shapes=[pltpu.VMEM((D_MODEL // LANES, e_rows, LANES), F32),
                        pltpu.VMEM((e_rows, D_MODEL), BF16),
                        pltpu.VMEM((e_rows, D_FF), BF16),
                        pltpu.VMEM((2, e_rows, FF_CHUNK), F32),
                        pltpu.VMEM((2, e_rows, FF_CHUNK), F32)],
        compiler_params=pltpu.CompilerParams(dimension_semantics=("arbitrary",),
                                             vmem_limit_bytes=VMEM_LIMIT),
        name="ffn",
    )(x, x, x, ya, ya, ya, ym, ym, ym, w_out, g, w_up, cwg, cwu, cbg, cbu, wd, fg)


def kernel(x, positions, norm_mix_g, w_in, lam_q1, lam_k1, lam_q2, lam_k2, subln_g, sc_conv_w, ssd_conv_w,
           ssd_conv_b, ssd_dt_bias, ssd_a_log, ssd_d, ssd_norm_g, w_out, norm_ffn_g, w_up, ffn_conv_w,
           ffn_conv_b, w_down, final_norm_g):
    batch, seq, _ = x.shape
    depth = w_in.shape[0]
    t = batch * seq
    tm = min(512, seq)
    tq = min(1024, seq)
    ssd_rows = min(1024, seq)

    w_in_b = w_in[:, :, 0:W_MAIN_COLS].astype(BF16)
    w_dt_b = jnp.pad(w_in[:, :, W_MAIN_COLS:], ((0, 0), (0, 0), (0, LANES - 2 * SSD_HEADS))).astype(BF16)
    w_out_b = w_out.astype(BF16)
    w_up_b = w_up.astype(BF16)
    wd = w_down.astype(BF16)
    fcw = ffn_conv_w.reshape(depth, 3, 2, N_FF_CHUNKS, FF_CHUNK)
    cwg = fcw[:, :, 0].transpose(0, 2, 1, 3)
    cwu = fcw[:, :, 1].transpose(0, 2, 1, 3)
    fcb = ffn_conv_b.reshape(depth, 2, N_FF_CHUNKS, 1, FF_CHUNK)
    cbg, cbu = fcb[:, 0], fcb[:, 1]
    row3 = lambda a: a.reshape(depth, 1, -1)
    dtb = ssd_dt_bias.reshape(depth, N_SERIES, 1)
    alog = ssd_a_log.reshape(depth, N_SERIES, 1)
    sel = _select_matrix()
    dexp = jnp.repeat(ssd_d, SSD_W // SSD_HEADS, axis=-1).reshape(depth, 1, SSD_W)
    fg = final_norm_g.reshape(1, D_MODEL)

    rope = _rope_tables(positions)
    xf = x.reshape(t, D_MODEL)
    for l in range(depth):
        lam_init = 0.8 - 0.6 * math.exp(-0.3 * l)
        q, k, vt, rest = _in_proj(xf, row3(norm_mix_g), w_in_b, w_dt_b, rope, l, tm)
        ya = _attention(q, k, vt, row3(lam_q1), row3(lam_k1), row3(lam_q2), row3(lam_k2),
                        subln_g.reshape(depth, V_DIM, 1), l, lam_init, batch, seq, tq)
        ym = _ssd_mixers(rest, dtb, alog, dexp, row3(ssd_norm_g), sc_conv_w, ssd_conv_w, row3(ssd_conv_b), sel,
                         l, batch, seq, ssd_rows)
        xf = _ffn(xf, ya, ym, w_out_b, row3(norm_ffn_g), w_up_b, cwg, cwu, cbg, cbu, wd, fg,
                  l, seq, tm, l == depth - 1)
    return xf.reshape(batch, seq, D_MODEL)
```

```python
import functools
import math

import jax
import jax.numpy as jnp
from jax import lax
from jax.experimental import pallas as pl
from jax.experimental.pallas import tpu as pltpu

F32 = jnp.float32
BF16 = jnp.bfloat16

D_MODEL = 1024
EPS = 1e-5
N_ATTN_HEADS = 4
QK_DIM = 64
V_DIM = 128
ATTN_W = N_ATTN_HEADS * V_DIM
ROPE_THETA = 500000.0
ROT_DIM = QK_DIM // 4
Q_SCALE = QK_DIM ** -0.5 * math.log2(math.e)
SC_W = 256
SSD_W = 256
SSD_STATE = 128
SSD_GROUPS = 2
SSD_HEADS = 4
SSD_CHUNK = 128
SSD_XBC = SSD_W + 2 * SSD_GROUPS * SSD_STATE
D_FF = 2816
IN_COLS = 3336

QKV_COLS = 3 * ATTN_W
REST_REAL = IN_COLS - QKV_COLS
REST_W = 1920
IN_PAD = QKV_COLS + REST_W
R_SCB, R_SCC, R_SCH, R_Z, R_XBC, R_DT = 0, 256, 512, 768, 1024, 1792

LANES = 128
SUBLANES = 8
BF16_SUBLANES = 16
VMEM_LIMIT = 56 * 1024 * 1024

FF_CHUNK = 256
N_FF_CHUNKS = D_FF // FF_CHUNK
HALO = BF16_SUBLANES
CONV_HALO = SUBLANES


def _sigmoid(x):
    return 1.0 / (1.0 + jnp.exp(-x))


def _rms(x, g):
    ms = jnp.mean(x * x, axis=-1, keepdims=True)
    return x * lax.rsqrt(ms + EPS) * g


def _conv3_rows(ext, w, lo, n):
    tot = ext.shape[0]
    up = pltpu.roll(ext, 1, 0)
    dn = pltpu.roll(ext, tot - 1, 0)
    out = up[lo:lo + n] * w[0:1]
    out = out + ext[lo:lo + n] * w[1:2]
    out = out + dn[lo:lo + n] * w[2:3]
    return out


def _rope_kernel(pos_ref, invf_ref, o_ref):
    pos = pos_ref[...].astype(F32)
    ang = pos * invf_ref[...]
    lane = lax.broadcasted_iota(jnp.int32, ang.shape, 1) & (QK_DIM - 1)
    c = jnp.cos(ang)
    s = jnp.sin(ang)
    half = ROT_DIM // 2
    o_ref[:, 0:LANES] = c
    o_ref[:, LANES:2 * LANES] = jnp.where(lane < half, -s, 0.0)
    o_ref[:, 2 * LANES:3 * LANES] = jnp.where((lane >= half) & (lane < ROT_DIM), s, 0.0)


def _rope_tables(positions):
    t = positions.size
    tm = min(t, 2048)
    half = ROT_DIM // 2
    inv_freq = ROPE_THETA ** (-jnp.arange(0, ROT_DIM, 2, dtype=F32) / ROT_DIM)
    lane = jnp.arange(LANES) % QK_DIM
    invf = jnp.where(lane < ROT_DIM, inv_freq[lane % half], 0.0).astype(F32)[None, :]
    return pl.pallas_call(
        _rope_kernel,
        grid=(t // tm,),
        in_specs=[pl.BlockSpec((tm, 1), lambda i: (i, 0)),
                  pl.BlockSpec((1, LANES), lambda i: (0, 0))],
        out_specs=pl.BlockSpec((tm, 3 * LANES), lambda i: (i, 0)),
        out_shape=jax.ShapeDtypeStruct((t, 3 * LANES), F32),
        name="rope_tables",
    )(positions.reshape(t, 1), invf)


def _inproj_kernel(x_ref, g_ref, w_ref, rope_ref, q_ref, k_ref, v_ref, r_ref, h_scr):
    h_scr[...] = _rms(x_ref[...], g_ref[...]).astype(BF16)
    c = rope_ref[:, 0:LANES]
    s1 = rope_ref[:, LANES:2 * LANES]
    s2 = rope_ref[:, 2 * LANES:3 * LANES]
    half = ROT_DIM // 2

    def rot(t):
        return t * c + pltpu.roll(t, LANES - half, 1) * s1 + pltpu.roll(t, half, 1) * s2

    cw = 2 * LANES
    for ci in range(2 * ATTN_W // cw):
        r = jnp.dot(h_scr[...], w_ref[:, ci * cw:(ci + 1) * cw], preferred_element_type=F32)
        for hf in range(2):
            t = rot(r[:, hf * LANES:(hf + 1) * LANES])
            col = ci * cw + hf * LANES
            if col < ATTN_W:
                q_ref[:, col:col + LANES] = (t * Q_SCALE).astype(BF16)
            else:
                k_ref[:, col - ATTN_W:col - ATTN_W + LANES] = t.astype(BF16)
    for ci in range(ATTN_W // cw):
        c0 = 2 * ATTN_W + ci * cw
        r = jnp.dot(h_scr[...], w_ref[:, c0:c0 + cw], preferred_element_type=F32)
        v_ref[ci * cw:(ci + 1) * cw, :] = r.T.astype(BF16)
    c0 = 0
    while c0 < REST_W:
        w = min(cw, REST_W - c0)
        r_ref[:, c0:c0 + w] = jnp.dot(h_scr[...], w_ref[:, QKV_COLS + c0:QKV_COLS + c0 + w],
                                      preferred_element_type=F32)
        c0 += w


def _in_proj(x, g, w_in, rope, layer, tm):
    t = x.shape[0]
    return pl.pallas_call(
        _inproj_kernel,
        grid=(t // tm,),
        in_specs=[pl.BlockSpec((tm, D_MODEL), lambda i: (i, 0)),
                  pl.BlockSpec((None, 1, D_MODEL), lambda i: (layer, 0, 0)),
                  pl.BlockSpec((None, D_MODEL, IN_PAD), lambda i: (layer, 0, 0)),
                  pl.BlockSpec((tm, 3 * LANES), lambda i: (i, 0))],
        out_specs=[pl.BlockSpec((tm, ATTN_W), lambda i: (i, 0)),
                   pl.BlockSpec((tm, ATTN_W), lambda i: (i, 0)),
                   pl.BlockSpec((None, ATTN_W, tm), lambda i: (i, 0, 0)),
                   pl.BlockSpec((tm, REST_W), lambda i: (i, 0))],
        out_shape=[jax.ShapeDtypeStruct((t, ATTN_W), BF16),
                   jax.ShapeDtypeStruct((t, ATTN_W), BF16),
                   jax.ShapeDtypeStruct((t // tm, ATTN_W, tm), BF16),
                   jax.ShapeDtypeStruct((t, REST_W), F32)],
        scratch_shapes=[pltpu.VMEM((tm, D_MODEL), BF16)],
        compiler_params=pltpu.CompilerParams(dimension_semantics=("arbitrary",),
                                             vmem_limit_bytes=VMEM_LIMIT),
        name="in_proj",
    )(x, g, w_in, rope)


def _attn_kernel(lq1_ref, lk1_ref, lq2_ref, lk2_ref, sg_ref, q_ref, k_ref, vt_ref, o_ref,
                 qp_scr, acc_scr, *, lam_init, tq):
    seq = k_ref.shape[0]
    nk, _, tk = vt_ref.shape
    lam = (jnp.exp(jnp.sum(lq1_ref[...] * lk1_ref[...], axis=-1, keepdims=True))
           - jnp.exp(jnp.sum(lq2_ref[...] * lk2_ref[...], axis=-1, keepdims=True)) + lam_init)
    lane = lax.broadcasted_iota(jnp.int32, (tq, V_DIM), 1)
    nt = (((1,), (1,)), ((), ()))

    def q_tile(qi, carry):
        q0 = pl.multiple_of(qi * tq, tq)
        q = q_ref[pl.ds(q0, tq), :]
        zero = jnp.zeros_like(q)
        qp_scr[0:tq, :] = jnp.where(lane < QK_DIM, q, zero)
        qp_scr[tq:2 * tq, :] = jnp.where(lane >= QK_DIM, q, zero)
        acc_scr[...] = jnp.zeros(acc_scr.shape, F32)

        def scores(j):
            kb = k_ref[j * tk:(j + 1) * tk, :]
            sts = [lax.dot_general(kb, qp_scr[c * tq:(c + 1) * tq, :], nt, preferred_element_type=F32) for c in range(2)]
            return sts, [jnp.max(st, axis=0, keepdims=True) for st in sts]

        m_run = [jnp.full((1, tq), -jnp.inf, F32) for _ in range(2)]
        ones = jnp.ones((BF16_SUBLANES, tk), BF16)
        sts, cmax = scores(0)
        for j in range(nk):
            nxt = scores(j + 1) if j + 1 < nk else None
            vt1 = jnp.concatenate([vt_ref[j], ones], axis=0)
            for c in range(2):
                cols = slice(c * tq, (c + 1) * tq)
                m_new = jnp.maximum(m_run[c], cmax[c])
                alpha = jnp.exp2(m_run[c] - m_new)
                p = jnp.exp2(sts[c] - m_new).astype(BF16)
                acc_scr[:, cols] = alpha * acc_scr[:, cols] + jnp.dot(vt1, p, preferred_element_type=F32)
                m_run[c] = m_new
            if nxt is not None:
                sts, cmax = nxt
        o = acc_scr[0:V_DIM, :] / acc_scr[V_DIM:V_DIM + 1, :]
        o = o[:, 0:tq] - lam * o[:, tq:2 * tq]
        ms = jnp.mean(o * o, axis=0, keepdims=True)
        y = o * lax.rsqrt(ms + EPS) * sg_ref[...] * (1.0 - lam_init)
        o_ref[pl.ds(q0, tq), :] = y.T.astype(BF16)
        return carry

    lax.fori_loop(0, seq // tq, q_tile, 0)


def _attention(q, k, vt, lq1, lk1, lq2, lk2, subg_col, layer, lam_init, batch, seq, tq):
    t = q.shape[0]
    tk = vt.shape[2]
    small = lambda n: pl.BlockSpec((None, 1, n), lambda bi, hi: (layer, 0, 0))
    head = pl.BlockSpec((seq, V_DIM), lambda bi, hi: (bi, hi))
    return pl.pallas_call(
        functools.partial(_attn_kernel, lam_init=lam_init, tq=tq),
        grid=(batch, N_ATTN_HEADS),
        in_specs=[small(QK_DIM), small(QK_DIM), small(QK_DIM), small(QK_DIM),
                  pl.BlockSpec((None, V_DIM, 1), lambda bi, hi: (layer, 0, 0)),
                  head, head,
                  pl.BlockSpec((seq // tk, V_DIM, tk), lambda bi, hi: (bi, hi, 0))],
        out_specs=head,
        out_shape=jax.ShapeDtypeStruct((t, ATTN_W), BF16),
        scratch_shapes=[pltpu.VMEM((2 * tq, V_DIM), BF16),
                        pltpu.VMEM((V_DIM + BF16_SUBLANES, 2 * tq), F32)],
        compiler_params=pltpu.CompilerParams(dimension_semantics=("arbitrary", "arbitrary"),
                                             vmem_limit_bytes=VMEM_LIMIT),
        name="diff_attn",
    )(lq1, lk1, lq2, lk2, subg_col, q, k, vt)


N_SERIES = 2 * SSD_HEADS
N_PICK = 2 * SSD_GROUPS
SEL_COLS = (N_SERIES + N_PICK) * LANES


def _select_matrix():
    lane = jnp.arange(SEL_COLS)
    blk, within = lane // LANES, lane % LANES
    d, g = (blk - N_SERIES) // SSD_GROUPS, (blk - N_SERIES) % SSD_GROUPS
    src = jnp.where(blk < N_SERIES, blk, SSD_HEADS * d + 2 * g + (within >= LANES // 2))
    row = jnp.arange(LANES)[:, None]
    return ((row < 3 * N_SERIES) & (row % N_SERIES == src[None, :])).astype(BF16)


def _split3(x):
    hi = x.astype(BF16)
    r1 = x - hi.astype(F32)
    mid = r1.astype(BF16)
    lo = (r1 - mid.astype(F32)).astype(BF16)
    return hi, mid, lo


def _ssd_kernel(dtb_ref, alog_ref, dexp_ref, ng_ref, scw_ref, cw_ref, cb_ref, sel_ref, main_ref, prev_ref, next_ref,
                o_ref, xbc_scr, bt_scr, dt_scr, hb_scr, h_scr, *, rows, nblk):
    sweep = pl.program_id(1)
    i = pl.program_id(2)
    blk = jnp.where(sweep == 0, nblk - 1 - i, i)
    has_prev = blk > 0
    has_next = blk < nblk - 1
    L = SSD_CHUNK
    nsub = rows // L
    half = LANES // 2

    @pl.when(i == 0)
    def _():
        h_scr[...] = jnp.zeros(h_scr.shape, F32)

    def ext_cols(c0, c1):
        pv = jnp.where(has_prev, prev_ref[:, c0:c1], 0.0)
        nx = jnp.where(has_next, next_ref[:, c0:c1], 0.0)
        return jnp.concatenate([pv, main_ref[:, c0:c1], nx], axis=0)

    rowi = lax.broadcasted_iota(jnp.int32, (L, L), 0)
    coli = lax.broadcasted_iota(jnp.int32, (L, L), 1)
    upper = (rowi <= coli).astype(BF16)
    lower = (rowi >= coli).astype(BF16)
    lo = lax.broadcasted_iota(jnp.int32, (L, LANES), 1) < half
    fwd_rows = lax.broadcasted_iota(jnp.int32, (N_SERIES, 1), 0) < SSD_HEADS
    neg_a = -jnp.exp(alog_ref[...])

    def series(dtt):
        hi, mid, low = _split3(dtt * neg_a)
        pre = sum(jnp.dot(t, upper, preferred_element_type=F32) for t in (hi, mid, low))
        suf = sum(jnp.dot(t, lower, preferred_element_type=F32) for t in (hi, mid, low))
        cst = jnp.where(fwd_rows, pre, suf)
        tot = jnp.where(fwd_rows, cst[:, L - 1:L], cst[:, 0:1])
        return cst, tot

    def chunk_state(bt, xs_g, w, j0):
        return (jnp.dot((bt * w[j0:j0 + 1, :]).astype(BF16), jnp.where(lo, xs_g, 0.0).astype(BF16),
                        preferred_element_type=F32)
                + jnp.dot((bt * w[j0 + 1:j0 + 2, :]).astype(BF16), jnp.where(lo, 0.0, xs_g).astype(BF16),
                          preferred_element_type=F32))

    def pick2(col, j0):
        return jnp.where(lo[0:1], col[j0:j0 + 1, :], col[j0 + 1:j0 + 2, :])

    def chunk_series(dtt_all):
        out = []
        for si in range(nsub):
            dtt = dtt_all[:, si * L:(si + 1) * L]
            cst, tot = series(dtt)
            out.append((dtt, cst, dtt * jnp.exp(tot - cst), jnp.exp(tot)))
        return out

    @pl.when(sweep == 0)
    def _():
        xbc = _conv3_rows(ext_cols(R_XBC, R_XBC + SSD_XBC), cw_ref[...], SUBLANES, rows) + cb_ref[...]
        xbc = xbc * _sigmoid(xbc)
        xbc_scr[blk] = xbc
        xdt = main_ref[:, R_DT:R_DT + LANES].T[0:N_SERIES, :] + dtb_ref[...]
        dtt_all = jnp.maximum(xdt, 0.0) + jnp.log1p(jnp.exp(-jnp.abs(xdt)))
        dt_scr[blk] = dtt_all
        ser = chunk_series(dtt_all)
        sts = {}
        for si in range(nsub):
            r0 = si * L
            for g in range(SSD_GROUPS):
                bt = xbc[r0:r0 + L, SSD_W + g * SSD_STATE:SSD_W + (g + 1) * SSD_STATE].T
                bt_scr[blk, g, :, r0:r0 + L] = bt
                sts[si, g] = chunk_state(bt, xbc[r0:r0 + L, g * LANES:(g + 1) * LANES], ser[si][2],
                                         SSD_HEADS + 2 * g)
        for g in range(SSD_GROUPS):
            hg = h_scr[g]
            for si in reversed(range(nsub)):
                hb_scr[blk * nsub + si, g] = hg
                hg = hg * pick2(ser[si][3], SSD_HEADS + 2 * g) + sts[si, g]
            h_scr[g] = hg

    @pl.when(sweep == 1)
    def _():
        u = ext_cols(R_SCC, R_SCC + SC_W) * ext_cols(R_SCH, R_SCH + SC_W)
        yc = main_ref[:, R_SCB:R_SCB + SC_W] * _conv3_rows(u, scw_ref[...], SUBLANES, rows)
        o_ref[:, 0:SC_W] = yc.astype(BF16)
        zpad = jnp.zeros((LANES - 3 * N_SERIES, L), F32)
        masks = (coli <= rowi, coli >= rowi)
        ser = chunk_series(dt_scr[blk])
        bcs = []
        for si in range(nsub):
            hi, mid, low = _split3(ser[si][1])
            cs_col = jnp.concatenate([hi.astype(F32), mid.astype(F32), low.astype(F32), zpad], axis=0).T
            bcs.append(jnp.dot(cs_col.astype(BF16), sel_ref[...], preferred_element_type=F32))
        pairs = [(si, g) for si in range(nsub) for g in range(SSD_GROUPS)]
        xs, cbf, bts, gms, sts, ys = {}, {}, {}, {}, {}, {}
        for si, g in pairs:
            r0 = si * L
            xs[si, g] = xbc_scr[blk, r0:r0 + L, g * LANES:(g + 1) * LANES]
            cbf[si, g] = xbc_scr[blk, r0:r0 + L, SSD_W + (SSD_GROUPS + g) * SSD_STATE:
                                 SSD_W + (SSD_GROUPS + g + 1) * SSD_STATE].astype(BF16)
            bts[si, g] = bt_scr[blk, g, :, r0:r0 + L]
            gms[si, g] = jnp.dot(cbf[si, g], bts[si, g].astype(BF16), preferred_element_type=F32)
        for si, g in pairs:
            sts[si, g] = chunk_state(bts[si, g], xs[si, g], ser[si][2], 2 * g)
        for si, g in pairs:
            dtt, cst = ser[si][0], ser[si][1]
            xs_lo = jnp.where(lo, xs[si, g], 0.0).astype(BF16)
            xs_hi = jnp.where(lo, 0.0, xs[si, g]).astype(BF16)
            ms = []
            for h in range(2):
                tot_decay = None
                for d in range(2):
                    j = SSD_HEADS * d + 2 * g + h
                    seg = bcs[si][:, j * LANES:(j + 1) * LANES] - cst[j:j + 1, :]
                    term = jnp.where(masks[d], jnp.exp(seg), 0.0) * dtt[j:j + 1, :]
                    tot_decay = term if tot_decay is None else tot_decay + term
                ms.append((gms[si, g] * tot_decay).astype(BF16))
            ys[si, g] = jnp.dot(jnp.concatenate(ms, axis=1), jnp.concatenate([xs_lo, xs_hi], axis=0),
                                preferred_element_type=F32)
        for g in range(SSD_GROUPS):
            hg = h_scr[g]
            for si in range(nsub):
                r0 = si * L
                hb = hb_scr[blk * nsub + si, g]
                ch = jnp.dot(cbf[si, g], jnp.concatenate([hg.astype(BF16), hb.astype(BF16)], axis=1),
                             preferred_element_type=F32)
                hg = hg * pick2(ser[si][3], 2 * g) + sts[si, g]
                pf = (N_SERIES + g) * LANES
                pb = (N_SERIES + SSD_GROUPS + g) * LANES
                y = (ys[si, g] + jnp.exp(bcs[si][:, pf:pf + LANES]) * ch[:, 0:LANES]
                     + jnp.exp(bcs[si][:, pb:pb + LANES]) * ch[:, LANES:])
                gl = slice(g * LANES, (g + 1) * LANES)
                y = y + dexp_ref[:, gl] * xs[si, g]
                z = main_ref[r0:r0 + L, R_Z + g * LANES:R_Z + (g + 1) * LANES]
                y = y * (z * _sigmoid(z))
                o_ref[r0:r0 + L, SC_W + g * LANES:SC_W + (g + 1) * LANES] = _rms(y, ng_ref[:, gl]).astype(BF16)
            h_scr[g] = hg


def _ssd_mixers(rest, dtb_col, alog_col, dexp, ng, scw, cw, cb, sel, layer, batch, seq, rows):
    t = rest.shape[0]
    nblk = seq // rows
    r8 = rows // SUBLANES

    def blk_of(s, i):
        return jnp.where(s == 0, nblk - 1 - i, i)

    small = lambda r, n: pl.BlockSpec((None, r, n), lambda bi, s, i: (layer, 0, 0))
    return pl.pallas_call(
        functools.partial(_ssd_kernel, rows=rows, nblk=nblk),
        grid=(batch, 2, nblk),
        in_specs=[small(N_SERIES, 1), small(N_SERIES, 1), small(1, SSD_W), small(1, SSD_W),
                  small(3, SC_W), small(3, SSD_XBC), small(1, SSD_XBC),
                  pl.BlockSpec((LANES, SEL_COLS), lambda bi, s, i: (0, 0)),
                  pl.BlockSpec((rows, REST_W), lambda bi, s, i: (bi * nblk + blk_of(s, i), 0)),
                  pl.BlockSpec((SUBLANES, REST_W),
                               lambda bi, s, i: (jnp.maximum((bi * nblk + blk_of(s, i)) * r8 - 1, 0), 0)),
                  pl.BlockSpec((SUBLANES, REST_W),
                               lambda bi, s, i: (jnp.minimum((bi * nblk + blk_of(s, i) + 1) * r8,
                                                             t // SUBLANES - 1), 0))],
        out_specs=pl.BlockSpec((rows, SC_W + SSD_W), lambda bi, s, i: (bi * nblk + s * i, 0)),
        out_shape=jax.ShapeDtypeStruct((t, SC_W + SSD_W), BF16),
        scratch_shapes=[pltpu.VMEM((nblk, rows, SSD_XBC), F32),
                        pltpu.VMEM((nblk, SSD_GROUPS, SSD_STATE, rows), F32),
                        pltpu.VMEM((nblk, N_SERIES, rows), F32),
                        pltpu.VMEM((seq // SSD_CHUNK, SSD_GROUPS, SSD_STATE, LANES), F32),
                        pltpu.VMEM((SSD_GROUPS, SSD_STATE, LANES), F32)],
        compiler_params=pltpu.CompilerParams(dimension_semantics=("arbitrary", "arbitrary", "arbitrary"),
                                             vmem_limit_bytes=VMEM_LIMIT),
        name="ssd_mixers",
    )(dtb_col, alog_col, dexp, ng, scw, cw, cb, sel, rest, rest, rest)


def _ffn_kernel(x_ref, xp_ref, xn_ref, ya_ref, yap_ref, yan_ref, ym_ref, ymp_ref, ymn_ref,
                wout_ref, g_ref, wup_ref, cwg_ref, cwu_ref, cbg_ref, cbu_ref, wd_ref, fg_ref,
                o_ref, slab_scr, hp_scr, act_scr, ua_scr, ub_scr, *, tm, tiles_per_seq, final_norm):
    ti = pl.program_id(0) % tiles_per_seq
    has_prev = ti > 0
    has_next = ti < tiles_per_seq - 1
    e_rows = tm + 2 * CONV_HALO
    seg = e_rows // SUBLANES
    n_slabs = D_MODEL // LANES

    def ext(prev, main, nxt):
        return jnp.concatenate([prev[...], main[...], nxt[...]], axis=0)

    mix = (jnp.dot(ext(yap_ref, ya_ref, yan_ref), wout_ref[0:ATTN_W, :], preferred_element_type=F32)
           + jnp.dot(ext(ymp_ref, ym_ref, ymn_ref), wout_ref[ATTN_W:2 * ATTN_W, :], preferred_element_type=F32))
    xnew = (ext(xp_ref, x_ref, xn_ref) + mix)[HALO - CONV_HALO:HALO - CONV_HALO + e_rows]
    row = lax.broadcasted_iota(jnp.int32, (e_rows, 1), 0)
    valid = (row >= jnp.where(has_prev, 0, CONV_HALO)) & (row < jnp.where(has_next, e_rows, CONV_HALO + tm))
    h = jnp.where(valid, _rms(xnew, g_ref[...]), 0.0)
    o_ref[...] = xnew[CONV_HALO:CONV_HALO + tm]

    for c in range(n_slabs):
        slab_scr[c] = h[:, c * LANES:(c + 1) * LANES]
    for i in range(0, seg, 2):
        blk = [jnp.concatenate([slab_scr[c, pl.ds(i + d, SUBLANES, stride=seg), :] for c in range(n_slabs)], axis=1)
               for d in range(2)]
        hp_scr[SUBLANES * i:SUBLANES * (i + 2), :] = jnp.concatenate(blk, axis=0).astype(BF16)

    def up_proj(j, u_ref):
        hb = hp_scr[...]
        c0 = j * FF_CHUNK
        u_ref[0] = jnp.dot(hb, wup_ref[:, c0:c0 + FF_CHUNK], preferred_element_type=F32)
        u_ref[1] = jnp.dot(hb, wup_ref[:, D_FF + c0:D_FF + c0 + FF_CHUNK], preferred_element_type=F32)

    def conv_rows(u_ref, idx, w):
        last = e_rows - SUBLANES
        mid = (u_ref[idx, 0:last - SUBLANES, :] * w[0:1] + u_ref[idx, SUBLANES:last, :] * w[1:2]
               + u_ref[idx, 2 * SUBLANES:e_rows, :] * w[2:3])
        first = (pltpu.roll(u_ref[idx, last:e_rows, :], 1, 0) * w[0:1] + u_ref[idx, 0:SUBLANES, :] * w[1:2]
                 + u_ref[idx, SUBLANES:2 * SUBLANES, :] * w[2:3])
        end = (u_ref[idx, last - SUBLANES:last, :] * w[0:1] + u_ref[idx, last:e_rows, :] * w[1:2]
               + pltpu.roll(u_ref[idx, 0:SUBLANES, :], SUBLANES - 1, 0) * w[2:3])
        return jnp.concatenate([first, mid, end], axis=0)

    def gate_down(j, u_ref):
        cg = conv_rows(u_ref, 0, cwg_ref[j]) + cbg_ref[j]
        cu = conv_rows(u_ref, 1, cwu_ref[j]) + cbu_ref[j]
        act_scr[:, j * FF_CHUNK:(j + 1) * FF_CHUNK] = ((cg * _sigmoid(cg)) * cu).astype(BF16)

    bufs = (ua_scr, ub_scr)
    up_proj(0, bufs[0])
    for j in range(N_FF_CHUNKS):
        if j + 1 < N_FF_CHUNKS:
            up_proj(j + 1, bufs[(j + 1) % 2])
        gate_down(j, bufs[j % 2])
    down = jnp.dot(act_scr[...], wd_ref[...], preferred_element_type=F32)
    for i in range(seg):
        for c in range(n_slabs):
            slab_scr[c, pl.ds(i, SUBLANES, stride=seg), :] = down[SUBLANES * i:SUBLANES * (i + 1),
                                                                c * LANES:(c + 1) * LANES]
    for c in range(n_slabs):
        o_ref[:, c * LANES:(c + 1) * LANES] += slab_scr[c, CONV_HALO:CONV_HALO + tm, :]
    if final_norm:
        o_ref[...] = _rms(o_ref[...], fg_ref[...])


def _ffn(x, ya, ym, w_out, g, w_up, cwg, cwu, cbg, cbu, wd, fg, layer, seq, tm, final_norm):
    t = x.shape[0]
    tiles_per_seq = seq // tm
    hb = tm // HALO
    nh = t // HALO

    def main(w):
        return pl.BlockSpec((tm, w), lambda i: (i, 0))

    def prev(w):
        return pl.BlockSpec((HALO, w), lambda i: (jnp.maximum(i * hb - 1, 0), 0))

    def nxt(w):
        return pl.BlockSpec((HALO, w), lambda i: (jnp.minimum((i + 1) * hb, nh - 1), 0))

    def resident(shape):
        nd = len(shape)
        return pl.BlockSpec((None,) + shape, lambda i: (layer,) + (0,) * nd,
                            pipeline_mode=pl.Buffered(1))

    e_rows = tm + 2 * CONV_HALO
    assert e_rows % SUBLANES == 0 and (e_rows // SUBLANES) % 8 != 0
    return pl.pallas_call(
        functools.partial(_ffn_kernel, tm=tm, tiles_per_seq=tiles_per_seq, final_norm=final_norm),
        grid=(t // tm,),
        in_specs=[main(D_MODEL), prev(D_MODEL), nxt(D_MODEL),
                  main(ATTN_W), prev(ATTN_W), nxt(ATTN_W),
                  main(SC_W + SSD_W), prev(SC_W + SSD_W), nxt(SC_W + SSD_W),
                  resident((D_MODEL, D_MODEL)),
                  resident((1, D_MODEL)),
                  resident((D_MODEL, 2 * D_FF)),
                  resident((N_FF_CHUNKS, 3, FF_CHUNK)),
                  resident((N_FF_CHUNKS, 3, FF_CHUNK)),
                  resident((N_FF_CHUNKS, 1, FF_CHUNK)),
                  resident((N_FF_CHUNKS, 1, FF_CHUNK)),
                  resident((D_FF, D_MODEL)),
                  pl.BlockSpec((1, D_MODEL), lambda i: (0, 0))],
        out_specs=main(D_MODEL),
        out_shape=jax.ShapeDtypeStruct((t, D_MODEL), F32),
        scratch_shapes=[pltpu.VMEM((D_MODEL // LANES, e_rows, LANES), F32),
                        pltpu.VMEM((e_rows, D_MODEL), BF16),
                        pltpu.VMEM((e_rows, D_FF), BF16),
                        pltpu.VMEM((2, e_rows, FF_CHUNK), F32),
                        pltpu.VMEM((2, e_rows, FF_CHUNK), F32)],
        compiler_params=pltpu.CompilerParams(dimension_semantics=("arbitrary",),
                                             vmem_limit_bytes=VMEM_LIMIT),
        name="ffn",
    )(x, x, x, ya, ya, ya, ym, ym, ym, w_out, g, w_up, cwg, cwu, cbg, cbu, wd, fg)


def kernel(x, positions, norm_mix_g, w_in, lam_q1, lam_k1, lam_q2, lam_k2, subln_g, sc_conv_w, ssd_conv_w,
           ssd_conv_b, ssd_dt_bias, ssd_a_log, ssd_d, ssd_norm_g, w_out, norm_ffn_g, w_up, ffn_conv_w,
           ffn_conv_b, w_down, final_norm_g):
    batch, seq, _ = x.shape
    depth = w_in.shape[0]
    t = batch * seq
    tm = min(512, seq)
    tq = min(1024, seq)
    ssd_rows = min(1024, seq)

    w_in_b = jnp.pad(w_in, ((0, 0), (0, 0), (0, IN_PAD - IN_COLS))).astype(BF16)
    w_out_b = w_out.astype(BF16)
    w_up_b = w_up.astype(BF16)
    wd = w_down.astype(BF16)
    fcw = ffn_conv_w.reshape(depth, 3, 2, N_FF_CHUNKS, FF_CHUNK)
    cwg = fcw[:, :, 0].transpose(0, 2, 1, 3)
    cwu = fcw[:, :, 1].transpose(0, 2, 1, 3)
    fcb = ffn_conv_b.reshape(depth, 2, N_FF_CHUNKS, 1, FF_CHUNK)
    cbg, cbu = fcb[:, 0], fcb[:, 1]
    row3 = lambda a: a.reshape(depth, 1, -1)
    dtb = ssd_dt_bias.reshape(depth, N_SERIES, 1)
    alog = ssd_a_log.reshape(depth, N_SERIES, 1)
    sel = _select_matrix()
    dexp = jnp.repeat(ssd_d, SSD_W // SSD_HEADS, axis=-1).reshape(depth, 1, SSD_W)
    fg = final_norm_g.reshape(1, D_MODEL)

    rope = _rope_tables(positions)
    xf = x.reshape(t, D_MODEL)
    for l in range(depth):
        lam_init = 0.8 - 0.6 * math.exp(-0.3 * l)
        q, k, vt, rest = _in_proj(xf, row3(norm_mix_g), w_in_b, rope, l, tm)
        ya = _attention(q, k, vt, row3(lam_q1), row3(lam_k1), row3(lam_q2), row3(lam_k2),
                        subln_g.reshape(depth, V_DIM, 1), l, lam_init, batch, seq, tq)
        ym = _ssd_mixers(rest, dtb, alog, dexp, row3(ssd_norm_g), sc_conv_w, ssd_conv_w, row3(ssd_conv_b), sel,
                         l, batch, seq, ssd_rows)
        xf = _ffn(xf, ya, ym, w_out_b, row3(norm_ffn_g), w_up_b, cwg, cwu, cbg, cbu, wd, fg,
                  l, seq, tm, l == depth - 1)
    return xf.reshape(batch, seq, D_MODEL)
```

```python
import functools
import math

import jax
import jax.numpy as jnp
from jax import lax
from jax.experimental import pallas as pl
from jax.experimental.pallas import tpu as pltpu

F32 = jnp.float32
BF16 = jnp.bfloat16

D_MODEL = 1024
EPS = 1e-5
N_ATTN_HEADS = 4
QK_DIM = 64
V_DIM = 128
ATTN_W = N_ATTN_HEADS * V_DIM
ROPE_THETA = 500000.0
ROT_DIM = QK_DIM // 4
Q_SCALE = QK_DIM ** -0.5 * math.log2(math.e)
SC_W = 256
SSD_W = 256
SSD_STATE = 128
SSD_GROUPS = 2
SSD_HEADS = 4
SSD_CHUNK = 128
SSD_XBC = SSD_W + 2 * SSD_GROUPS * SSD_STATE
D_FF = 2816
IN_COLS = 3336

QKV_COLS = 3 * ATTN_W
REST_REAL = IN_COLS - QKV_COLS
REST_W = 1920
IN_PAD = QKV_COLS + REST_W
R_SCB, R_SCC, R_SCH, R_Z, R_XBC, R_DT = 0, 256, 512, 768, 1024, 1792

LANES = 128
SUBLANES = 8
BF16_SUBLANES = 16
VMEM_LIMIT = 56 * 1024 * 1024

FF_CHUNK = 256
N_FF_CHUNKS = D_FF // FF_CHUNK
HALO = BF16_SUBLANES
CONV_HALO = SUBLANES


def _sigmoid(x):
    return 1.0 / (1.0 + jnp.exp(-x))


def _rms(x, g):
    ms = jnp.mean(x * x, axis=-1, keepdims=True)
    return x * lax.rsqrt(ms + EPS) * g


def _conv3_rows(ext, w, lo, n):
    tot = ext.shape[0]
    up = pltpu.roll(ext, 1, 0)
    dn = pltpu.roll(ext, tot - 1, 0)
    out = up[lo:lo + n] * w[0:1]
    out = out + ext[lo:lo + n] * w[1:2]
    out = out + dn[lo:lo + n] * w[2:3]
    return out


def _rope_kernel(pos_ref, invf_ref, o_ref):
    pos = pos_ref[...].astype(F32)
    ang = pos * invf_ref[...]
    lane = lax.broadcasted_iota(jnp.int32, ang.shape, 1) & (QK_DIM - 1)
    c = jnp.cos(ang)
    s = jnp.sin(ang)
    half = ROT_DIM // 2
    o_ref[:, 0:LANES] = c
    o_ref[:, LANES:2 * LANES] = jnp.where(lane < half, -s, 0.0)
    o_ref[:, 2 * LANES:3 * LANES] = jnp.where((lane >= half) & (lane < ROT_DIM), s, 0.0)


def _rope_tables(positions):
    t = positions.size
    tm = min(t, 2048)
    half = ROT_DIM // 2
    inv_freq = ROPE_THETA ** (-jnp.arange(0, ROT_DIM, 2, dtype=F32) / ROT_DIM)
    lane = jnp.arange(LANES) % QK_DIM
    invf = jnp.where(lane < ROT_DIM, inv_freq[lane % half], 0.0).astype(F32)[None, :]
    return pl.pallas_call(
        _rope_kernel,
        grid=(t // tm,),
        in_specs=[pl.BlockSpec((tm, 1), lambda i: (i, 0)),
                  pl.BlockSpec((1, LANES), lambda i: (0, 0))],
        out_specs=pl.BlockSpec((tm, 3 * LANES), lambda i: (i, 0)),
        out_shape=jax.ShapeDtypeStruct((t, 3 * LANES), F32),
        name="rope_tables",
    )(positions.reshape(t, 1), invf)


def _inproj_kernel(x_ref, g_ref, w_ref, rope_ref, q_ref, k_ref, v_ref, r_ref, h_scr):
    h_scr[...] = _rms(x_ref[...], g_ref[...]).astype(BF16)
    c = rope_ref[:, 0:LANES]
    s1 = rope_ref[:, LANES:2 * LANES]
    s2 = rope_ref[:, 2 * LANES:3 * LANES]
    half = ROT_DIM // 2

    def rot(t):
        return t * c + pltpu.roll(t, LANES - half, 1) * s1 + pltpu.roll(t, half, 1) * s2

    cw = 2 * LANES
    for ci in range(2 * ATTN_W // cw):
        r = jnp.dot(h_scr[...], w_ref[:, ci * cw:(ci + 1) * cw], preferred_element_type=F32)
        for hf in range(2):
            t = rot(r[:, hf * LANES:(hf + 1) * LANES])
            col = ci * cw + hf * LANES
            if col < ATTN_W:
                q_ref[:, col:col + LANES] = (t * Q_SCALE).astype(BF16)
            else:
                k_ref[:, col - ATTN_W:col - ATTN_W + LANES] = t.astype(BF16)
    for ci in range(ATTN_W // cw):
        c0 = 2 * ATTN_W + ci * cw
        r = jnp.dot(h_scr[...], w_ref[:, c0:c0 + cw], preferred_element_type=F32)
        v_ref[ci * cw:(ci + 1) * cw, :] = r.T.astype(BF16)
    c0 = 0
    while c0 < REST_W:
        w = min(cw, REST_W - c0)
        r_ref[:, c0:c0 + w] = jnp.dot(h_scr[...], w_ref[:, QKV_COLS + c0:QKV_COLS + c0 + w],
                                      preferred_element_type=F32)
        c0 += w


def _in_proj(x, g, w_in, rope, layer, tm):
    t = x.shape[0]
    return pl.pallas_call(
        _inproj_kernel,
        grid=(t // tm,),
        in_specs=[pl.BlockSpec((tm, D_MODEL), lambda i: (i, 0)),
                  pl.BlockSpec((None, 1, D_MODEL), lambda i: (layer, 0, 0)),
                  pl.BlockSpec((None, D_MODEL, IN_PAD), lambda i: (layer, 0, 0)),
                  pl.BlockSpec((tm, 3 * LANES), lambda i: (i, 0))],
        out_specs=[pl.BlockSpec((tm, ATTN_W), lambda i: (i, 0)),
                   pl.BlockSpec((tm, ATTN_W), lambda i: (i, 0)),
                   pl.BlockSpec((None, ATTN_W, tm), lambda i: (i, 0, 0)),
                   pl.BlockSpec((tm, REST_W), lambda i: (i, 0))],
        out_shape=[jax.ShapeDtypeStruct((t, ATTN_W), BF16),
                   jax.ShapeDtypeStruct((t, ATTN_W), BF16),
                   jax.ShapeDtypeStruct((t // tm, ATTN_W, tm), BF16),
                   jax.ShapeDtypeStruct((t, REST_W), F32)],
        scratch_shapes=[pltpu.VMEM((tm, D_MODEL), BF16)],
        compiler_params=pltpu.CompilerParams(dimension_semantics=("arbitrary",),
                                             vmem_limit_bytes=VMEM_LIMIT),
        name="in_proj",
    )(x, g, w_in, rope)


def _attn_kernel(lq1_ref, lk1_ref, lq2_ref, lk2_ref, sg_ref, q_ref, k_ref, vt_ref, o_ref,
                 qp_scr, acc_scr, *, lam_init, tq):
    seq = k_ref.shape[0]
    nk, _, tk = vt_ref.shape
    lam = (jnp.exp(jnp.sum(lq1_ref[...] * lk1_ref[...], axis=-1, keepdims=True))
           - jnp.exp(jnp.sum(lq2_ref[...] * lk2_ref[...], axis=-1, keepdims=True)) + lam_init)
    lane = lax.broadcasted_iota(jnp.int32, (tq, V_DIM), 1)
    nt = (((1,), (1,)), ((), ()))

    def q_tile(qi, carry):
        q0 = pl.multiple_of(qi * tq, tq)
        q = q_ref[pl.ds(q0, tq), :]
        zero = jnp.zeros_like(q)
        qp_scr[0:tq, :] = jnp.where(lane < QK_DIM, q, zero)
        qp_scr[tq:2 * tq, :] = jnp.where(lane >= QK_DIM, q, zero)
        acc_scr[...] = jnp.zeros(acc_scr.shape, F32)

        def scores(j):
            kb = k_ref[j * tk:(j + 1) * tk, :]
            sts = [lax.dot_general(kb, qp_scr[c * tq:(c + 1) * tq, :], nt, preferred_element_type=F32) for c in range(2)]
            return sts, [jnp.max(st, axis=0, keepdims=True) for st in sts]

        m_run = [jnp.full((1, tq), -jnp.inf, F32) for _ in range(2)]
        ones = jnp.ones((BF16_SUBLANES, tk), BF16)
        sts, cmax = scores(0)
        for j in range(nk):
            nxt = scores(j + 1) if j + 1 < nk else None
            vt1 = jnp.concatenate([vt_ref[j], ones], axis=0)
            for c in range(2):
                cols = slice(c * tq, (c + 1) * tq)
                m_new = jnp.maximum(m_run[c], cmax[c])
                alpha = jnp.exp2(m_run[c] - m_new)
                p = jnp.exp2(sts[c] - m_new).astype(BF16)
                acc_scr[:, cols] = alpha * acc_scr[:, cols] + jnp.dot(vt1, p, preferred_element_type=F32)
                m_run[c] = m_new
            if nxt is not None:
                sts, cmax = nxt
        o = acc_scr[0:V_DIM, :] / acc_scr[V_DIM:V_DIM + 1, :]
        o = o[:, 0:tq] - lam * o[:, tq:2 * tq]
        ms = jnp.mean(o * o, axis=0, keepdims=True)
        y = o * lax.rsqrt(ms + EPS) * sg_ref[...] * (1.0 - lam_init)
        o_ref[pl.ds(q0, tq), :] = y.T.astype(BF16)
        return carry

    lax.fori_loop(0, seq // tq, q_tile, 0)


def _attention(q, k, vt, lq1, lk1, lq2, lk2, subg_col, layer, lam_init, batch, seq, tq):
    t = q.shape[0]
    tk = vt.shape[2]
    small = lambda n: pl.BlockSpec((None, 1, n), lambda bi, hi: (layer, 0, 0))
    head = pl.BlockSpec((seq, V_DIM), lambda bi, hi: (bi, hi))
    return pl.pallas_call(
        functools.partial(_attn_kernel, lam_init=lam_init, tq=tq),
        grid=(batch, N_ATTN_HEADS),
        in_specs=[small(QK_DIM), small(QK_DIM), small(QK_DIM), small(QK_DIM),
                  pl.BlockSpec((None, V_DIM, 1), lambda bi, hi: (layer, 0, 0)),
                  head, head,
                  pl.BlockSpec((seq // tk, V_DIM, tk), lambda bi, hi: (bi, hi, 0))],
        out_specs=head,
        out_shape=jax.ShapeDtypeStruct((t, ATTN_W), BF16),
        scratch_shapes=[pltpu.VMEM((2 * tq, V_DIM), BF16),
                        pltpu.VMEM((V_DIM + BF16_SUBLANES, 2 * tq), F32)],
        compiler_params=pltpu.CompilerParams(dimension_semantics=("arbitrary", "arbitrary"),
                                             vmem_limit_bytes=VMEM_LIMIT),
        name="diff_attn",
    )(lq1, lk1, lq2, lk2, subg_col, q, k, vt)


N_SERIES = 2 * SSD_HEADS
N_PICK = 2 * SSD_GROUPS
SEL_COLS = (N_SERIES + N_PICK) * LANES


def _select_matrix():
    lane = jnp.arange(SEL_COLS)
    blk, within = lane // LANES, lane % LANES
    d, g = (blk - N_SERIES) // SSD_GROUPS, (blk - N_SERIES) % SSD_GROUPS
    src = jnp.where(blk < N_SERIES, blk, SSD_HEADS * d + 2 * g + (within >= LANES // 2))
    row = jnp.arange(LANES)[:, None]
    return ((row < 3 * N_SERIES) & (row % N_SERIES == src[None, :])).astype(BF16)


def _split3(x):
    hi = x.astype(BF16)
    r1 = x - hi.astype(F32)
    mid = r1.astype(BF16)
    lo = (r1 - mid.astype(F32)).astype(BF16)
    return hi, mid, lo


def _ssd_kernel(dtb_ref, alog_ref, dexp_ref, ng_ref, scw_ref, cw_ref, cb_ref, sel_ref, main_ref, prev_ref, next_ref,
                o_ref, xbc_scr, bt_scr, dt_scr, hb_scr, h_scr, *, rows, nblk):
    sweep = pl.program_id(1)
    i = pl.program_id(2)
    blk = jnp.where(sweep == 0, nblk - 1 - i, i)
    has_prev = blk > 0
    has_next = blk < nblk - 1
    L = SSD_CHUNK
    nsub = rows // L
    half = LANES // 2

    @pl.when(i == 0)
    def _():
        h_scr[...] = jnp.zeros(h_scr.shape, F32)

    def ext_cols(c0, c1):
        pv = jnp.where(has_prev, prev_ref[:, c0:c1], 0.0)
        nx = jnp.where(has_next, next_ref[:, c0:c1], 0.0)
        return jnp.concatenate([pv, main_ref[:, c0:c1], nx], axis=0)

    rowi = lax.broadcasted_iota(jnp.int32, (L, L), 0)
    coli = lax.broadcasted_iota(jnp.int32, (L, L), 1)
    upper = (rowi <= coli).astype(BF16)
    lower = (rowi >= coli).astype(BF16)
    lo = lax.broadcasted_iota(jnp.int32, (L, LANES), 1) < half
    fwd_rows = lax.broadcasted_iota(jnp.int32, (N_SERIES, 1), 0) < SSD_HEADS
    neg_a = -jnp.exp(alog_ref[...])

    def series(dtt):
        hi, mid, low = _split3(dtt * neg_a)
        pre = sum(jnp.dot(t, upper, preferred_element_type=F32) for t in (hi, mid, low))
        suf = sum(jnp.dot(t, lower, preferred_element_type=F32) for t in (hi, mid, low))
        cst = jnp.where(fwd_rows, pre, suf)
        tot = jnp.where(fwd_rows, cst[:, L - 1:L], cst[:, 0:1])
        return cst, tot

    def chunk_state(bt, xs_g, w, j0):
        return (jnp.dot((bt * w[j0:j0 + 1, :]).astype(BF16), jnp.where(lo, xs_g, 0.0).astype(BF16),
                        preferred_element_type=F32)
                + jnp.dot((bt * w[j0 + 1:j0 + 2, :]).astype(BF16), jnp.where(lo, 0.0, xs_g).astype(BF16),
                          preferred_element_type=F32))

    def pick2(col, j0):
        return jnp.where(lo[0:1], col[j0:j0 + 1, :], col[j0 + 1:j0 + 2, :])

    def chunk_series(dtt_all):
        out = []
        for si in range(nsub):
            dtt = dtt_all[:, si * L:(si + 1) * L]
            cst, tot = series(dtt)
            out.append((dtt, cst, dtt * jnp.exp(tot - cst), jnp.exp(tot)))
        return out

    @pl.when(sweep == 0)
    def _():
        xbc = _conv3_rows(ext_cols(R_XBC, R_XBC + SSD_XBC), cw_ref[...], SUBLANES, rows) + cb_ref[...]
        xbc = xbc * _sigmoid(xbc)
        xbc_scr[blk] = xbc
        xdt = main_ref[:, R_DT:R_DT + LANES].T[0:N_SERIES, :] + dtb_ref[...]
        dtt_all = jnp.maximum(xdt, 0.0) + jnp.log1p(jnp.exp(-jnp.abs(xdt)))
        dt_scr[blk] = dtt_all
        ser = chunk_series(dtt_all)
        sts = {}
        for si in range(nsub):
            r0 = si * L
            for g in range(SSD_GROUPS):
                bt = xbc[r0:r0 + L, SSD_W + g * SSD_STATE:SSD_W + (g + 1) * SSD_STATE].T
                bt_scr[blk, g, :, r0:r0 + L] = bt
                sts[si, g] = chunk_state(bt, xbc[r0:r0 + L, g * LANES:(g + 1) * LANES], ser[si][2],
                                         SSD_HEADS + 2 * g)
        for g in range(SSD_GROUPS):
            hg = h_scr[g]
            for si in reversed(range(nsub)):
                hb_scr[blk * nsub + si, g] = hg
                hg = hg * pick2(ser[si][3], SSD_HEADS + 2 * g) + sts[si, g]
            h_scr[g] = hg

    @pl.when(sweep == 1)
    def _():
        u = ext_cols(R_SCC, R_SCC + SC_W) * ext_cols(R_SCH, R_SCH + SC_W)
        yc = main_ref[:, R_SCB:R_SCB + SC_W] * _conv3_rows(u, scw_ref[...], SUBLANES, rows)
        o_ref[:, 0:SC_W] = yc.astype(BF16)
        zpad = jnp.zeros((LANES - 3 * N_SERIES, L), F32)
        masks = (coli <= rowi, coli >= rowi)
        ser = chunk_series(dt_scr[blk])
        bcs = []
        for si in range(nsub):
            hi, mid, low = _split3(ser[si][1])
            cs_col = jnp.concatenate([hi.astype(F32), mid.astype(F32), low.astype(F32), zpad], axis=0).T
            bcs.append(jnp.dot(cs_col.astype(BF16), sel_ref[...], preferred_element_type=F32))
        pairs = [(si, g) for si in range(nsub) for g in range(SSD_GROUPS)]
        xs, cbf, bts, gms, sts, ys = {}, {}, {}, {}, {}, {}
        for si, g in pairs:
            r0 = si * L
            xs[si, g] = xbc_scr[blk, r0:r0 + L, g * LANES:(g + 1) * LANES]
            cbf[si, g] = xbc_scr[blk, r0:r0 + L, SSD_W + (SSD_GROUPS + g) * SSD_STATE:
                                 SSD_W + (SSD_GROUPS + g + 1) * SSD_STATE].astype(BF16)
            bts[si, g] = bt_scr[blk, g, :, r0:r0 + L]
            gms[si, g] = jnp.dot(cbf[si, g], bts[si, g].astype(BF16), preferred_element_type=F32)
        for si, g in pairs:
            sts[si, g] = chunk_state(bts[si, g], xs[si, g], ser[si][2], 2 * g)
        for si, g in pairs:
            dtt, cst = ser[si][0], ser[si][1]
            xs_lo = jnp.where(lo, xs[si, g], 0.0).astype(BF16)
            xs_hi = jnp.where(lo, 0.0, xs[si, g]).astype(BF16)
            ms = []
            for h in range(2):
                tot_decay = None
                for d in range(2):
                    j = SSD_HEADS * d + 2 * g + h
                    seg = bcs[si][:, j * LANES:(j + 1) * LANES] - cst[j:j + 1, :]
                    term = jnp.where(masks[d], jnp.exp(seg), 0.0) * dtt[j:j + 1, :]
                    tot_decay = term if tot_decay is None else tot_decay + term
                ms.append((gms[si, g] * tot_decay).astype(BF16))
            ys[si, g] = jnp.dot(jnp.concatenate(ms, axis=1), jnp.concatenate([xs_lo, xs_hi], axis=0),
                                preferred_element_type=F32)
        for g in range(SSD_GROUPS):
            hg = h_scr[g]
            for si in range(nsub):
                r0 = si * L
                hb = hb_scr[blk * nsub + si, g]
                ch = jnp.dot(cbf[si, g], jnp.concatenate([hg.astype(BF16), hb.astype(BF16)], axis=1),
                             preferred_element_type=F32)
                hg = hg * pick2(ser[si][3], 2 * g) + sts[si, g]
                pf = (N_SERIES + g) * LANES
                pb = (N_SERIES + SSD_GROUPS + g) * LANES
                y = (ys[si, g] + jnp.exp(bcs[si][:, pf:pf + LANES]) * ch[:, 0:LANES]
                     + jnp.exp(bcs[si][:, pb:pb + LANES]) * ch[:, LANES:])
                gl = slice(g * LANES, (g + 1) * LANES)
                y = y + dexp_ref[:, gl] * xs[si, g]
                z = main_ref[r0:r0 + L, R_Z + g * LANES:R_Z + (g + 1) * LANES]
                y = y * (z * _sigmoid(z))
                o_ref[r0:r0 + L, SC_W + g * LANES:SC_W + (g + 1) * LANES] = _rms(y, ng_ref[:, gl]).astype(BF16)
            h_scr[g] = hg


def _ssd_mixers(rest, dtb_col, alog_col, dexp, ng, scw, cw, cb, sel, layer, batch, seq, rows):
    t = rest.shape[0]
    nblk = seq // rows
    r8 = rows // SUBLANES

    def blk_of(s, i):
        return jnp.where(s == 0, nblk - 1 - i, i)

    small = lambda r, n: pl.BlockSpec((None, r, n), lambda bi, s, i: (layer, 0, 0))
    return pl.pallas_call(
        functools.partial(_ssd_kernel, rows=rows, nblk=nblk),
        grid=(batch, 2, nblk),
        in_specs=[small(N_SERIES, 1), small(N_SERIES, 1), small(1, SSD_W), small(1, SSD_W),
                  small(3, SC_W), small(3, SSD_XBC), small(1, SSD_XBC),
                  pl.BlockSpec((LANES, SEL_COLS), lambda bi, s, i: (0, 0)),
                  pl.BlockSpec((rows, REST_W), lambda bi, s, i: (bi * nblk + blk_of(s, i), 0)),
                  pl.BlockSpec((SUBLANES, REST_W),
                               lambda bi, s, i: (jnp.maximum((bi * nblk + blk_of(s, i)) * r8 - 1, 0), 0)),
                  pl.BlockSpec((SUBLANES, REST_W),
                               lambda bi, s, i: (jnp.minimum((bi * nblk + blk_of(s, i) + 1) * r8,
                                                             t // SUBLANES - 1), 0))],
        out_specs=pl.BlockSpec((rows, SC_W + SSD_W), lambda bi, s, i: (bi * nblk + s * i, 0)),
        out_shape=jax.ShapeDtypeStruct((t, SC_W + SSD_W), BF16),
        scratch_shapes=[pltpu.VMEM((nblk, rows, SSD_XBC), F32),
                        pltpu.VMEM((nblk, SSD_GROUPS, SSD_STATE, rows), F32),
                        pltpu.VMEM((nblk, N_SERIES, rows), F32),
                        pltpu.VMEM((seq // SSD_CHUNK, SSD_GROUPS, SSD_STATE, LANES), F32),
                        pltpu.VMEM((SSD_GROUPS, SSD_STATE, LANES), F32)],
        compiler_params=pltpu.CompilerParams(dimension_semantics=("arbitrary", "arbitrary", "arbitrary"),
                                             vmem_limit_bytes=VMEM_LIMIT),
        name="ssd_mixers",
    )(dtb_col, alog_col, dexp, ng, scw, cw, cb, sel, rest, rest, rest)


def _ffn_kernel(x_ref, xp_ref, xn_ref, ya_ref, yap_ref, yan_ref, ym_ref, ymp_ref, ymn_ref,
                wout_ref, g_ref, wup_ref, cwg_ref, cwu_ref, cbg_ref, cbu_ref, wd_ref, fg_ref,
                o_ref, slab_scr, hp_scr, act_scr, ua_scr, ub_scr, *, tm, tiles_per_seq, final_norm):
    ti = pl.program_id(0) % tiles_per_seq
    has_prev = ti > 0
    has_next = ti < tiles_per_seq - 1
    e_rows = tm + 2 * CONV_HALO
    seg = e_rows // SUBLANES
    n_slabs = D_MODEL // LANES

    def ext(prev, main, nxt):
        return jnp.concatenate([prev[...], main[...], nxt[...]], axis=0)

    y_all = jnp.concatenate([ext(yap_ref, ya_ref, yan_ref), ext(ymp_ref, ym_ref, ymn_ref)], axis=1)
    mix = jnp.dot(y_all, wout_ref[...], preferred_element_type=F32)
    xnew = (ext(xp_ref, x_ref, xn_ref) + mix)[HALO - CONV_HALO:HALO - CONV_HALO + e_rows]
    row = lax.broadcasted_iota(jnp.int32, (e_rows, 1), 0)
    valid = (row >= jnp.where(has_prev, 0, CONV_HALO)) & (row < jnp.where(has_next, e_rows, CONV_HALO + tm))
    h = jnp.where(valid, _rms(xnew, g_ref[...]), 0.0)
    o_ref[...] = xnew[CONV_HALO:CONV_HALO + tm]

    for c in range(n_slabs):
        slab_scr[c] = h[:, c * LANES:(c + 1) * LANES]
    for i in range(0, seg, 2):
        blk = [jnp.concatenate([slab_scr[c, pl.ds(i + d, SUBLANES, stride=seg), :] for c in range(n_slabs)], axis=1)
               for d in range(2)]
        hp_scr[SUBLANES * i:SUBLANES * (i + 2), :] = jnp.concatenate(blk, axis=0).astype(BF16)

    def up_proj(j, u_ref):
        hb = hp_scr[...]
        c0 = j * FF_CHUNK
        u_ref[0] = jnp.dot(hb, wup_ref[:, c0:c0 + FF_CHUNK], preferred_element_type=F32)
        u_ref[1] = jnp.dot(hb, wup_ref[:, D_FF + c0:D_FF + c0 + FF_CHUNK], preferred_element_type=F32)

    def conv_rows(u_ref, idx, w):
        last = e_rows - SUBLANES
        mid = (u_ref[idx, 0:last - SUBLANES, :] * w[0:1] + u_ref[idx, SUBLANES:last, :] * w[1:2]
               + u_ref[idx, 2 * SUBLANES:e_rows, :] * w[2:3])
        first = (pltpu.roll(u_ref[idx, last:e_rows, :], 1, 0) * w[0:1] + u_ref[idx, 0:SUBLANES, :] * w[1:2]
                 + u_ref[idx, SUBLANES:2 * SUBLANES, :] * w[2:3])
        end = (u_ref[idx, last - SUBLANES:last, :] * w[0:1] + u_ref[idx, last:e_rows, :] * w[1:2]
               + pltpu.roll(u_ref[idx, 0:SUBLANES, :], SUBLANES - 1, 0) * w[2:3])
        return jnp.concatenate([first, mid, end], axis=0)

    def gate_down(j, u_ref):
        cg = conv_rows(u_ref, 0, cwg_ref[j]) + cbg_ref[j]
        cu = conv_rows(u_ref, 1, cwu_ref[j]) + cbu_ref[j]
        act_scr[:, j * FF_CHUNK:(j + 1) * FF_CHUNK] = ((cg * _sigmoid(cg)) * cu).astype(BF16)

    bufs = (ua_scr, ub_scr)
    up_proj(0, bufs[0])
    for j in range(N_FF_CHUNKS):
        if j + 1 < N_FF_CHUNKS:
            up_proj(j + 1, bufs[(j + 1) % 2])
        gate_down(j, bufs[j % 2])
    down = jnp.dot(act_scr[...], wd_ref[...], preferred_element_type=F32)
    for i in range(seg):
        for c in range(n_slabs):
            slab_scr[c, pl.ds(i, SUBLANES, stride=seg), :] = down[SUBLANES * i:SUBLANES * (i + 1),
                                                                c * LANES:(c + 1) * LANES]
    for c in range(n_slabs):
        o_ref[:, c * LANES:(c + 1) * LANES] += slab_scr[c, CONV_HALO:CONV_HALO + tm, :]
    if final_norm:
        o_ref[...] = _rms(o_ref[...], fg_ref[...])


def _ffn(x, ya, ym, w_out, g, w_up, cwg, cwu, cbg, cbu, wd, fg, layer, seq, tm, final_norm):
    t = x.shape[0]
    tiles_per_seq = seq // tm
    hb = tm // HALO
    nh = t // HALO

    def main(w):
        return pl.BlockSpec((tm, w), lambda i: (i, 0))

    def prev(w):
        return pl.BlockSpec((HALO, w), lambda i: (jnp.maximum(i * hb - 1, 0), 0))

    def nxt(w):
        return pl.BlockSpec((HALO, w), lambda i: (jnp.minimum((i + 1) * hb, nh - 1), 0))

    def resident(shape):
        nd = len(shape)
        return pl.BlockSpec((None,) + shape, lambda i: (layer,) + (0,) * nd,
                            pipeline_mode=pl.Buffered(1))

    e_rows = tm + 2 * CONV_HALO
    assert e_rows % SUBLANES == 0 and (e_rows // SUBLANES) % 8 != 0
    return pl.pallas_call(
        functools.partial(_ffn_kernel, tm=tm, tiles_per_seq=tiles_per_seq, final_norm=final_norm),
        grid=(t // tm,),
        in_specs=[main(D_MODEL), prev(D_MODEL), nxt(D_MODEL),
                  main(ATTN_W), prev(ATTN_W), nxt(ATTN_W),
                  main(SC_W + SSD_W), prev(SC_W + SSD_W), nxt(SC_W + SSD_W),
                  resident((D_MODEL, D_MODEL)),
                  resident((1, D_MODEL)),
                  resident((D_MODEL, 2 * D_FF)),
                  resident((N_FF_CHUNKS, 3, FF_CHUNK)),
                  resident((N_FF_CHUNKS, 3, FF_CHUNK)),
                  resident((N_FF_CHUNKS, 1, FF_CHUNK)),
                  resident((N_FF_CHUNKS, 1, FF_CHUNK)),
                  resident((D_FF, D_MODEL)),
                  pl.BlockSpec((1, D_MODEL), lambda i: (0, 0))],
        out_specs=main(D_MODEL),
        out_shape=jax.ShapeDtypeStruct((t, D_MODEL), F32),
        scratch_shapes=[pltpu.VMEM((D_MODEL // LANES, e_rows, LANES), F32),
                        pltpu.VMEM((e_rows, D_MODEL), BF16),
                        pltpu.VMEM((e_rows, D_FF), BF16),
                        pltpu.VMEM((2, e_rows, FF_CHUNK), F32),
                        pltpu.VMEM((2, e_rows, FF_CHUNK), F32)],
        compiler_params=pltpu.CompilerParams(dimension_semantics=("arbitrary",),
                                             vmem_limit_bytes=VMEM_LIMIT),
        name="ffn",
    )(x, x, x, ya, ya, ya, ym, ym, ym, w_out, g, w_up, cwg, cwu, cbg, cbu, wd, fg)


def kernel(x, positions, norm_mix_g, w_in, lam_q1, lam_k1, lam_q2, lam_k2, subln_g, sc_conv_w, ssd_conv_w,
           ssd_conv_b, ssd_dt_bias, ssd_a_log, ssd_d, ssd_norm_g, w_out, norm_ffn_g, w_up, ffn_conv_w,
           ffn_conv_b, w_down, final_norm_g):
    batch, seq, _ = x.shape
    depth = w_in.shape[0]
    t = batch * seq
    tm = min(512, seq)
    tq = min(1024, seq)
    ssd_rows = min(1024, seq)

    w_in_b = jnp.pad(w_in.astype(BF16), ((0, 0), (0, 0), (0, IN_PAD - IN_COLS)))
    w_out_b = w_out.astype(BF16)
    w_up_b = w_up.astype(BF16)
    wd = w_down.astype(BF16)
    fcw = ffn_conv_w.reshape(depth, 3, 2, N_FF_CHUNKS, FF_CHUNK)
    cwg = fcw[:, :, 0].transpose(0, 2, 1, 3)
    cwu = fcw[:, :, 1].transpose(0, 2, 1, 3)
    fcb = ffn_conv_b.reshape(depth, 2, N_FF_CHUNKS, 1, FF_CHUNK)
    cbg, cbu = fcb[:, 0], fcb[:, 1]
    row3 = lambda a: a.reshape(depth, 1, -1)
    dtb = ssd_dt_bias.reshape(depth, N_SERIES, 1)
    alog = ssd_a_log.reshape(depth, N_SERIES, 1)
    sel = _select_matrix()
    dexp = jnp.repeat(ssd_d, SSD_W // SSD_HEADS, axis=-1).reshape(depth, 1, SSD_W)
    fg = final_norm_g.reshape(1, D_MODEL)

    rope = _rope_tables(positions)
    xf = x.reshape(t, D_MODEL)
    for l in range(depth):
        lam_init = 0.8 - 0.6 * math.exp(-0.3 * l)
        q, k, vt, rest = _in_proj(xf, row3(norm_mix_g), w_in_b, rope, l, tm)
        ya = _attention(q, k, vt, row3(lam_q1), row3(lam_k1), row3(lam_q2), row3(lam_k2),
                        subln_g.reshape(depth, V_DIM, 1), l, lam_init, batch, seq, tq)
        ym = _ssd_mixers(rest, dtb, alog, dexp, row3(ssd_norm_g), sc_conv_w, ssd_conv_w, row3(ssd_conv_b), sel,
                         l, batch, seq, ssd_rows)
        xf = _ffn(xf, ya, ym, w_out_b, row3(norm_ffn_g), w_up_b, cwg, cwu, cbg, cbu, wd, fg,
                  l, seq, tm, l == depth - 1)
    return xf.reshape(batch, seq, D_MODEL)
```

```python
import functools
import math

import jax
import jax.numpy as jnp
from jax import lax
from jax.experimental import pallas as pl
from jax.experimental.pallas import tpu as pltpu

F32 = jnp.float32
BF16 = jnp.bfloat16

D_MODEL = 1024
EPS = 1e-5
N_ATTN_HEADS = 4
QK_DIM = 64
V_DIM = 128
ATTN_W = N_ATTN_HEADS * V_DIM
ROPE_THETA = 500000.0
ROT_DIM = QK_DIM // 4
Q_SCALE = QK_DIM ** -0.5 * math.log2(math.e)
SC_W = 256
SSD_W = 256
SSD_STATE = 128
SSD_GROUPS = 2
SSD_HEADS = 4
SSD_CHUNK = 128
SSD_XBC = SSD_W + 2 * SSD_GROUPS * SSD_STATE
D_FF = 2816
IN_COLS = 3336

QKV_COLS = 3 * ATTN_W
REST_REAL = IN_COLS - QKV_COLS
REST_W = 1920
IN_PAD = QKV_COLS + REST_W
R_SCB, R_SCC, R_SCH, R_Z, R_XBC, R_DT = 0, 256, 512, 768, 1024, 1792

LANES = 128
SUBLANES = 8
BF16_SUBLANES = 16
VMEM_LIMIT = 56 * 1024 * 1024

FF_CHUNK = 256
N_FF_CHUNKS = D_FF // FF_CHUNK
HALO = BF16_SUBLANES
CONV_HALO = SUBLANES


def _sigmoid(x):
    return 1.0 / (1.0 + jnp.exp(-x))


def _rms(x, g):
    ms = jnp.mean(x * x, axis=-1, keepdims=True)
    return x * lax.rsqrt(ms + EPS) * g


def _conv3_rows(ext, w, lo, n):
    tot = ext.shape[0]
    up = pltpu.roll(ext, 1, 0)
    dn = pltpu.roll(ext, tot - 1, 0)
    out = up[lo:lo + n] * w[0:1]
    out = out + ext[lo:lo + n] * w[1:2]
    out = out + dn[lo:lo + n] * w[2:3]
    return out


def _rope_kernel(pos_ref, invf_ref, o_ref):
    pos = pos_ref[...].astype(F32)
    ang = pos * invf_ref[...]
    lane = lax.broadcasted_iota(jnp.int32, ang.shape, 1) & (QK_DIM - 1)
    c = jnp.cos(ang)
    s = jnp.sin(ang)
    half = ROT_DIM // 2
    o_ref[:, 0:LANES] = c
    o_ref[:, LANES:2 * LANES] = jnp.where(lane < half, -s, 0.0)
    o_ref[:, 2 * LANES:3 * LANES] = jnp.where((lane >= half) & (lane < ROT_DIM), s, 0.0)


def _rope_tables(positions):
    t = positions.size
    tm = min(t, 2048)
    half = ROT_DIM // 2
    inv_freq = ROPE_THETA ** (-jnp.arange(0, ROT_DIM, 2, dtype=F32) / ROT_DIM)
    lane = jnp.arange(LANES) % QK_DIM
    invf = jnp.where(lane < ROT_DIM, inv_freq[lane % half], 0.0).astype(F32)[None, :]
    return pl.pallas_call(
        _rope_kernel,
        grid=(t // tm,),
        in_specs=[pl.BlockSpec((tm, 1), lambda i: (i, 0)),
                  pl.BlockSpec((1, LANES), lambda i: (0, 0))],
        out_specs=pl.BlockSpec((tm, 3 * LANES), lambda i: (i, 0)),
        out_shape=jax.ShapeDtypeStruct((t, 3 * LANES), F32),
        name="rope_tables",
    )(positions.reshape(t, 1), invf)


def _inproj_kernel(x_ref, g_ref, w_ref, rope_ref, q_ref, k_ref, v_ref, r_ref, h_scr):
    h_scr[...] = _rms(x_ref[...], g_ref[...]).astype(BF16)
    c = rope_ref[:, 0:LANES]
    s1 = rope_ref[:, LANES:2 * LANES]
    s2 = rope_ref[:, 2 * LANES:3 * LANES]
    half = ROT_DIM // 2

    def rot(t):
        return t * c + pltpu.roll(t, LANES - half, 1) * s1 + pltpu.roll(t, half, 1) * s2

    cw = 2 * LANES
    for ci in range(2 * ATTN_W // cw):
        r = jnp.dot(h_scr[...], w_ref[:, ci * cw:(ci + 1) * cw], preferred_element_type=F32)
        for hf in range(2):
            t = rot(r[:, hf * LANES:(hf + 1) * LANES])
            col = ci * cw + hf * LANES
            if col < ATTN_W:
                q_ref[:, col:col + LANES] = (t * Q_SCALE).astype(BF16)
            else:
                k_ref[:, col - ATTN_W:col - ATTN_W + LANES] = t.astype(BF16)
    for ci in range(ATTN_W // cw):
        c0 = 2 * ATTN_W + ci * cw
        r = jnp.dot(h_scr[...], w_ref[:, c0:c0 + cw], preferred_element_type=F32)
        v_ref[ci * cw:(ci + 1) * cw, :] = r.T.astype(BF16)
    c0 = 0
    while c0 < REST_W:
        w = min(cw, REST_W - c0)
        r_ref[:, c0:c0 + w] = jnp.dot(h_scr[...], w_ref[:, QKV_COLS + c0:QKV_COLS + c0 + w],
                                      preferred_element_type=F32)
        c0 += w


def _in_proj(x, g, w_in, rope, layer, tm):
    t = x.shape[0]
    return pl.pallas_call(
        _inproj_kernel,
        grid=(t // tm,),
        in_specs=[pl.BlockSpec((tm, D_MODEL), lambda i: (i, 0)),
                  pl.BlockSpec((None, 1, D_MODEL), lambda i: (layer, 0, 0)),
                  pl.BlockSpec((None, D_MODEL, IN_PAD), lambda i: (layer, 0, 0)),
                  pl.BlockSpec((tm, 3 * LANES), lambda i: (i, 0))],
        out_specs=[pl.BlockSpec((tm, ATTN_W), lambda i: (i, 0)),
                   pl.BlockSpec((tm, ATTN_W), lambda i: (i, 0)),
                   pl.BlockSpec((None, ATTN_W, tm), lambda i: (i, 0, 0)),
                   pl.BlockSpec((tm, REST_W), lambda i: (i, 0))],
        out_shape=[jax.ShapeDtypeStruct((t, ATTN_W), BF16),
                   jax.ShapeDtypeStruct((t, ATTN_W), BF16),
                   jax.ShapeDtypeStruct((t // tm, ATTN_W, tm), BF16),
                   jax.ShapeDtypeStruct((t, REST_W), F32)],
        scratch_shapes=[pltpu.VMEM((tm, D_MODEL), BF16)],
        compiler_params=pltpu.CompilerParams(dimension_semantics=("arbitrary",),
                                             vmem_limit_bytes=VMEM_LIMIT),
        name="in_proj",
    )(x, g, w_in, rope)


def _attn_kernel(lq1_ref, lk1_ref, lq2_ref, lk2_ref, sg_ref, q_ref, k_ref, vt_ref, o_ref,
                 qp_scr, acc_scr, *, lam_init, tq):
    seq = k_ref.shape[0]
    nk, _, tk = vt_ref.shape
    lam = (jnp.exp(jnp.sum(lq1_ref[...] * lk1_ref[...], axis=-1, keepdims=True))
           - jnp.exp(jnp.sum(lq2_ref[...] * lk2_ref[...], axis=-1, keepdims=True)) + lam_init)
    lane = lax.broadcasted_iota(jnp.int32, (tq, V_DIM), 1)
    nt = (((1,), (1,)), ((), ()))

    def q_tile(qi, carry):
        q0 = pl.multiple_of(qi * tq, tq)
        q = q_ref[pl.ds(q0, tq), :]
        zero = jnp.zeros_like(q)
        qp_scr[0:tq, :] = jnp.where(lane < QK_DIM, q, zero)
        qp_scr[tq:2 * tq, :] = jnp.where(lane >= QK_DIM, q, zero)
        acc_scr[qi] = jnp.zeros(acc_scr.shape[1:], F32)

        def scores(j):
            kb = k_ref[j * tk:(j + 1) * tk, :]
            sts = [lax.dot_general(kb, qp_scr[c * tq:(c + 1) * tq, :], nt, preferred_element_type=F32) for c in range(2)]
            return sts, [jnp.max(st, axis=0, keepdims=True) for st in sts]

        m_run = [jnp.full((1, tq), -jnp.inf, F32) for _ in range(2)]
        ones = jnp.ones((BF16_SUBLANES, tk), BF16)
        sts, cmax = scores(0)
        for j in range(nk):
            nxt = scores(j + 1) if j + 1 < nk else None
            vt1 = jnp.concatenate([vt_ref[j], ones], axis=0)
            for c in range(2):
                cols = slice(c * tq, (c + 1) * tq)
                m_new = jnp.maximum(m_run[c], cmax[c])
                alpha = jnp.exp2(m_run[c] - m_new)
                p = jnp.exp2(sts[c] - m_new).astype(BF16)
                acc_scr[qi, :, cols] = alpha * acc_scr[qi, :, cols] + jnp.dot(vt1, p, preferred_element_type=F32)
                m_run[c] = m_new
            if nxt is not None:
                sts, cmax = nxt
        return carry

    lax.fori_loop(0, seq // tq, q_tile, 0)
    for ti in range(seq // tq):
        acc = acc_scr[ti]
        o = acc[0:V_DIM, :] * (1.0 / acc[V_DIM:V_DIM + 1, :])
        o = o[:, 0:tq] - lam * o[:, tq:2 * tq]
        ms = jnp.mean(o * o, axis=0, keepdims=True)
        y = o * lax.rsqrt(ms + EPS) * sg_ref[...] * (1.0 - lam_init)
        o_ref[ti * tq:(ti + 1) * tq, :] = y.T.astype(BF16)


def _attention(q, k, vt, lq1, lk1, lq2, lk2, subg_col, layer, lam_init, batch, seq, tq):
    t = q.shape[0]
    tk = vt.shape[2]
    small = lambda n: pl.BlockSpec((None, 1, n), lambda bi, hi: (layer, 0, 0))
    head = pl.BlockSpec((seq, V_DIM), lambda bi, hi: (bi, hi))
    return pl.pallas_call(
        functools.partial(_attn_kernel, lam_init=lam_init, tq=tq),
        grid=(batch, N_ATTN_HEADS),
        in_specs=[small(QK_DIM), small(QK_DIM), small(QK_DIM), small(QK_DIM),
                  pl.BlockSpec((None, V_DIM, 1), lambda bi, hi: (layer, 0, 0)),
                  head, head,
                  pl.BlockSpec((seq // tk, V_DIM, tk), lambda bi, hi: (bi, hi, 0))],
        out_specs=head,
        out_shape=jax.ShapeDtypeStruct((t, ATTN_W), BF16),
        scratch_shapes=[pltpu.VMEM((2 * tq, V_DIM), BF16),
                        pltpu.VMEM((seq // tq, V_DIM + BF16_SUBLANES, 2 * tq), F32)],
        compiler_params=pltpu.CompilerParams(dimension_semantics=("arbitrary", "arbitrary"),
                                             vmem_limit_bytes=VMEM_LIMIT),
        name="diff_attn",
    )(lq1, lk1, lq2, lk2, subg_col, q, k, vt)


N_SERIES = 2 * SSD_HEADS
N_PICK = 2 * SSD_GROUPS
SEL_COLS = (N_SERIES + N_PICK) * LANES


def _select_matrix():
    lane = jnp.arange(SEL_COLS)
    blk, within = lane // LANES, lane % LANES
    d, g = (blk - N_SERIES) // SSD_GROUPS, (blk - N_SERIES) % SSD_GROUPS
    src = jnp.where(blk < N_SERIES, blk, SSD_HEADS * d + 2 * g + (within >= LANES // 2))
    row = jnp.arange(LANES)[:, None]
    return ((row < 3 * N_SERIES) & (row % N_SERIES == src[None, :])).astype(BF16)


def _split3(x):
    hi = x.astype(BF16)
    r1 = x - hi.astype(F32)
    mid = r1.astype(BF16)
    lo = (r1 - mid.astype(F32)).astype(BF16)
    return hi, mid, lo


def _ssd_kernel(dtb_ref, alog_ref, dexp_ref, ng_ref, scw_ref, cw_ref, cb_ref, sel_ref, main_ref, prev_ref, next_ref,
                o_ref, xbc_scr, bt_scr, dt_scr, hb_scr, h_scr, *, rows, nblk):
    sweep = pl.program_id(1)
    i = pl.program_id(2)
    blk = jnp.where(sweep == 0, nblk - 1 - i, i)
    has_prev = blk > 0
    has_next = blk < nblk - 1
    L = SSD_CHUNK
    nsub = rows // L
    half = LANES // 2

    @pl.when(i == 0)
    def _():
        h_scr[...] = jnp.zeros(h_scr.shape, F32)

    def ext_cols(c0, c1):
        pv = jnp.where(has_prev, prev_ref[:, c0:c1], 0.0)
        nx = jnp.where(has_next, next_ref[:, c0:c1], 0.0)
        return jnp.concatenate([pv, main_ref[:, c0:c1], nx], axis=0)

    rowi = lax.broadcasted_iota(jnp.int32, (L, L), 0)
    coli = lax.broadcasted_iota(jnp.int32, (L, L), 1)
    upper = (rowi <= coli).astype(BF16)
    lower = (rowi >= coli).astype(BF16)
    lo = lax.broadcasted_iota(jnp.int32, (L, LANES), 1) < half
    fwd_rows = lax.broadcasted_iota(jnp.int32, (N_SERIES, 1), 0) < SSD_HEADS
    neg_a = -jnp.exp(alog_ref[...])

    def series(dtt):
        hi, mid, low = _split3(dtt * neg_a)
        pre = sum(jnp.dot(t, upper, preferred_element_type=F32) for t in (hi, mid, low))
        suf = sum(jnp.dot(t, lower, preferred_element_type=F32) for t in (hi, mid, low))
        cst = jnp.where(fwd_rows, pre, suf)
        tot = jnp.where(fwd_rows, cst[:, L - 1:L], cst[:, 0:1])
        return cst, tot

    def chunk_state(bt, xs_g, w, j0):
        return (jnp.dot((bt * w[j0:j0 + 1, :]).astype(BF16), jnp.where(lo, xs_g, 0.0).astype(BF16),
                        preferred_element_type=F32)
                + jnp.dot((bt * w[j0 + 1:j0 + 2, :]).astype(BF16), jnp.where(lo, 0.0, xs_g).astype(BF16),
                          preferred_element_type=F32))

    def pick2(col, j0):
        return jnp.where(lo[0:1], col[j0:j0 + 1, :], col[j0 + 1:j0 + 2, :])

    def chunk_series(dtt_all):
        out = []
        for si in range(nsub):
            dtt = dtt_all[:, si * L:(si + 1) * L]
            cst, tot = series(dtt)
            out.append((dtt, cst, dtt * jnp.exp(tot - cst), jnp.exp(tot)))
        return out

    @pl.when(sweep == 0)
    def _():
        xbc = _conv3_rows(ext_cols(R_XBC, R_XBC + SSD_XBC), cw_ref[...], SUBLANES, rows) + cb_ref[...]
        xbc = xbc * _sigmoid(xbc)
        xbc_scr[blk] = xbc
        xdt = main_ref[:, R_DT:R_DT + LANES].T[0:N_SERIES, :] + dtb_ref[...]
        dtt_all = jnp.maximum(xdt, 0.0) + jnp.log1p(jnp.exp(-jnp.abs(xdt)))
        dt_scr[blk] = dtt_all
        ser = chunk_series(dtt_all)
        sts = {}
        for si in range(nsub):
            r0 = si * L
            for g in range(SSD_GROUPS):
                bt = xbc[r0:r0 + L, SSD_W + g * SSD_STATE:SSD_W + (g + 1) * SSD_STATE].T
                bt_scr[blk, g, :, r0:r0 + L] = bt
                sts[si, g] = chunk_state(bt, xbc[r0:r0 + L, g * LANES:(g + 1) * LANES], ser[si][2],
                                         SSD_HEADS + 2 * g)
        for g in range(SSD_GROUPS):
            hg = h_scr[g]
            for si in reversed(range(nsub)):
                hb_scr[blk * nsub + si, g] = hg
                hg = hg * pick2(ser[si][3], SSD_HEADS + 2 * g) + sts[si, g]
            h_scr[g] = hg

    @pl.when(sweep == 1)
    def _():
        u = ext_cols(R_SCC, R_SCC + SC_W) * ext_cols(R_SCH, R_SCH + SC_W)
        yc = main_ref[:, R_SCB:R_SCB + SC_W] * _conv3_rows(u, scw_ref[...], SUBLANES, rows)
        o_ref[:, 0:SC_W] = yc.astype(BF16)
        zpad = jnp.zeros((LANES - 3 * N_SERIES, L), F32)
        masks = (coli <= rowi, coli >= rowi)
        ser = chunk_series(dt_scr[blk])
        bcs = []
        for si in range(nsub):
            hi, mid, low = _split3(ser[si][1])
            cs_col = jnp.concatenate([hi.astype(F32), mid.astype(F32), low.astype(F32), zpad], axis=0).T
            bcs.append(jnp.dot(cs_col.astype(BF16), sel_ref[...], preferred_element_type=F32))
        pairs = [(si, g) for si in range(nsub) for g in range(SSD_GROUPS)]
        xs, cbf, bts, gms, sts, ys = {}, {}, {}, {}, {}, {}
        for si, g in pairs:
            r0 = si * L
            xs[si, g] = xbc_scr[blk, r0:r0 + L, g * LANES:(g + 1) * LANES]
            cbf[si, g] = xbc_scr[blk, r0:r0 + L, SSD_W + (SSD_GROUPS + g) * SSD_STATE:
                                 SSD_W + (SSD_GROUPS + g + 1) * SSD_STATE].astype(BF16)
            bts[si, g] = bt_scr[blk, g, :, r0:r0 + L]
            gms[si, g] = jnp.dot(cbf[si, g], bts[si, g].astype(BF16), preferred_element_type=F32)
        for si, g in pairs:
            sts[si, g] = chunk_state(bts[si, g], xs[si, g], ser[si][2], 2 * g)
        for si, g in pairs:
            dtt, cst = ser[si][0], ser[si][1]
            xs_lo = jnp.where(lo, xs[si, g], 0.0).astype(BF16)
            xs_hi = jnp.where(lo, 0.0, xs[si, g]).astype(BF16)
            ms = []
            for h in range(2):
                tot_decay = None
                for d in range(2):
                    j = SSD_HEADS * d + 2 * g + h
                    seg = bcs[si][:, j * LANES:(j + 1) * LANES] - cst[j:j + 1, :]
                    term = jnp.where(masks[d], jnp.exp(seg), 0.0) * dtt[j:j + 1, :]
                    tot_decay = term if tot_decay is None else tot_decay + term
                ms.append((gms[si, g] * tot_decay).astype(BF16))
            ys[si, g] = jnp.dot(jnp.concatenate(ms, axis=1), jnp.concatenate([xs_lo, xs_hi], axis=0),
                                preferred_element_type=F32)
        for g in range(SSD_GROUPS):
            hg = h_scr[g]
            for si in range(nsub):
                r0 = si * L
                hb = hb_scr[blk * nsub + si, g]
                ch = jnp.dot(cbf[si, g], jnp.concatenate([hg.astype(BF16), hb.astype(BF16)], axis=1),
                             preferred_element_type=F32)
                hg = hg * pick2(ser[si][3], 2 * g) + sts[si, g]
                pf = (N_SERIES + g) * LANES
                pb = (N_SERIES + SSD_GROUPS + g) * LANES
                y = (ys[si, g] + jnp.exp(bcs[si][:, pf:pf + LANES]) * ch[:, 0:LANES]
                     + jnp.exp(bcs[si][:, pb:pb + LANES]) * ch[:, LANES:])
                gl = slice(g * LANES, (g + 1) * LANES)
                y = y + dexp_ref[:, gl] * xs[si, g]
                z = main_ref[r0:r0 + L, R_Z + g * LANES:R_Z + (g + 1) * LANES]
                y = y * (z * _sigmoid(z))
                o_ref[r0:r0 + L, SC_W + g * LANES:SC_W + (g + 1) * LANES] = _rms(y, ng_ref[:, gl]).astype(BF16)
            h_scr[g] = hg


def _ssd_mixers(rest, dtb_col, alog_col, dexp, ng, scw, cw, cb, sel, layer, batch, seq, rows):
    t = rest.shape[0]
    nblk = seq // rows
    r8 = rows // SUBLANES

    def blk_of(s, i):
        return jnp.where(s == 0, nblk - 1 - i, i)

    small = lambda r, n: pl.BlockSpec((None, r, n), lambda bi, s, i: (layer, 0, 0))
    return pl.pallas_call(
        functools.partial(_ssd_kernel, rows=rows, nblk=nblk),
        grid=(batch, 2, nblk),
        in_specs=[small(N_SERIES, 1), small(N_SERIES, 1), small(1, SSD_W), small(1, SSD_W),
                  small(3, SC_W), small(3, SSD_XBC), small(1, SSD_XBC),
                  pl.BlockSpec((LANES, SEL_COLS), lambda bi, s, i: (0, 0)),
                  pl.BlockSpec((rows, REST_W), lambda bi, s, i: (bi * nblk + blk_of(s, i), 0)),
                  pl.BlockSpec((SUBLANES, REST_W),
                               lambda bi, s, i: (jnp.maximum((bi * nblk + blk_of(s, i)) * r8 - 1, 0), 0)),
                  pl.BlockSpec((SUBLANES, REST_W),
                               lambda bi, s, i: (jnp.minimum((bi * nblk + blk_of(s, i) + 1) * r8,
                                                             t // SUBLANES - 1), 0))],
        out_specs=pl.BlockSpec((rows, SC_W + SSD_W), lambda bi, s, i: (bi * nblk + s * i, 0)),
        out_shape=jax.ShapeDtypeStruct((t, SC_W + SSD_W), BF16),
        scratch_shapes=[pltpu.VMEM((nblk, rows, SSD_XBC), F32),
                        pltpu.VMEM((nblk, SSD_GROUPS, SSD_STATE, rows), F32),
                        pltpu.VMEM((nblk, N_SERIES, rows), F32),
                        pltpu.VMEM((seq // SSD_CHUNK, SSD_GROUPS, SSD_STATE, LANES), F32),
                        pltpu.VMEM((SSD_GROUPS, SSD_STATE, LANES), F32)],
        compiler_params=pltpu.CompilerParams(dimension_semantics=("arbitrary", "arbitrary", "arbitrary"),
                                             vmem_limit_bytes=VMEM_LIMIT),
        name="ssd_mixers",
    )(dtb_col, alog_col, dexp, ng, scw, cw, cb, sel, rest, rest, rest)


def _ffn_kernel(x_ref, xp_ref, xn_ref, ya_ref, yap_ref, yan_ref, ym_ref, ymp_ref, ymn_ref,
                wout_ref, g_ref, wup_ref, cwg_ref, cwu_ref, cbg_ref, cbu_ref, wd_ref, fg_ref,
                o_ref, slab_scr, hp_scr, act_scr, ua_scr, ub_scr, *, tm, tiles_per_seq, final_norm):
    ti = pl.program_id(0) % tiles_per_seq
    has_prev = ti > 0
    has_next = ti < tiles_per_seq - 1
    e_rows = tm + 2 * CONV_HALO
    seg = e_rows // SUBLANES
    n_slabs = D_MODEL // LANES

    def ext(prev, main, nxt):
        return jnp.concatenate([prev[...], main[...], nxt[...]], axis=0)

    mix = (jnp.dot(ext(yap_ref, ya_ref, yan_ref), wout_ref[0:ATTN_W, :], preferred_element_type=F32)
           + jnp.dot(ext(ymp_ref, ym_ref, ymn_ref), wout_ref[ATTN_W:2 * ATTN_W, :], preferred_element_type=F32))
    xnew = (ext(xp_ref, x_ref, xn_ref) + mix)[HALO - CONV_HALO:HALO - CONV_HALO + e_rows]
    row = lax.broadcasted_iota(jnp.int32, (e_rows, 1), 0)
    valid = (row >= jnp.where(has_prev, 0, CONV_HALO)) & (row < jnp.where(has_next, e_rows, CONV_HALO + tm))
    h = jnp.where(valid, _rms(xnew, g_ref[...]), 0.0)
    o_ref[...] = xnew[CONV_HALO:CONV_HALO + tm]

    for c in range(n_slabs):
        slab_scr[c] = h[:, c * LANES:(c + 1) * LANES]
    for i in range(0, seg, 2):
        blk = [jnp.concatenate([slab_scr[c, pl.ds(i + d, SUBLANES, stride=seg), :] for c in range(n_slabs)], axis=1)
               for d in range(2)]
        hp_scr[SUBLANES * i:SUBLANES * (i + 2), :] = jnp.concatenate(blk, axis=0).astype(BF16)

    def up_proj(j, u_ref):
        hb = hp_scr[...]
        c0 = j * FF_CHUNK
        u_ref[0] = jnp.dot(hb, wup_ref[:, c0:c0 + FF_CHUNK], preferred_element_type=F32)
        u_ref[1] = jnp.dot(hb, wup_ref[:, D_FF + c0:D_FF + c0 + FF_CHUNK], preferred_element_type=F32)

    def conv_rows(u_ref, idx, w):
        last = e_rows - SUBLANES
        mid = (u_ref[idx, 0:last - SUBLANES, :] * w[0:1] + u_ref[idx, SUBLANES:last, :] * w[1:2]
               + u_ref[idx, 2 * SUBLANES:e_rows, :] * w[2:3])
        first = (pltpu.roll(u_ref[idx, last:e_rows, :], 1, 0) * w[0:1] + u_ref[idx, 0:SUBLANES, :] * w[1:2]
                 + u_ref[idx, SUBLANES:2 * SUBLANES, :] * w[2:3])
        end = (u_ref[idx, last - SUBLANES:last, :] * w[0:1] + u_ref[idx, last:e_rows, :] * w[1:2]
               + pltpu.roll(u_ref[idx, 0:SUBLANES, :], SUBLANES - 1, 0) * w[2:3])
        return jnp.concatenate([first, mid, end], axis=0)

    def gate_down(j, u_ref):
        cg = conv_rows(u_ref, 0, cwg_ref[j]) + cbg_ref[j]
        cu = conv_rows(u_ref, 1, cwu_ref[j]) + cbu_ref[j]
        act_scr[:, j * FF_CHUNK:(j + 1) * FF_CHUNK] = ((cg * _sigmoid(cg)) * cu).astype(BF16)

    bufs = (ua_scr, ub_scr)
    up_proj(0, bufs[0])
    for j in range(N_FF_CHUNKS):
        if j + 1 < N_FF_CHUNKS:
            up_proj(j + 1, bufs[(j + 1) % 2])
        gate_down(j, bufs[j % 2])
    down = jnp.dot(act_scr[...], wd_ref[...], preferred_element_type=F32)
    for i in range(seg):
        for c in range(n_slabs):
            slab_scr[c, pl.ds(i, SUBLANES, stride=seg), :] = down[SUBLANES * i:SUBLANES * (i + 1),
                                                                c * LANES:(c + 1) * LANES]
    for c in range(n_slabs):
        o_ref[:, c * LANES:(c + 1) * LANES] += slab_scr[c, CONV_HALO:CONV_HALO + tm, :]
    if final_norm:
        o_ref[...] = _rms(o_ref[...], fg_ref[...])


def _ffn(x, ya, ym, w_out, g, w_up, cwg, cwu, cbg, cbu, wd, fg, layer, seq, tm, final_norm):
    t = x.shape[0]
    tiles_per_seq = seq // tm
    hb = tm // HALO
    nh = t // HALO

    def main(w):
        return pl.BlockSpec((tm, w), lambda i: (i, 0))

    def prev(w):
        return pl.BlockSpec((HALO, w), lambda i: (jnp.maximum(i * hb - 1, 0), 0))

    def nxt(w):
        return pl.BlockSpec((HALO, w), lambda i: (jnp.minimum((i + 1) * hb, nh - 1), 0))

    def resident(shape):
        nd = len(shape)
        return pl.BlockSpec((None,) + shape, lambda i: (layer,) + (0,) * nd,
                            pipeline_mode=pl.Buffered(1))

    e_rows = tm + 2 * CONV_HALO
    assert e_rows % SUBLANES == 0 and (e_rows // SUBLANES) % 8 != 0
    return pl.pallas_call(
        functools.partial(_ffn_kernel, tm=tm, tiles_per_seq=tiles_per_seq, final_norm=final_norm),
        grid=(t // tm,),
        in_specs=[main(D_MODEL), prev(D_MODEL), nxt(D_MODEL),
                  main(ATTN_W), prev(ATTN_W), nxt(ATTN_W),
                  main(SC_W + SSD_W), prev(SC_W + SSD_W), nxt(SC_W + SSD_W),
                  resident((D_MODEL, D_MODEL)),
                  resident((1, D_MODEL)),
                  resident((D_MODEL, 2 * D_FF)),
                  resident((N_FF_CHUNKS, 3, FF_CHUNK)),
                  resident((N_FF_CHUNKS, 3, FF_CHUNK)),
                  resident((N_FF_CHUNKS, 1, FF_CHUNK)),
                  resident((N_FF_CHUNKS, 1, FF_CHUNK)),
                  resident((D_FF, D_MODEL)),
                  pl.BlockSpec((1, D_MODEL), lambda i: (0, 0))],
        out_specs=main(D_MODEL),
        out_shape=jax.ShapeDtypeStruct((t, D_MODEL), F32),
        scratch_shapes=[pltpu.VMEM((D_MODEL // LANES, e_rows, LANES), F32),
                        pltpu.VMEM((e_rows, D_MODEL), BF16),
                        pltpu.VMEM((e_rows, D_FF), BF16),
                        pltpu.VMEM((2, e_rows, FF_CHUNK), F32),
                        pltpu.VMEM((2, e_rows, FF_CHUNK), F32)],
        compiler_params=pltpu.CompilerParams(dimension_semantics=("arbitrary",),
                                             vmem_limit_bytes=VMEM_LIMIT),
        name="ffn",
    )(x, x, x, ya, ya, ya, ym, ym, ym, w_out, g, w_up, cwg, cwu, cbg, cbu, wd, fg)


def kernel(x, positions, norm_mix_g, w_in, lam_q1, lam_k1, lam_q2, lam_k2, subln_g, sc_conv_w, ssd_conv_w,
           ssd_conv_b, ssd_dt_bias, ssd_a_log, ssd_d, ssd_norm_g, w_out, norm_ffn_g, w_up, ffn_conv_w,
           ffn_conv_b, w_down, final_norm_g):
    batch, seq, _ = x.shape
    depth = w_in.shape[0]
    t = batch * seq
    tm = min(512, seq)
    tq = min(1024, seq)
    ssd_rows = min(1024, seq)

    w_in_b = jnp.pad(w_in, ((0, 0), (0, 0), (0, IN_PAD - IN_COLS))).astype(BF16)
    w_out_b = w_out.astype(BF16)
    w_up_b = w_up.astype(BF16)
    wd = w_down.astype(BF16)
    fcw = ffn_conv_w.reshape(depth, 3, 2, N_FF_CHUNKS, FF_CHUNK)
    cwg = fcw[:, :, 0].transpose(0, 2, 1, 3)
    cwu = fcw[:, :, 1].transpose(0, 2, 1, 3)
    fcb = ffn_conv_b.reshape(depth, 2, N_FF_CHUNKS, 1, FF_CHUNK)
    cbg, cbu = fcb[:, 0], fcb[:, 1]
    row3 = lambda a: a.reshape(depth, 1, -1)
    dtb = ssd_dt_bias.reshape(depth, N_SERIES, 1)
    alog = ssd_a_log.reshape(depth, N_SERIES, 1)
    sel = _select_matrix()
    dexp = jnp.repeat(ssd_d, SSD_W // SSD_HEADS, axis=-1).reshape(depth, 1, SSD_W)
    fg = final_norm_g.reshape(1, D_MODEL)

    rope = _rope_tables(positions)
    xf = x.reshape(t, D_MODEL)
    for l in range(depth):
        lam_init = 0.8 - 0.6 * math.exp(-0.3 * l)
        q, k, vt, rest = _in_proj(xf, row3(norm_mix_g), w_in_b, rope, l, tm)
        ya = _attention(q, k, vt, row3(lam_q1), row3(lam_k1), row3(lam_q2), row3(lam_k2),
                        subln_g.reshape(depth, V_DIM, 1), l, lam_init, batch, seq, tq)
        ym = _ssd_mixers(rest, dtb, alog, dexp, row3(ssd_norm_g), sc_conv_w, ssd_conv_w, row3(ssd_conv_b), sel,
                         l, batch, seq, ssd_rows)
        xf = _ffn(xf, ya, ym, w_out_b, row3(norm_ffn_g), w_up_b, cwg, cwu, cbg, cbu, wd, fg,
                  l, seq, tm, l == depth - 1)
    return xf.reshape(batch, seq, D_MODEL)
```
